```python
import math
import jax, jax.numpy as jnp
from jax import lax
import numpy as np

D_MODEL = 1024
BATCH = 32
SEQ = 2048
DEPTH = 1
DEC_BATCH = 128
DEC_SEQ = 8
PAST_LEN = 8192
PAGE_SIZE = 128

N_META = 16
MIX_DIM = D_MODEL
A_HEADS = 8
A_HEAD_DIM = MIX_DIM // 2 // A_HEADS
A_DIM = A_HEADS * A_HEAD_DIM
B_HEADS = 8
B_HEAD_DIM = (MIX_DIM - A_DIM) // B_HEADS
B_DIM = B_HEADS * B_HEAD_DIM
CONV_WIDTH = 4
GDN_CHUNK = 64
IDX_HEADS = 8
IDX_DIM = 64
TOPK_MAX = 256
QBLK = 128
REL_BUCKETS = 32
REL_MAX_DIST = 1024
D_FF = -(-8 * D_MODEL // 768) * 256
EPS = 1e-6
IDX_SCALE = IDX_DIM ** -0.5
ATTN_SCALE = B_HEAD_DIM ** -0.5
PROJ_SIZES = (3 * A_DIM, A_DIM, A_HEADS, A_HEADS, 3 * B_DIM, IDX_HEADS * IDX_DIM, IDX_DIM, IDX_HEADS)
PROJ_DIM = 3 * A_DIM + A_DIM + 2 * A_HEADS + 3 * B_DIM + IDX_HEADS * IDX_DIM + IDX_DIM + IDX_HEADS

kernel_name = "hymba_gdn_dsa_decode_step"


def rms_norm(x, g):
    xf = x.astype(jnp.float32)
    y = xf * lax.rsqrt(jnp.mean(xf * xf, axis=-1, keepdims=True) + EPS)
    return (y * g.astype(jnp.float32)).astype(x.dtype)


def l2norm(x):
    xf = x.astype(jnp.float32)
    return xf * lax.rsqrt(jnp.sum(xf * xf, axis=-1, keepdims=True) + EPS)


def split_proj(h):
    cuts = np.cumsum(PROJ_SIZES)[:-1].tolist()
    return jnp.split(h, cuts, axis=-1)


def mixer_parts(x, g, w_in):
    return split_proj(rms_norm(x, g) @ w_in)


def swiglu(x, w_gate, w_up, w_down):
    return (jax.nn.silu(x @ w_gate) * (x @ w_up)) @ w_down


def rel_bucket(d):
    d = jnp.maximum(d, 0)
    max_exact = REL_BUCKETS // 2
    df = jnp.maximum(d, 1).astype(jnp.float32)
    large = max_exact + (jnp.log(df / max_exact) / math.log(REL_MAX_DIST / max_exact)
                         * (REL_BUCKETS - max_exact)).astype(jnp.int32)
    large = jnp.minimum(large, REL_BUCKETS - 1)
    return jnp.where(d < max_exact, d, large)


def causal_conv(x_ext, w):
    t_len = x_ext.shape[1] - (CONV_WIDTH - 1)
    y = sum(x_ext[:, j:j + t_len] * w[j] for j in range(CONV_WIDTH))
    return jax.nn.silu(y)


def gdn_inputs(qkv, beta_raw, a_raw, a_log, dt_bias):
    bsz, t_len = qkv.shape[:2]
    q, k, v = [t.reshape(bsz, t_len, A_HEADS, A_HEAD_DIM) for t in jnp.split(qkv, 3, axis=-1)]
    q = l2norm(q) * A_HEAD_DIM ** -0.5
    k = l2norm(k)
    v = v.astype(jnp.float32)
    beta = jax.nn.sigmoid(beta_raw.astype(jnp.float32))
    g = -jnp.exp(a_log.astype(jnp.float32)) * jax.nn.softplus(
        a_raw.astype(jnp.float32) + dt_bias.astype(jnp.float32))
    return q, k, v, beta, g


def gdn_chunk(s0, q, k, v, beta, g):
    qh, kh, vh = [jnp.transpose(t, (0, 2, 1, 3)) for t in (q, k, v)]
    beta, g = jnp.transpose(beta, (0, 2, 1)), jnp.transpose(g, (0, 2, 1))
    c = qh.shape[2]
    b = jnp.cumsum(g, axis=-1)
    incl = jnp.tril(jnp.ones((c, c), dtype=bool))
    strict = jnp.tril(jnp.ones((c, c), dtype=bool), -1)
    diff = b[..., :, None] - b[..., None, :]
    decay = jnp.where(incl, jnp.exp(jnp.where(incl, diff, 0.0)), 0.0)
    a_mat = jnp.where(strict, beta[..., :, None] * jnp.einsum("bhik,bhjk->bhij", kh, kh) * decay, 0.0)
    rhs = jnp.concatenate([beta[..., None] * vh, (beta * jnp.exp(b))[..., None] * kh], axis=-1)
    sol = lax.linalg.triangular_solve(jnp.eye(c, dtype=jnp.float32) + a_mat, rhs,
                                      left_side=True, lower=True, unit_diagonal=True)
    u, w = sol[..., :A_HEAD_DIM], sol[..., A_HEAD_DIM:]
    delta = u - jnp.einsum("bhck,bhkv->bhcv", w, s0)
    o = (jnp.exp(b)[..., None] * jnp.einsum("bhck,bhkv->bhcv", qh, s0)
         + jnp.einsum("bhij,bhjv->bhiv", jnp.einsum("bhik,bhjk->bhij", qh, kh) * decay, delta))
    b_last = b[..., -1:]
    s_new = (jnp.exp(b_last)[..., None] * s0
             + jnp.einsum("bhck,bhcv->bhkv", kh * jnp.exp(b_last - b)[..., None], delta))
    return s_new, jnp.transpose(o, (0, 2, 1, 3))


def gdn_prompt(qkv_pre, beta_raw, a_raw, conv_w, a_log, dt_bias):
    bsz = qkv_pre.shape[0]
    x_ext = jnp.pad(qkv_pre, ((0, 0), (CONV_WIDTH - 1, 0), (0, 0)))
    conv_state = x_ext[:, -(CONV_WIDTH - 1):]
    q, k, v, beta, g = gdn_inputs(causal_conv(x_ext, conv_w), beta_raw, a_raw, a_log, dt_bias)
    s0 = jnp.zeros((bsz, A_HEADS, A_HEAD_DIM, A_HEAD_DIM), jnp.float32)
    s, o_meta = gdn_chunk(s0, q[:, :N_META], k[:, :N_META], v[:, :N_META], beta[:, :N_META], g[:, :N_META])

    def to_chunks(t):
        t = t[:, N_META:]
        n = t.shape[1] // GDN_CHUNK
        return jnp.moveaxis(t.reshape((t.shape[0], n, GDN_CHUNK) + t.shape[2:]), 1, 0)

    def step(state, xs):
        qc, kc, vc, bc, gc = xs
        return gdn_chunk(state, qc, kc, vc, bc, gc)

    s, o_real = lax.scan(step, s, (to_chunks(q), to_chunks(k), to_chunks(v), to_chunks(beta), to_chunks(g)))
    o_real = jnp.moveaxis(o_real, 0, 1)
    o_real = o_real.reshape((bsz, -1, A_HEADS, A_HEAD_DIM))
    return jnp.concatenate([o_meta, o_real], axis=1), conv_state, s


def gdn_sample(qkv_pre, beta_raw, a_raw, conv_state, s0, conv_w, a_log, dt_bias):
    x_ext = jnp.concatenate([conv_state.astype(qkv_pre.dtype), qkv_pre], axis=1)
    new_conv = x_ext[:, -(CONV_WIDTH - 1):]
    q, k, v, beta, g = gdn_inputs(causal_conv(x_ext, conv_w), beta_raw, a_raw, a_log, dt_bias)
    s_new, o = gdn_chunk(s0.astype(jnp.float32), q, k, v, beta, g)
    return o, new_conv, s_new


def gdn_out(o, z, o_norm_g):
    bsz, t_len = z.shape[:2]
    zh = z.reshape(bsz, t_len, A_HEADS, A_HEAD_DIM).astype(jnp.float32)
    y = rms_norm(o, o_norm_g) * jax.nn.silu(zh)
    return y.reshape(bsz, t_len, A_DIM).astype(z.dtype)


def dsa_query_block(q, qi, w, q_pos, kidx, topk, rel_bias, fetch):
    f32 = jnp.float32
    key_pos = jnp.arange(kidx.shape[0], dtype=jnp.int32)
    rel = jnp.einsum("qhd,sd->qhs", qi.astype(f32), kidx.astype(f32)) * IDX_SCALE
    score = jnp.einsum("qh,qhs->qs", w.astype(f32) * IDX_HEADS ** -0.5, jax.nn.relu(rel))
    score = jnp.where(key_pos[None, :] <= q_pos[:, None], score, -jnp.inf)
    top_val, sel = lax.top_k(score, topk)
    valid = jnp.isfinite(top_val)
    k_sel, v_sel = fetch(sel)
    logits = jnp.einsum("qhd,qkhd->qhk", q.astype(f32), k_sel.astype(f32)) * ATTN_SCALE
    bias = jnp.transpose(rel_bias.astype(f32)[rel_bucket(q_pos[:, None] - sel)], (0, 2, 1))
    logits = jnp.where(valid[:, None, :], logits + bias, -jnp.inf)
    p = jax.nn.softmax(logits, axis=-1)
    return jnp.einsum("qhk,qkhd->qhd", p, v_sel.astype(f32)).astype(q.dtype)


def dsa_prompt(qkv_b, q_idx, k_idx, w_idx, rel_bias, topk):
    bsz, seq_len = qkv_b.shape[:2]
    q, k, v = [t.reshape(bsz, seq_len, B_HEADS, B_HEAD_DIM) for t in jnp.split(qkv_b, 3, axis=-1)]
    qi = q_idx.reshape(bsz, seq_len, IDX_HEADS, IDX_DIM)
    n_blk = -(-seq_len // QBLK)
    pad = n_blk * QBLK - seq_len

    def blocks(t):
        t = jnp.pad(t, ((0, 0), (0, pad)) + ((0, 0),) * (t.ndim - 2))
        return t.reshape((bsz * n_blk, QBLK) + t.shape[2:])

    b_idx = jnp.repeat(jnp.arange(bsz, dtype=jnp.int32), n_blk)
    q_start = jnp.tile(jnp.arange(n_blk, dtype=jnp.int32) * QBLK, bsz)

    def one_block(xs):
        qx, qix, wx, bi, st = xs

        def fetch(sel):
            return k[bi, sel], v[bi, sel]

        return dsa_query_block(qx, qix, wx, st + jnp.arange(QBLK, dtype=jnp.int32), k_idx[bi], topk, rel_bias, fetch)

    out = lax.map(one_block, (blocks(q), blocks(qi), blocks(w_idx), b_idx, q_start))
    out = out.reshape(bsz, n_blk * QBLK, B_HEADS, B_HEAD_DIM)[:, :seq_len]
    return out, k, v


def dsa_sample(qkv_b, q_idx, k_idx, w_idx, cache_k, cache_v, cache_kidx, page_table, layer, rel_bias, topk):
    bsz, n_new = qkv_b.shape[:2]
    past_len = page_table.shape[1] * PAGE_SIZE
    q, k, v = [t.reshape(bsz, n_new, B_HEADS, B_HEAD_DIM) for t in jnp.split(qkv_b, 3, axis=-1)]
    qi = q_idx.reshape(bsz, n_new, IDX_HEADS, IDX_DIM)
    q_pos = past_len + jnp.arange(n_new, dtype=jnp.int32)

    def one_seq(xs):
        qx, qix, wx, kx, vx, kix, pt = xs
        past_kidx = cache_kidx[layer, pt].reshape(past_len, IDX_DIM).astype(kix.dtype)
        keys = jnp.concatenate([past_kidx, kix], axis=0)

        def fetch(sel):
            in_past = (sel < past_len)[..., None, None]
            sp = jnp.minimum(sel, past_len - 1)
            phys, off = pt[sp // PAGE_SIZE], sp % PAGE_SIZE
            sn = jnp.clip(sel - past_len, 0, n_new - 1)
            k_sel = jnp.where(in_past, cache_k[layer, phys, off].astype(kx.dtype), kx[sn])
            v_sel = jnp.where(in_past, cache_v[layer, phys, off].astype(vx.dtype), vx[sn])
            return k_sel, v_sel

        return dsa_query_block(qx, qix, wx, q_pos, keys, topk, rel_bias, fetch)

    out = lax.map(one_seq, (q, qi, w_idx, k, v, k_idx, page_table))
    return out, k, v


def layer_out(x, o_a, z_a, o_norm_g, o_b, w_out, g_ffn, w_gate, w_up, w_down):
    bsz, t_len = x.shape[:2]
    mix = jnp.concatenate([gdn_out(o_a, z_a, o_norm_g), o_b.reshape(bsz, t_len, B_DIM)], axis=-1)
    h = x + mix @ w_out
    return h + swiglu(rms_norm(h, g_ffn), w_gate, w_up, w_down)


def setup_inputs(seed: int = 0) -> dict:
    key = jax.random.key(seed)
    ks = jax.random.split(key, 24)
    f32 = jnp.float32
    n_pages = PAST_LEN // PAGE_SIZE
    n_used = DEC_BATCH * n_pages
    n_pool = n_used + n_used // 4

    def nrm(k, shape, s):
        return s * jax.random.normal(k, shape, f32)

    dt = jax.random.uniform(ks[13], (DEPTH, A_HEADS), f32, 0.001, 0.1)
    return {
        "x_prompt": nrm(ks[0], (BATCH, SEQ, D_MODEL), 1.0),
        "x_sample": nrm(ks[1], (DEC_BATCH, DEC_SEQ, D_MODEL), 1.0),
        "cache_k": nrm(ks[2], (DEPTH, n_pool, PAGE_SIZE, B_HEADS, B_HEAD_DIM), 1.0),
        "cache_v": nrm(ks[3], (DEPTH, n_pool, PAGE_SIZE, B_HEADS, B_HEAD_DIM), 1.0),
        "cache_kidx": nrm(ks[4], (DEPTH, n_pool, PAGE_SIZE, IDX_DIM), 1.0),
        "state_conv": nrm(ks[5], (DEPTH, DEC_BATCH, CONV_WIDTH - 1, 3 * A_DIM), 1.0),
        "state_ssm": nrm(ks[6], (DEPTH, DEC_BATCH, A_HEADS, A_HEAD_DIM, A_HEAD_DIM), 0.1),
        "page_table": jax.random.permutation(ks[7], n_pool)[:n_used].reshape(DEC_BATCH, n_pages).astype(jnp.int32),
        "meta_tokens": nrm(ks[8], (N_META, D_MODEL), 1.0),
        "norm_mix": 1.0 + nrm(ks[9], (DEPTH, D_MODEL), 0.02),
        "w_in": nrm(ks[10], (DEPTH, D_MODEL, PROJ_DIM), D_MODEL ** -0.5),
        "conv_w": nrm(ks[11], (DEPTH, CONV_WIDTH, 3 * A_DIM), CONV_WIDTH ** -0.5),
        "a_log": jnp.log(jax.random.uniform(ks[12], (DEPTH, A_HEADS), f32, 1.0, 16.0)),
        "dt_bias": dt + jnp.log(-jnp.expm1(-dt)),
        "o_norm": 1.0 + nrm(ks[14], (DEPTH, A_HEAD_DIM), 0.02),
        "w_out": nrm(ks[15], (DEPTH, MIX_DIM, D_MODEL), MIX_DIM ** -0.5),
        "rel_bias": nrm(ks[16], (REL_BUCKETS, B_HEADS), 0.5),
        "norm_ffn": 1.0 + nrm(ks[17], (DEPTH, D_MODEL), 0.02),
        "w_gate": nrm(ks[18], (DEPTH, D_MODEL, D_FF), D_MODEL ** -0.5),
        "w_up": nrm(ks[19], (DEPTH, D_MODEL, D_FF), D_MODEL ** -0.5),
        "w_down": nrm(ks[20], (DEPTH, D_FF, D_MODEL), D_FF ** -0.5),
        "norm_final": 1.0 + nrm(ks[21], (D_MODEL,), 0.02),
    }


def reference(x_prompt, x_sample, cache_k, cache_v, cache_kidx, state_conv, state_ssm, page_table,
              meta_tokens, norm_mix, w_in, conv_w, a_log, dt_bias, o_norm, w_out, rel_bias,
              norm_ffn, w_gate, w_up, w_down, norm_final):
    bsz = x_prompt.shape[0]
    past_len = page_table.shape[1] * PAGE_SIZE
    topk_p = min(TOPK_MAX, x_prompt.shape[1] // 4)
    topk_s = min(TOPK_MAX, (past_len + x_sample.shape[1]) // 4)
    meta = jnp.broadcast_to(meta_tokens.astype(x_prompt.dtype)[None], (bsz, N_META, D_MODEL))
    xp = jnp.concatenate([meta, x_prompt], axis=1)
    xs = x_sample
    k_p, v_p, ki_p, conv_p, ssm_p = [], [], [], [], []
    k_s, v_s, ki_s, conv_s, ssm_s = [], [], [], [], []
    for l in range(DEPTH):
        qkv_a, z_a, beta_a, alpha_a, qkv_b, q_idx, k_idx, w_idx = mixer_parts(xp, norm_mix[l], w_in[l])
        o_a, c_new, s_new = gdn_prompt(qkv_a, beta_a, alpha_a, conv_w[l], a_log[l], dt_bias[l])
        o_b, k_new, v_new = dsa_prompt(qkv_b, q_idx, k_idx, w_idx, rel_bias, topk_p)
        xp = layer_out(xp, o_a, z_a, o_norm[l], o_b, w_out[l], norm_ffn[l], w_gate[l], w_up[l], w_down[l])
        k_p.append(k_new); v_p.append(v_new); ki_p.append(k_idx)
        conv_p.append(c_new); ssm_p.append(s_new.astype(x_prompt.dtype))
        qkv_a, z_a, beta_a, alpha_a, qkv_b, q_idx, k_idx, w_idx = mixer_parts(xs, norm_mix[l], w_in[l])
        o_a, c_new, s_new = gdn_sample(qkv_a, beta_a, alpha_a, state_conv[l], state_ssm[l],
                                       conv_w[l], a_log[l], dt_bias[l])
        o_b, k_new, v_new = dsa_sample(qkv_b, q_idx, k_idx, w_idx, cache_k, cache_v, cache_kidx,
                                       page_table, l, rel_bias, topk_s)
        xs = layer_out(xs, o_a, z_a, o_norm[l], o_b, w_out[l], norm_ffn[l], w_gate[l], w_up[l], w_down[l])
        k_s.append(k_new); v_s.append(v_new); ki_s.append(k_idx)
        conv_s.append(c_new); ssm_s.append(s_new.astype(x_sample.dtype))
    y_prompt = rms_norm(xp, norm_final)[:, N_META:]
    y_sample = rms_norm(xs, norm_final)
    return (y_prompt, y_sample,
            jnp.stack(k_p), jnp.stack(v_p), jnp.stack(ki_p), jnp.stack(conv_p), jnp.stack(ssm_p),
            jnp.stack(k_s), jnp.stack(v_s), jnp.stack(ki_s), jnp.stack(conv_s), jnp.stack(ssm_s))
```

```python
import functools
import math

import jax
import jax.numpy as jnp
from jax import lax
from jax.experimental import pallas as pl
from jax.experimental.pallas import tpu as pltpu

F32 = jnp.float32
BF16 = jnp.bfloat16
I32 = jnp.int32
HI = lax.Precision.HIGHEST

D_MODEL = 1024
N_META = 16
HEADS = 8
HD = 64
A_DIM = HEADS * HD
B_DIM = HEADS * HD
CONV_WIDTH = 4
GDN_CHUNK = 64
TOPK_MAX = 256
PAGE = 128
REL_BUCKETS = 32
REL_MAX_DIST = 1024
D_FF = 2816
EPS = 1e-6
LANES = 128
SUBLANES = 8
FRONT = 128
NEG = -1e30
INT_MIN = -2 ** 31
VMEM_LIMIT = 56 * 1024 * 1024

S_KIDX, S_BETA, S_A, S_W = 0, 64, 72, 80


def _mm(a, b):
    return jnp.dot(a, b, preferred_element_type=F32)


def _mm_hi(a, b):
    return jnp.dot(a, b, preferred_element_type=F32, precision=HI)


def _nt(a, b, precision=None):
    return lax.dot_general(a, b, (((1,), (1,)), ((), ())), preferred_element_type=F32, precision=precision)


def _tn(a, b, precision=None):
    return lax.dot_general(a, b, (((0,), (0,)), ((), ())), preferred_element_type=F32, precision=precision)


def _const_spec(shape):
    nd = len(shape)
    return pl.BlockSpec(shape, lambda *_: (0,) * nd, pipeline_mode=pl.Buffered(1))


def _silu(x):
    return x * jax.nn.sigmoid(x)


def _proj_body(x_ref, g_ref, wa_ref, wz_ref, wb_ref, wqi_ref, ws_ref, wst_ref,
               qkva_ref, z_ref, qb_ref, k_ref, v_ref, kbf_ref, vbf_ref, qi_ref, small_ref, smallbf_ref, smallt_ref):
    x = x_ref[...]
    ms = jnp.mean(x * x, axis=-1, keepdims=True)
    h = (x * lax.rsqrt(ms + EPS) * g_ref[...]).astype(BF16)
    qkva_ref[...] = _mm(h, wa_ref[...])
    z_ref[...] = _mm(h, wz_ref[...])
    qkvb = _mm(h, wb_ref[...])
    qb_ref[...] = (qkvb[:, :B_DIM] * 0.125).astype(BF16)
    k = qkvb[:, B_DIM:2 * B_DIM]
    v = qkvb[:, 2 * B_DIM:]
    k_ref[...] = k
    v_ref[...] = v
    kbf_ref[...] = k.astype(BF16)
    vbf_ref[...] = v.astype(BF16)
    qi_ref[...] = (_mm(h, wqi_ref[...]) * 0.125).astype(BF16)
    s = _mm(h, ws_ref[...])
    small_ref[...] = s
    smallbf_ref[...] = s.astype(BF16)
    smallt_ref[...] = _nt(wst_ref[...], h)


def _split_w_in(w_in):
    c = 0
    parts = []
    for n in (3 * A_DIM, A_DIM, HEADS, HEADS, 3 * B_DIM, HEADS * HD, HD, HEADS):
        parts.append(w_in[:, c:c + n])
        c += n
    w_qkva, w_z, w_beta, w_a, w_qkvb, w_qi, w_kidx, w_w = parts
    pad = jnp.zeros((w_in.shape[0], LANES - (HD + 3 * HEADS)), w_in.dtype)
    w_small = jnp.concatenate([w_kidx, w_beta, w_a, w_w, pad], axis=1)
    bf = lambda t: t.astype(BF16)
    return bf(w_qkva), bf(w_z), bf(w_qkvb), bf(w_qi), bf(w_small), bf(w_small.T)


def _proj(x2d, g, w_parts, tm):
    n = x2d.shape[0]
    assert n % tm == 0 and tm % LANES == 0
    wa, wz, wb, wqi, ws, wst = w_parts
    row = lambda w: pl.BlockSpec((tm, w), lambda i: (i, 0))
    outs = [
        (jax.ShapeDtypeStruct((n, 3 * A_DIM), F32), row(3 * A_DIM)),
        (jax.ShapeDtypeStruct((n, A_DIM), F32), row(A_DIM)),
        (jax.ShapeDtypeStruct((n, B_DIM), BF16), row(B_DIM)),
        (jax.ShapeDtypeStruct((n, B_DIM), F32), row(B_DIM)),
        (jax.ShapeDtypeStruct((n, B_DIM), F32), row(B_DIM)),
        (jax.ShapeDtypeStruct((n, B_DIM), BF16), row(B_DIM)),
        (jax.ShapeDtypeStruct((n, B_DIM), BF16), row(B_DIM)),
        (jax.ShapeDtypeStruct((n, HEADS * HD), BF16), row(HEADS * HD)),
        (jax.ShapeDtypeStruct((n, LANES), F32), row(LANES)),
        (jax.ShapeDtypeStruct((n, LANES), BF16), row(LANES)),
        (jax.ShapeDtypeStruct((LANES, n), F32), pl.BlockSpec((LANES, tm), lambda i: (0, i))),
    ]
    return pl.pallas_call(
        _proj_body,
        grid=(n // tm,),
        in_specs=[row(D_MODEL), _const_spec((1, D_MODEL)), _const_spec(wa.shape), _const_spec(wz.shape),
                  _const_spec(wb.shape), _const_spec(wqi.shape), _const_spec(ws.shape), _const_spec(wst.shape)],
        out_specs=[o[1] for o in outs],
        out_shape=[o[0] for o in outs],
        compiler_params=pltpu.CompilerParams(dimension_semantics=("parallel",), vmem_limit_bytes=VMEM_LIMIT),
        name="proj",
    )(x2d, g.reshape(1, D_MODEL), wa, wz, wb, wqi, ws, wst)


def _gdn_prep(qkv, tail, conv_w, small, small_t, a_log_row, dt_row, a_log_col, dt_col, valid_col, valid_row, chunk):
    rows = qkv.shape[0]
    n_chunks = rows // chunk
    ys = []
    for c in range(n_chunks):
        ext = jnp.concatenate([tail[c], qkv[c * chunk:(c + 1) * chunk]], axis=0)
        y = sum(ext[SUBLANES - (CONV_WIDTH - 1) + j: SUBLANES - (CONV_WIDTH - 1) + j + chunk] * conv_w[j:j + 1]
                for j in range(CONV_WIDTH))
        ys.append(_silu(y))
    y = ys[0] if n_chunks == 1 else jnp.concatenate(ys, axis=0)
    beta_c = jnp.where(valid_col, jax.nn.sigmoid(small), 0.0)
    beta_r = jnp.where(valid_row, jax.nn.sigmoid(small_t), 0.0)
    g_c = jnp.where(valid_col, -jnp.exp(a_log_row) * jax.nn.softplus(small + dt_row), 0.0)
    g_r = jnp.where(valid_row, -jnp.exp(a_log_col) * jax.nn.softplus(small_t + dt_col), 0.0)
    ri = lax.broadcasted_iota(I32, (rows, rows), 0)
    ci = lax.broadcasted_iota(I32, (rows, rows), 1)
    same = (ri // chunk) == (ci // chunk)
    low = jnp.where(same & (ci <= ri), 1.0, 0.0).astype(F32)
    b_c = _mm_hi(low, g_c)
    b_r = _nt(g_r, low, precision=HI)
    return y, beta_c, b_c, b_r


def _gdn_head(qh, kh, vh, beta_c, b_c, b_r, s0, o_norm, zh, chunk):
    c = chunk
    qn = qh * lax.rsqrt(jnp.sum(qh * qh, axis=-1, keepdims=True) + EPS) * (HD ** -0.5)
    kn = kh * lax.rsqrt(jnp.sum(kh * kh, axis=-1, keepdims=True) + EPS)
    ri = lax.broadcasted_iota(I32, (c, c), 0)
    ci = lax.broadcasted_iota(I32, (c, c), 1)
    incl = ci <= ri
    strict = ci < ri
    decay = jnp.where(incl, jnp.exp(jnp.where(incl, b_c - b_r, 0.0)), 0.0)
    kk = _nt(kn, kn, precision=HI)
    qk = _nt(qn, kn, precision=HI)
    a_mat = jnp.where(strict, beta_c * kk * decay, 0.0)
    eb = jnp.exp(b_c)
    y = jnp.concatenate([beta_c * vh, (beta_c * eb) * kn], axis=1)
    bk = -a_mat
    n_levels = int(math.log2(c))
    for lvl in range(n_levels):
        y = y + _mm_hi(bk, y)
        if lvl + 1 < n_levels:
            bk = _mm_hi(bk, bk)
    u, w = y[:, :HD], y[:, HD:]
    delta = u - _mm_hi(w, s0)
    o = eb * _mm_hi(qn, s0) + _mm_hi(qk * decay, delta)
    b_last = b_c[c - 1:c, :]
    s_new = jnp.exp(b_last) * s0 + _tn(kn * jnp.exp(b_last - b_c), delta, precision=HI)
    on = o * lax.rsqrt(jnp.mean(o * o, axis=-1, keepdims=True) + EPS) * o_norm
    return s_new, on * _silu(zh)


def _gdn_chunk_all_heads(y, beta_c, b_c, b_r, z, o_norm, r0, chunk, get_s0, put_s, put_o):
    for h in range(HEADS):
        sl = slice(r0, r0 + chunk)
        qh = y[sl, h * HD:(h + 1) * HD]
        kh = y[sl, A_DIM + h * HD:A_DIM + (h + 1) * HD]
        vh = y[sl, 2 * A_DIM + h * HD:2 * A_DIM + (h + 1) * HD]
        bc = beta_c[sl, S_BETA + h:S_BETA + h + 1]
        bcol = b_c[sl, S_A + h:S_A + h + 1]
        brow = b_r[S_A + h:S_A + h + 1, r0:r0 + chunk]
        s_new, out = _gdn_head(qh, kh, vh, bc, bcol, brow, get_s0(h), o_norm, z[sl, h * HD:(h + 1) * HD], chunk)
        put_s(h, s_new)
        put_o(h, out)


def _gdn_prompt_body(qkv_ref, z_ref, small_ref, smallt_ref, convw_ref, alr_ref, dtr_ref, alc_ref, dtc_ref, onorm_ref,
                     mix_ref, s_ref, tail_scr, *, rows, lo):
    t = pl.program_id(1)

    @pl.when(t == 0)
    def _():
        s_ref[...] = jnp.zeros_like(s_ref)
        tail_scr[...] = jnp.zeros_like(tail_scr)

    chunk = GDN_CHUNK
    qkv = qkv_ref[0]
    tails = [tail_scr[...]] + [qkv[c * chunk - SUBLANES:c * chunk] for c in range(1, rows // chunk)]
    pos_c = t * rows + lax.broadcasted_iota(I32, (rows, LANES), 0)
    pos_r = t * rows + lax.broadcasted_iota(I32, (LANES, rows), 1)
    y, beta_c, b_c, b_r = _gdn_prep(qkv, tails, convw_ref[...], small_ref[0], smallt_ref[...],
                                    alr_ref[...], dtr_ref[...], alc_ref[...], dtc_ref[...],
                                    pos_c >= lo, pos_r >= lo, chunk)
    tail_scr[...] = qkv[rows - SUBLANES:rows]
    z = z_ref[0]
    o_norm = onorm_ref[...]
    for c in range(rows // chunk):
        def put_s(h, s):
            s_ref[0, h] = s

        def put_o(h, o, c=c):
            mix_ref[0, c * chunk:(c + 1) * chunk, h * HD:(h + 1) * HD] = o.astype(mix_ref.dtype)

        _gdn_chunk_all_heads(y, beta_c, b_c, b_r, z, o_norm, c * chunk, chunk, lambda h: s_ref[0, h], put_s, put_o)


def _gate_params(a_log, dt_bias):
    row = lambda v: jnp.zeros((1, LANES), F32).at[0, S_A:S_A + HEADS].set(v.astype(F32))
    return row(a_log), row(dt_bias), row(a_log).T, row(dt_bias).T


def _gdn_prompt(qkva, z, small, small_t, conv_w, a_log, dt_bias, o_norm, bsz, t_pad, lo):
    rows = LANES
    nt = t_pad // rows
    alr, dtr, alc, dtc = _gate_params(a_log, dt_bias)
    body = functools.partial(_gdn_prompt_body, rows=rows, lo=lo)
    mix, s = pl.pallas_call(
        body,
        grid=(bsz, nt),
        in_specs=[
            pl.BlockSpec((1, rows, 3 * A_DIM), lambda b, t: (b, t, 0)),
            pl.BlockSpec((1, rows, A_DIM), lambda b, t: (b, t, 0)),
            pl.BlockSpec((1, rows, LANES), lambda b, t: (b, t, 0)),
            pl.BlockSpec((LANES, rows), lambda b, t: (0, b * nt + t)),
            _const_spec((CONV_WIDTH, 3 * A_DIM)), _const_spec((1, LANES)), _const_spec((1, LANES)),
            _const_spec((LANES, 1)), _const_spec((LANES, 1)), _const_spec((1, HD)),
        ],
        out_specs=[
            pl.BlockSpec((1, rows, A_DIM), lambda b, t: (b, jnp.maximum(t - 1, 0), 0)),
            pl.BlockSpec((1, HEADS, HD, HD), lambda b, t: (b, 0, 0, 0)),
        ],
        out_shape=[jax.ShapeDtypeStruct((bsz, t_pad - FRONT, A_DIM), BF16),
                   jax.ShapeDtypeStruct((bsz, HEADS, HD, HD), F32)],
        scratch_shapes=[pltpu.VMEM((SUBLANES, 3 * A_DIM), F32)],
        compiler_params=pltpu.CompilerParams(dimension_semantics=("parallel", "arbitrary"),
                                             vmem_limit_bytes=VMEM_LIMIT),
        name="gdn_prompt",
    )(qkva.reshape(bsz, t_pad, 3 * A_DIM), z.reshape(bsz, t_pad, A_DIM), small.reshape(bsz, t_pad, LANES), small_t,
      conv_w, alr, dtr, alc, dtc, o_norm.reshape(1, HD))
    return mix, s


def _gdn_sample_body(qkv_ref, z_ref, small_ref, smallt_ref, conv0_ref, s0_ref, convw_ref, alr_ref, dtr_ref, alc_ref,
                     dtc_ref, onorm_ref, mix_ref, s_ref, *, chunk):
    rows = LANES
    n_seq = rows // chunk
    qkv = qkv_ref[...]
    tails = [conv0_ref[i] for i in range(n_seq)]
    y, beta_c, b_c, b_r = _gdn_prep(qkv, tails, convw_ref[...], small_ref[...], smallt_ref[...],
                                    alr_ref[...], dtr_ref[...], alc_ref[...], dtc_ref[...], True, True, chunk)
    z = z_ref[...]
    o_norm = onorm_ref[...]
    for i in range(n_seq):
        def put_s(h, s, i=i):
            s_ref[i, h] = s

        def put_o(h, o, i=i):
            mix_ref[i * chunk:(i + 1) * chunk, h * HD:(h + 1) * HD] = o.astype(mix_ref.dtype)

        _gdn_chunk_all_heads(y, beta_c, b_c, b_r, z, o_norm, i * chunk, chunk, lambda h, i=i: s0_ref[i, h], put_s, put_o)


def _gdn_sample(qkva, z, small, small_t, state_conv, state_ssm, conv_w, a_log, dt_bias, o_norm, n_new):
    n = qkva.shape[0]
    bsz = n // n_new
    rows = LANES
    n_seq = rows // n_new
    assert n_new == SUBLANES and n % rows == 0
    alr, dtr, alc, dtc = _gate_params(a_log, dt_bias)
    conv0 = jnp.pad(state_conv.astype(F32), ((0, 0), (SUBLANES - (CONV_WIDTH - 1), 0), (0, 0)))
    body = functools.partial(_gdn_sample_body, chunk=n_new)
    mix, s = pl.pallas_call(
        body,
        grid=(n // rows,),
        in_specs=[
            pl.BlockSpec((rows, 3 * A_DIM), lambda g: (g, 0)),
            pl.BlockSpec((rows, A_DIM), lambda g: (g, 0)),
            pl.BlockSpec((rows, LANES), lambda g: (g, 0)),
            pl.BlockSpec((LANES, rows), lambda g: (0, g)),
            pl.BlockSpec((n_seq, SUBLANES, 3 * A_DIM), lambda g: (g, 0, 0)),
            pl.BlockSpec((n_seq, HEADS, HD, HD), lambda g: (g, 0, 0, 0)),
            _const_spec((CONV_WIDTH, 3 * A_DIM)), _const_spec((1, LANES)), _const_spec((1, LANES)),
            _const_spec((LANES, 1)), _const_spec((LANES, 1)), _const_spec((1, HD)),
        ],
        out_specs=[
            pl.BlockSpec((rows, A_DIM), lambda g: (g, 0)),
            pl.BlockSpec((n_seq, HEADS, HD, HD), lambda g: (g, 0, 0, 0)),
        ],
        out_shape=[jax.ShapeDtypeStruct((n, A_DIM), BF16), jax.ShapeDtypeStruct((bsz, HEADS, HD, HD), F32)],
        compiler_params=pltpu.CompilerParams(dimension_semantics=("parallel",), vmem_limit_bytes=VMEM_LIMIT),
        name="gdn_sample",
    )(qkva, z, small, small_t, conv0, state_ssm.astype(F32), conv_w, alr, dtr, alc, dtc, o_norm.reshape(1, HD))
    return mix, s


def _rel_bucket(d):
    d = jnp.maximum(d, 0)
    max_exact = REL_BUCKETS // 2
    df = jnp.maximum(d, 1).astype(F32)
    large = max_exact + (jnp.log(df / max_exact) / math.log(REL_MAX_DIST / max_exact)
                         * (REL_BUCKETS - max_exact)).astype(I32)
    large = jnp.minimum(large, REL_BUCKETS - 1)
    return jnp.where(d < max_exact, d, large)


def _bias_lookup(bucket, rb_ref, h):
    acc = jnp.zeros(bucket.shape, F32)
    for b in range(REL_BUCKETS):
        acc = jnp.where(bucket == b, rb_ref[b, h], acc)
    return acc


def _bias_tab_body(rb_ref, out_ref):
    dlt = pl.program_id(0)
    i = lax.broadcasted_iota(I32, (LANES, LANES), 0)
    j = lax.broadcasted_iota(I32, (LANES, LANES), 1)
    bucket = _rel_bucket(dlt * LANES + i - j)
    for h in range(HEADS):
        out_ref[h, 0] = _bias_lookup(bucket, rb_ref, h)


def _bias_tab(rel_bias, n_tiles):
    return pl.pallas_call(
        _bias_tab_body,
        grid=(n_tiles,),
        in_specs=[pl.BlockSpec(memory_space=pltpu.SMEM)],
        out_specs=pl.BlockSpec((HEADS, 1, LANES, LANES), lambda t: (0, t, 0, 0)),
        out_shape=jax.ShapeDtypeStruct((HEADS, n_tiles, LANES, LANES), F32),
        name="bias_tab",
    )(rel_bias.astype(F32))


def _bias_sample_body(rb_ref, out_ref, *, past_len):
    blk = pl.program_id(0)
    q = lax.broadcasted_iota(I32, (SUBLANES, LANES), 0)
    s = blk * LANES + lax.broadcasted_iota(I32, (SUBLANES, LANES), 1)
    bucket = _rel_bucket(past_len + q - s)
    for h in range(HEADS):
        out_ref[h * SUBLANES:(h + 1) * SUBLANES, :] = _bias_lookup(bucket, rb_ref, h)


def _bias_sample(rel_bias, past_len, n_blocks):
    return pl.pallas_call(
        functools.partial(_bias_sample_body, past_len=past_len),
        grid=(n_blocks,),
        in_specs=[pl.BlockSpec(memory_space=pltpu.SMEM)],
        out_specs=pl.BlockSpec((HEADS * SUBLANES, LANES), lambda t: (0, t)),
        out_shape=jax.ShapeDtypeStruct((HEADS * SUBLANES, n_blocks * LANES), F32),
        name="bias_sample",
    )(rel_bias.astype(F32))


def _sort_key(score, admissible):
    score = jnp.where(score == 0.0, 0.0, score)
    bits = pltpu.bitcast(score, I32)
    key = jnp.where(bits < 0, bits ^ 0x7FFFFFFF, bits)
    return jnp.where(admissible, key, INT_MIN)


def _kth_largest_key(count_ge, rows, topk):
    t0 = jnp.where(count_ge(jnp.zeros((rows, LANES), I32)) >= topk, 0, INT_MIN).astype(I32)

    def bit_step(i, t):
        cand = t + lax.shift_left(jnp.int32(1), 30 - i)
        return jnp.where(count_ge(cand) >= topk, cand, t)

    return lax.fori_loop(0, 31, bit_step, t0)


def _dsa_prompt_body(qb_ref, qi_ref, small_ref, kbf_ref, vbf_ref, kidx_ref, bias_ref, out_ref,
                     keys_scr, mbias_scr, logit_scr, *, lo, topk):
    j = pl.program_id(1)

    @pl.when(j > 0)
    def _():
        tq = LANES
        nkb = j + 1
        t_pos = j * tq + lax.broadcasted_iota(I32, (tq, LANES), 0)
        lane = lax.broadcasted_iota(I32, (tq, LANES), 1)
        w = small_ref[0][:, S_W:S_W + HEADS] * (HEADS ** -0.5)
        qi = qi_ref[0]

        def blk(jb):
            return pl.ds(pl.multiple_of(jb * LANES, LANES), LANES)

        def score_blk(jb, carry):
            kx = kidx_ref[0, blk(jb), :][:, S_KIDX:S_KIDX + HD]
            acc = jnp.zeros((tq, LANES), F32)
            for h in range(HEADS):
                rel = _nt(qi[:, h * HD:(h + 1) * HD], kx)
                acc = acc + jnp.maximum(rel, 0.0) * w[:, h:h + 1]
            s_pos = jb * LANES + lane
            keys_scr[:, blk(jb)] = _sort_key(acc, (s_pos <= t_pos) & (s_pos >= lo))
            return carry

        lax.fori_loop(0, nkb, score_blk, 0)

        def count_where(pred):
            def body(jb, cnt):
                return cnt + jnp.where(pred(keys_scr[:, blk(jb)]), 1.0, 0.0)
            cnt = lax.fori_loop(0, nkb, body, jnp.zeros((tq, LANES), F32))
            return jnp.sum(cnt, axis=1, keepdims=True)

        thr = _kth_largest_key(lambda cand: count_where(lambda key: key >= cand), tq, topk)
        need = topk - count_where(lambda key: key > thr)
        tri = jnp.where(lax.broadcasted_iota(I32, (LANES, LANES), 0) <= lane, 1.0, 0.0).astype(BF16)

        def mask_blk(jb, running):
            key = keys_scr[:, blk(jb)]
            eq = (key == thr) & (key != INT_MIN)
            pref = _mm(jnp.where(eq, 1.0, 0.0).astype(BF16), tri) + running
            sel = (key > thr) | (eq & (pref <= need))
            mbias_scr[:, blk(jb)] = jnp.where(sel, 0.0, NEG)
            return jnp.broadcast_to(pref[:, LANES - 1:LANES], (tq, LANES))

        lax.fori_loop(0, nkb, mask_blk, jnp.zeros((tq, LANES), F32))

        qb = qb_ref[0]
        for h in range(HEADS):
            qh = qb[:, h * HD:(h + 1) * HD]

            def logit_blk(jb, m, h=h, qh=qh):
                kh = kbf_ref[0, blk(jb), h * HD:(h + 1) * HD]
                lg = _nt(qh, kh) + bias_ref[h, j - jb] + mbias_scr[:, blk(jb)]
                logit_scr[:, blk(jb)] = lg
                return jnp.maximum(m, lg)

            m = lax.fori_loop(0, nkb, logit_blk, jnp.full((tq, LANES), NEG, F32))
            m = jnp.max(m, axis=1, keepdims=True)

            def pv_blk(jb, carry, h=h, m=m):
                ssum, acc = carry
                p = jnp.exp(logit_scr[:, blk(jb)] - m)
                vh = vbf_ref[0, blk(jb), h * HD:(h + 1) * HD]
                return ssum + p, acc + _mm(p.astype(BF16), vh)

            ssum, acc = lax.fori_loop(0, nkb, pv_blk, (jnp.zeros((tq, LANES), F32), jnp.zeros((tq, HD), F32)))
            o = acc / jnp.sum(ssum, axis=1, keepdims=True)
            out_ref[0, :, h * HD:(h + 1) * HD] = o.astype(out_ref.dtype)


def _dsa_prompt(qb, qi, small, kbf, vbf, smallbf, bias_tab, bsz, t_pad, lo, topk):
    tq = LANES
    nq = t_pad // tq
    r3 = lambda a: a.reshape(bsz, t_pad, a.shape[-1])
    tile = lambda w: pl.BlockSpec((1, tq, w), lambda b, j: (b, j, 0))
    whole = lambda w: pl.BlockSpec((1, t_pad, w), lambda b, j: (b, 0, 0))
    return pl.pallas_call(
        functools.partial(_dsa_prompt_body, lo=lo, topk=topk),
        grid=(bsz, nq),
        in_specs=[tile(B_DIM), tile(HEADS * HD), tile(LANES), whole(B_DIM), whole(B_DIM), whole(LANES),
                  _const_spec(bias_tab.shape)],
        out_specs=pl.BlockSpec((1, tq, B_DIM), lambda b, j: (b, jnp.maximum(j - 1, 0), 0)),
        out_shape=jax.ShapeDtypeStruct((bsz, t_pad - FRONT, B_DIM), BF16),
        scratch_shapes=[pltpu.VMEM((tq, t_pad), I32), pltpu.VMEM((tq, t_pad), F32), pltpu.VMEM((tq, t_pad), F32)],
        compiler_params=pltpu.CompilerParams(dimension_semantics=("parallel", "arbitrary"),
                                             vmem_limit_bytes=VMEM_LIMIT),
        name="dsa_prompt",
    )(r3(qb), r3(qi), r3(small), r3(kbf), r3(vbf), r3(smallbf), bias_tab)


PAGES_PER_STEP = 8


def _dsa_sample_body(pt_ref, *refs, n_new, past_len, topk):
    npp = PAGES_PER_STEP
    kidx_refs, k_refs, v_refs = refs[:npp], refs[npp:2 * npp], refs[2 * npp:3 * npp]
    (qi_ref, qbd_ref, w_ref, knew_ref, vnew_ref, kidxnew_ref, bias_ref, out_ref,
     keys_scr, logit_scr, v_scr) = refs[3 * npp:]
    del pt_ref
    s = pl.program_id(1)
    n_page_steps = past_len // (PAGE * npp)
    n_blocks = past_len // PAGE + 1
    qi = qi_ref[0]
    qbd = qbd_ref[0]
    w = w_ref[0] * (HEADS ** -0.5)
    q_row = lax.broadcasted_iota(I32, (n_new, LANES), 0)
    lane = lax.broadcasted_iota(I32, (n_new, LANES), 1)

    def do_block(off, kx, kp, vp, admissible):
        rel = jnp.maximum(_nt(qi, kx), 0.0) * w
        sc = rel[0:n_new]
        for h in range(1, HEADS):
            sc = sc + rel[h * n_new:(h + 1) * n_new]
        keys_scr[:, pl.ds(off, LANES)] = _sort_key(sc, admissible)
        logit_scr[:, pl.ds(off, LANES)] = _nt(qbd, kp) + bias_ref[:, pl.ds(off, LANES)]
        v_scr[pl.ds(off, LANES), :] = vp

    @pl.when(s < n_page_steps)
    def _():
        for i in range(npp):
            off = pl.multiple_of((s * npp + i) * PAGE, PAGE)
            do_block(off, kidx_refs[i][0].astype(BF16), k_refs[i][0].astype(BF16), v_refs[i][0].astype(BF16), True)

    @pl.when(s == n_page_steps)
    def _():
        pad = lambda a: jnp.concatenate([a, jnp.zeros((PAGE - n_new, a.shape[1]), a.dtype)], axis=0)
        kx = pad(kidxnew_ref[0])[:, S_KIDX:S_KIDX + HD]
        do_block(past_len, kx, pad(knew_ref[0]), pad(vnew_ref[0]), lane <= q_row)

        keys = keys_scr[...]
        thr = _kth_largest_key(
            lambda cand: jnp.sum(jnp.where(keys >= cand[:, 0:1], 1.0, 0.0), axis=1, keepdims=True), n_new, topk)
        need = topk - jnp.sum(jnp.where(keys > thr[:, 0:1], 1.0, 0.0), axis=1, keepdims=True)
        tri = jnp.where(lax.broadcasted_iota(I32, (LANES, LANES), 0) <= lax.broadcasted_iota(I32, (LANES, LANES), 1),
                        1.0, 0.0).astype(BF16)

        def mask_blk(jb, running):
            cols = pl.ds(pl.multiple_of(jb * LANES, LANES), LANES)
            key = keys_scr[:, cols]
            eq = (key == thr) & (key != INT_MIN)
            pref = _mm(jnp.where(eq, 1.0, 0.0).astype(BF16), tri) + running
            sel = (key > thr) | (eq & (pref <= need))
            mb = jnp.where(sel, 0.0, NEG)
            logit_scr[:, cols] = logit_scr[:, cols] + jnp.concatenate([mb] * HEADS, axis=0)
            return jnp.broadcast_to(pref[:, LANES - 1:LANES], (n_new, LANES))

        lax.fori_loop(0, n_blocks, mask_blk, jnp.zeros((n_new, LANES), F32))
        lg = logit_scr[...]
        m = jnp.max(lg, axis=1, keepdims=True)
        p = jnp.exp(lg - m)
        ssum = jnp.sum(p, axis=1, keepdims=True)
        pv = _mm(p.astype(BF16), v_scr[...]) / ssum
        for h in range(HEADS):
            out_ref[0, :, h * HD:(h + 1) * HD] = pv[h * n_new:(h + 1) * n_new, h * HD:(h + 1) * HD].astype(out_ref.dtype)


def _dsa_sample(qb, qi, small, kbf, vbf, smallbf, cache_k, cache_v, cache_kidx, page_table, bias_s, n_new, topk):
    dbs, n_pages = page_table.shape
    past_len = n_pages * PAGE
    npp = PAGES_PER_STEP
    assert n_pages % npp == 0 and n_new == SUBLANES
    n_page_steps = n_pages // npp
    n_blocks = n_pages + 1
    n_pool = cache_k.shape[0]
    qi_rows = qi.reshape(dbs, n_new, HEADS, HD).transpose(0, 2, 1, 3).reshape(dbs, HEADS * n_new, HD)
    q4 = qb.reshape(dbs, n_new, HEADS, HD).transpose(0, 2, 1, 3)
    eye = jnp.eye(HEADS, dtype=qb.dtype)
    qbd = (q4[:, :, :, None, :] * eye[None, :, None, :, None]).reshape(dbs, HEADS * n_new, B_DIM)
    w_rows = small.reshape(dbs, n_new, LANES)[:, :, S_W:S_W + HEADS].transpose(0, 2, 1).reshape(dbs, HEADS * n_new, 1)
    w_rows = jnp.broadcast_to(w_rows, (dbs, HEADS * n_new, LANES))

    def page_spec(width, i):
        def imap(b, s, pt):
            return (pt[b, jnp.minimum(s, n_page_steps - 1) * npp + i], 0, 0)
        return pl.BlockSpec((1, PAGE, width), imap)

    per_seq = lambda r, w: pl.BlockSpec((1, r, w), lambda b, s, pt: (b, 0, 0))
    in_specs = ([page_spec(HD, i) for i in range(npp)] + [page_spec(B_DIM, i) for i in range(npp)]
                + [page_spec(B_DIM, i) for i in range(npp)]
                + [per_seq(HEADS * n_new, HD), per_seq(HEADS * n_new, B_DIM), per_seq(HEADS * n_new, LANES),
                   per_seq(n_new, B_DIM), per_seq(n_new, B_DIM), per_seq(n_new, LANES),
                   pl.BlockSpec(bias_s.shape, lambda b, s, pt: (0, 0), pipeline_mode=pl.Buffered(1))])
    grid_spec = pltpu.PrefetchScalarGridSpec(
        num_scalar_prefetch=1,
        grid=(dbs, n_page_steps + 1),
        in_specs=in_specs,
        out_specs=pl.BlockSpec((1, n_new, B_DIM), lambda b, s, pt: (b, 0, 0)),
        scratch_shapes=[pltpu.VMEM((n_new, n_blocks * LANES), I32), pltpu.VMEM((HEADS * n_new, n_blocks * LANES), F32),
                        pltpu.VMEM((n_blocks * LANES, B_DIM), BF16)],
    )
    ck = cache_k.reshape(n_pool, PAGE, B_DIM)
    cv = cache_v.reshape(n_pool, PAGE, B_DIM)
    r3 = lambda a: a.reshape(dbs, n_new, a.shape[-1])
    return pl.pallas_call(
        functools.partial(_dsa_sample_body, n_new=n_new, past_len=past_len, topk=topk),
        grid_spec=grid_spec,
        out_shape=jax.ShapeDtypeStruct((dbs, n_new, B_DIM), BF16),
        compiler_params=pltpu.CompilerParams(dimension_semantics=("parallel", "arbitrary"),
                                             vmem_limit_bytes=VMEM_LIMIT),
        name="dsa_sample",
    )(page_table, *([cache_kidx] * npp), *([ck] * npp), *([cv] * npp),
      qi_rows, qbd, w_rows, r3(kbf), r3(vbf), r3(smallbf), bias_s)


def _rms(x, g):
    return x * lax.rsqrt(jnp.mean(x * x, axis=-1, keepdims=True) + EPS) * g


def _mlp_body(x_ref, ma_ref, ob_ref, woa_ref, wob_ref, gffn_ref, wg_ref, wu_ref, wd_ref, gfin_ref, y_ref):
    h = x_ref[...] + (_mm(ma_ref[...], woa_ref[...]) + _mm(ob_ref[...], wob_ref[...]))
    hn = _rms(h, gffn_ref[...]).astype(BF16)
    act = _silu(_mm(hn, wg_ref[...])) * _mm(hn, wu_ref[...])
    y_ref[...] = _rms(h + _mm(act.astype(BF16), wd_ref[...]), gfin_ref[...])


def _mlp(x2d, mix_a, o_b, w_out, g_ffn, w_gate, w_up, w_down, g_final, tm):
    n = x2d.shape[0]
    assert n % tm == 0
    row = lambda w: pl.BlockSpec((tm, w), lambda i: (i, 0))
    woa, wob = w_out[:A_DIM].astype(BF16), w_out[A_DIM:].astype(BF16)
    return pl.pallas_call(
        _mlp_body,
        grid=(n // tm,),
        in_specs=[row(D_MODEL), row(A_DIM), row(B_DIM), _const_spec(woa.shape), _const_spec(wob.shape),
                  _const_spec((1, D_MODEL)), _const_spec(w_gate.shape), _const_spec(w_up.shape),
                  _const_spec(w_down.shape), _const_spec((1, D_MODEL))],
        out_specs=row(D_MODEL),
        out_shape=jax.ShapeDtypeStruct((n, D_MODEL), F32),
        compiler_params=pltpu.CompilerParams(dimension_semantics=("parallel",), vmem_limit_bytes=VMEM_LIMIT),
        name="mlp",
    )(x2d, mix_a, o_b, woa, wob, g_ffn.reshape(1, D_MODEL), w_gate.astype(BF16), w_up.astype(BF16),
      w_down.astype(BF16), g_final.reshape(1, D_MODEL))


def kernel(x_prompt, x_sample, cache_k, cache_v, cache_kidx, state_conv, state_ssm, page_table, meta_tokens, norm_mix, w_in, conv_w, a_log, dt_bias, o_norm, w_out, rel_bias, norm_ffn, w_gate, w_up, w_down, norm_final):
    assert w_in.shape[0] == 1, "single-layer stack"
    bsz, seq, _ = x_prompt.shape
    dbs, n_new, _ = x_sample.shape
    t_pad = FRONT + seq
    lo = FRONT - N_META
    past_len = page_table.shape[1] * PAGE
    topk_p = min(TOPK_MAX, seq // 4)
    topk_s = min(TOPK_MAX, (past_len + n_new) // 4)
    w_parts = _split_w_in(w_in[0])
    mlp_w = (w_out[0], norm_ffn[0], w_gate[0], w_up[0], w_down[0], norm_final)
    gdn_w = (conv_w[0], a_log[0], dt_bias[0], o_norm[0])

    meta = jnp.broadcast_to(meta_tokens.astype(F32)[None], (bsz, N_META, D_MODEL))
    xp = jnp.concatenate([jnp.zeros((bsz, lo, D_MODEL), F32), meta, x_prompt], axis=1).reshape(bsz * t_pad, D_MODEL)
    qkva, z, qb, k, v, kbf, vbf, qi, small, smallbf, small_t = _proj(xp, norm_mix[0], w_parts, 512)
    mix_a, ssm_p = _gdn_prompt(qkva, z, small, small_t, *gdn_w, bsz, t_pad, lo)
    bias_tab = _bias_tab(rel_bias, t_pad // LANES)
    o_b = _dsa_prompt(qb, qi, small, kbf, vbf, smallbf, bias_tab, bsz, t_pad, lo, topk_p)
    y_p = _mlp(x_prompt.reshape(bsz * seq, D_MODEL), mix_a.reshape(bsz * seq, A_DIM), o_b.reshape(bsz * seq, B_DIM),
               *mlp_w, 512).reshape(bsz, seq, D_MODEL)
    real = lambda a: a.reshape(bsz, t_pad, a.shape[-1])[:, lo:]
    k_p = real(k).reshape(1, bsz, N_META + seq, HEADS, HD)
    v_p = real(v).reshape(1, bsz, N_META + seq, HEADS, HD)
    kidx_p = real(small)[:, :, S_KIDX:S_KIDX + HD][None]
    conv_p = real(qkva)[:, -(CONV_WIDTH - 1):][None]

    xs = x_sample.reshape(dbs * n_new, D_MODEL)
    qkva_s, z_s, qb_s, k_s, v_s, kbf_s, vbf_s, qi_s, small_s, smallbf_s, small_t_s = _proj(xs, norm_mix[0], w_parts, 512)
    mix_a_s, ssm_s = _gdn_sample(qkva_s, z_s, small_s, small_t_s, state_conv[0], state_ssm[0], *gdn_w, n_new)
    bias_s = _bias_sample(rel_bias, past_len, past_len // PAGE + 1)
    o_b_s = _dsa_sample(qb_s, qi_s, small_s, kbf_s, vbf_s, smallbf_s, cache_k[0], cache_v[0], cache_kidx[0], page_table,
                        bias_s, n_new, topk_s)
    y_s = _mlp(xs, mix_a_s, o_b_s.reshape(dbs * n_new, B_DIM), *mlp_w, 512).reshape(dbs, n_new, D_MODEL)
    k_sn = k_s.reshape(1, dbs, n_new, HEADS, HD)
    v_sn = v_s.reshape(1, dbs, n_new, HEADS, HD)
    kidx_s = small_s.reshape(dbs, n_new, LANES)[:, :, S_KIDX:S_KIDX + HD][None]
    conv_s = jnp.concatenate([state_conv[0].astype(F32), qkva_s.reshape(dbs, n_new, 3 * A_DIM)],
                             axis=1)[:, -(CONV_WIDTH - 1):][None]
    return (y_p, y_s, k_p, v_p, kidx_p, conv_p, ssm_p[None], k_sn, v_sn, kidx_s, conv_s, ssm_s[None])
```

```python
import functools
import math

import jax
import jax.numpy as jnp
from jax import lax
from jax.experimental import pallas as pl
from jax.experimental.pallas import tpu as pltpu

F32 = jnp.float32
BF16 = jnp.bfloat16
I32 = jnp.int32
HI = lax.Precision.HIGHEST

D_MODEL = 1024
N_META = 16
HEADS = 8
HD = 64
A_DIM = HEADS * HD
B_DIM = HEADS * HD
CONV_WIDTH = 4
GDN_CHUNK = 64
TOPK_MAX = 256
PAGE = 128
REL_BUCKETS = 32
REL_MAX_DIST = 1024
D_FF = 2816
EPS = 1e-6
LANES = 128
SUBLANES = 8
FRONT = 128
NEG = -1e30
INT_MIN = -2 ** 31
VMEM_LIMIT = 56 * 1024 * 1024

S_KIDX, S_BETA, S_A, S_W = 0, 64, 72, 80


def _mm(a, b):
    return jnp.dot(a, b, preferred_element_type=F32)


def _mm_hi(a, b):
    return jnp.dot(a, b, preferred_element_type=F32, precision=HI)


def _split_bf16(x):
    hi = x.astype(BF16)
    return hi, (x - hi.astype(F32)).astype(BF16)


def _mm_solve(a, b):
    a_hi, a_lo = _split_bf16(a)
    b_hi, b_lo = _split_bf16(b)
    return _mm(a_hi, b_hi) + (_mm(a_hi, b_lo) + _mm(a_lo, b_hi))


def _nt(a, b, precision=None):
    return lax.dot_general(a, b, (((1,), (1,)), ((), ())), preferred_element_type=F32, precision=precision)


def _tn(a, b, precision=None):
    return lax.dot_general(a, b, (((0,), (0,)), ((), ())), preferred_element_type=F32, precision=precision)


def _const_spec(shape):
    nd = len(shape)
    return pl.BlockSpec(shape, lambda *_: (0,) * nd, pipeline_mode=pl.Buffered(1))


def _silu(x):
    return x * jax.nn.sigmoid(x)


def _proj_body(x_ref, g_ref, wa_ref, wz_ref, wb_ref, wqi_ref, ws_ref, wst_ref,
               qkva_ref, z_ref, qb_ref, k_ref, v_ref, kbf_ref, vbf_ref, qi_ref, small_ref, smallbf_ref, smallt_ref):
    x = x_ref[...]
    ms = jnp.mean(x * x, axis=-1, keepdims=True)
    h = (x * lax.rsqrt(ms + EPS) * g_ref[...]).astype(BF16)
    qkva_ref[...] = _mm(h, wa_ref[...])
    z_ref[...] = _mm(h, wz_ref[...])
    qkvb = _mm(h, wb_ref[...])
    qb_ref[...] = (qkvb[:, :B_DIM] * 0.125).astype(BF16)
    k = qkvb[:, B_DIM:2 * B_DIM]
    v = qkvb[:, 2 * B_DIM:]
    k_ref[...] = k
    v_ref[...] = v
    kbf_ref[...] = k.astype(BF16)
    vbf_ref[...] = v.astype(BF16)
    qi_ref[...] = (_mm(h, wqi_ref[...]) * 0.125).astype(BF16)
    s = _mm(h, ws_ref[...])
    small_ref[...] = s
    smallbf_ref[...] = s.astype(BF16)
    smallt_ref[...] = _nt(wst_ref[...], h)


def _split_w_in(w_in):
    c = 0
    parts = []
    for n in (3 * A_DIM, A_DIM, HEADS, HEADS, 3 * B_DIM, HEADS * HD, HD, HEADS):
        parts.append(w_in[:, c:c + n])
        c += n
    w_qkva, w_z, w_beta, w_a, w_qkvb, w_qi, w_kidx, w_w = parts
    pad = jnp.zeros((w_in.shape[0], LANES - (HD + 3 * HEADS)), w_in.dtype)
    w_small = jnp.concatenate([w_kidx, w_beta, w_a, w_w, pad], axis=1)
    bf = lambda t: t.astype(BF16)
    return bf(w_qkva), bf(w_z), bf(w_qkvb), bf(w_qi), bf(w_small), bf(w_small.T)


def _proj(x2d, g, w_parts, tm):
    n = x2d.shape[0]
    assert n % tm == 0 and tm % LANES == 0
    wa, wz, wb, wqi, ws, wst = w_parts
    row = lambda w: pl.BlockSpec((tm, w), lambda i: (i, 0))
    outs = [
        (jax.ShapeDtypeStruct((n, 3 * A_DIM), F32), row(3 * A_DIM)),
        (jax.ShapeDtypeStruct((n, A_DIM), F32), row(A_DIM)),
        (jax.ShapeDtypeStruct((n, B_DIM), BF16), row(B_DIM)),
        (jax.ShapeDtypeStruct((n, B_DIM), F32), row(B_DIM)),
        (jax.ShapeDtypeStruct((n, B_DIM), F32), row(B_DIM)),
        (jax.ShapeDtypeStruct((n, B_DIM), BF16), row(B_DIM)),
        (jax.ShapeDtypeStruct((n, B_DIM), BF16), row(B_DIM)),
        (jax.ShapeDtypeStruct((n, HEADS * HD), BF16), row(HEADS * HD)),
        (jax.ShapeDtypeStruct((n, LANES), F32), row(LANES)),
        (jax.ShapeDtypeStruct((n, LANES), BF16), row(LANES)),
        (jax.ShapeDtypeStruct((LANES, n), F32), pl.BlockSpec((LANES, tm), lambda i: (0, i))),
    ]
    return pl.pallas_call(
        _proj_body,
        grid=(n // tm,),
        in_specs=[row(D_MODEL), _const_spec((1, D_MODEL)), _const_spec(wa.shape), _const_spec(wz.shape),
                  _const_spec(wb.shape), _const_spec(wqi.shape), _const_spec(ws.shape), _const_spec(wst.shape)],
        out_specs=[o[1] for o in outs],
        out_shape=[o[0] for o in outs],
        compiler_params=pltpu.CompilerParams(dimension_semantics=("parallel",), vmem_limit_bytes=VMEM_LIMIT),
        name="proj",
    )(x2d, g.reshape(1, D_MODEL), wa, wz, wb, wqi, ws, wst)


def _gdn_prep(qkv, tail, conv_w, small, small_t, a_log_row, dt_row, a_log_col, dt_col, valid_col, valid_row, chunk):
    rows = qkv.shape[0]
    n_chunks = rows // chunk
    ys = []
    for c in range(n_chunks):
        ext = jnp.concatenate([tail[c], qkv[c * chunk:(c + 1) * chunk]], axis=0)
        y = sum(ext[SUBLANES - (CONV_WIDTH - 1) + j: SUBLANES - (CONV_WIDTH - 1) + j + chunk] * conv_w[j:j + 1]
                for j in range(CONV_WIDTH))
        ys.append(_silu(y))
    y = ys[0] if n_chunks == 1 else jnp.concatenate(ys, axis=0)
    beta_c = jnp.where(valid_col, jax.nn.sigmoid(small), 0.0)
    beta_r = jnp.where(valid_row, jax.nn.sigmoid(small_t), 0.0)
    g_c = jnp.where(valid_col, -jnp.exp(a_log_row) * jax.nn.softplus(small + dt_row), 0.0)
    g_r = jnp.where(valid_row, -jnp.exp(a_log_col) * jax.nn.softplus(small_t + dt_col), 0.0)
    ri = lax.broadcasted_iota(I32, (rows, rows), 0)
    ci = lax.broadcasted_iota(I32, (rows, rows), 1)
    same = (ri // chunk) == (ci // chunk)
    low = jnp.where(same & (ci <= ri), 1.0, 0.0).astype(F32)
    b_c = _mm_hi(low, g_c)
    b_r = _nt(g_r, low, precision=HI)
    return y, beta_c, b_c, b_r


def _gdn_head(qh, kh, vh, beta_c, b_c, b_r, s0, o_norm, zh, chunk):
    c = chunk
    qn = qh * lax.rsqrt(jnp.sum(qh * qh, axis=-1, keepdims=True) + EPS) * (HD ** -0.5)
    kn = kh * lax.rsqrt(jnp.sum(kh * kh, axis=-1, keepdims=True) + EPS)
    ri = lax.broadcasted_iota(I32, (c, c), 0)
    ci = lax.broadcasted_iota(I32, (c, c), 1)
    incl = ci <= ri
    strict = ci < ri
    decay = jnp.where(incl, jnp.exp(jnp.where(incl, b_c - b_r, 0.0)), 0.0)
    kb, qb16, s0b = kn.astype(BF16), qn.astype(BF16), s0.astype(BF16)
    kk = _nt(kb, kb)
    qk = _nt(qb16, kb)
    a_mat = jnp.where(strict, beta_c * kk * decay, 0.0)
    eb = jnp.exp(b_c)
    y = jnp.concatenate([beta_c * vh, (beta_c * eb) * kn], axis=1)
    bk = -a_mat
    n_levels = int(math.log2(c))
    for lvl in range(n_levels):
        if lvl + 1 < n_levels:
            prod = _mm_solve(bk, jnp.concatenate([y, bk], axis=1))
            y, bk = y + prod[:, :2 * HD], prod[:, 2 * HD:]
        else:
            y = y + _mm_solve(bk, y)
    u, w = y[:, :HD], y[:, HD:]
    delta = u - _mm(w.astype(BF16), s0b)
    deltab = delta.astype(BF16)
    o = eb * _mm(qb16, s0b) + _mm((qk * decay).astype(BF16), deltab)
    b_last = b_c[c - 1:c, :]
    s_new = jnp.exp(b_last) * s0 + _tn((kn * jnp.exp(b_last - b_c)).astype(BF16), deltab)
    on = o * lax.rsqrt(jnp.mean(o * o, axis=-1, keepdims=True) + EPS) * o_norm
    return s_new, on * _silu(zh)


def _gdn_chunk_all_heads(y, beta_c, b_c, b_r, z, o_norm, r0, chunk, get_s0, put_s, put_o):
    for h in range(HEADS):
        sl = slice(r0, r0 + chunk)
        qh = y[sl, h * HD:(h + 1) * HD]
        kh = y[sl, A_DIM + h * HD:A_DIM + (h + 1) * HD]
        vh = y[sl, 2 * A_DIM + h * HD:2 * A_DIM + (h + 1) * HD]
        bc = beta_c[sl, S_BETA + h:S_BETA + h + 1]
        bcol = b_c[sl, S_A + h:S_A + h + 1]
        brow = b_r[S_A + h:S_A + h + 1, r0:r0 + chunk]
        s_new, out = _gdn_head(qh, kh, vh, bc, bcol, brow, get_s0(h), o_norm, z[sl, h * HD:(h + 1) * HD], chunk)
        put_s(h, s_new)
        put_o(h, out)


def _gdn_prompt_body(qkv_ref, z_ref, small_ref, smallt_ref, convw_ref, alr_ref, dtr_ref, alc_ref, dtc_ref, onorm_ref,
                     mix_ref, s_ref, tail_scr, *, rows, lo):
    t = pl.program_id(1)

    @pl.when(t == 0)
    def _():
        s_ref[...] = jnp.zeros_like(s_ref)
        tail_scr[...] = jnp.zeros_like(tail_scr)

    chunk = GDN_CHUNK
    qkv = qkv_ref[0]
    tails = [tail_scr[...]] + [qkv[c * chunk - SUBLANES:c * chunk] for c in range(1, rows // chunk)]
    pos_c = t * rows + lax.broadcasted_iota(I32, (rows, LANES), 0)
    pos_r = t * rows + lax.broadcasted_iota(I32, (LANES, rows), 1)
    y, beta_c, b_c, b_r = _gdn_prep(qkv, tails, convw_ref[...], small_ref[0], smallt_ref[...],
                                    alr_ref[...], dtr_ref[...], alc_ref[...], dtc_ref[...],
                                    pos_c >= lo, pos_r >= lo, chunk)
    tail_scr[...] = qkv[rows - SUBLANES:rows]
    z = z_ref[0]
    o_norm = onorm_ref[...]
    for c in range(rows // chunk):
        def put_s(h, s):
            s_ref[0, h] = s

        def put_o(h, o, c=c):
            mix_ref[0, c * chunk:(c + 1) * chunk, h * HD:(h + 1) * HD] = o.astype(mix_ref.dtype)

        _gdn_chunk_all_heads(y, beta_c, b_c, b_r, z, o_norm, c * chunk, chunk, lambda h: s_ref[0, h], put_s, put_o)


def _gate_params(a_log, dt_bias):
    row = lambda v: jnp.zeros((1, LANES), F32).at[0, S_A:S_A + HEADS].set(v.astype(F32))
    return row(a_log), row(dt_bias), row(a_log).T, row(dt_bias).T


def _gdn_prompt(qkva, z, small, small_t, conv_w, a_log, dt_bias, o_norm, bsz, t_pad, lo):
    rows = LANES
    nt = t_pad // rows
    alr, dtr, alc, dtc = _gate_params(a_log, dt_bias)
    body = functools.partial(_gdn_prompt_body, rows=rows, lo=lo)
    mix, s = pl.pallas_call(
        body,
        grid=(bsz, nt),
        in_specs=[
            pl.BlockSpec((1, rows, 3 * A_DIM), lambda b, t: (b, t, 0)),
            pl.BlockSpec((1, rows, A_DIM), lambda b, t: (b, t, 0)),
            pl.BlockSpec((1, rows, LANES), lambda b, t: (b, t, 0)),
            pl.BlockSpec((LANES, rows), lambda b, t: (0, b * nt + t)),
            _const_spec((CONV_WIDTH, 3 * A_DIM)), _const_spec((1, LANES)), _const_spec((1, LANES)),
            _const_spec((LANES, 1)), _const_spec((LANES, 1)), _const_spec((1, HD)),
        ],
        out_specs=[
            pl.BlockSpec((1, rows, A_DIM), lambda b, t: (b, jnp.maximum(t - 1, 0), 0)),
            pl.BlockSpec((1, HEADS, HD, HD), lambda b, t: (b, 0, 0, 0)),
        ],
        out_shape=[jax.ShapeDtypeStruct((bsz, t_pad - FRONT, A_DIM), BF16),
                   jax.ShapeDtypeStruct((bsz, HEADS, HD, HD), F32)],
        scratch_shapes=[pltpu.VMEM((SUBLANES, 3 * A_DIM), F32)],
        compiler_params=pltpu.CompilerParams(dimension_semantics=("parallel", "arbitrary"),
                                             vmem_limit_bytes=VMEM_LIMIT),
        name="gdn_prompt",
    )(qkva.reshape(bsz, t_pad, 3 * A_DIM), z.reshape(bsz, t_pad, A_DIM), small.reshape(bsz, t_pad, LANES), small_t,
      conv_w, alr, dtr, alc, dtc, o_norm.reshape(1, HD))
    return mix, s


def _gdn_sample_body(qkv_ref, z_ref, small_ref, smallt_ref, conv0_ref, s0_ref, convw_ref, alr_ref, dtr_ref, alc_ref,
                     dtc_ref, onorm_ref, mix_ref, s_ref, *, chunk):
    rows = LANES
    n_seq = rows // chunk
    qkv = qkv_ref[...]
    tails = [conv0_ref[i] for i in range(n_seq)]
    y, beta_c, b_c, b_r = _gdn_prep(qkv, tails, convw_ref[...], small_ref[...], smallt_ref[...],
                                    alr_ref[...], dtr_ref[...], alc_ref[...], dtc_ref[...], True, True, chunk)
    z = z_ref[...]
    o_norm = onorm_ref[...]
    for i in range(n_seq):
        def put_s(h, s, i=i):
            s_ref[i, h] = s

        def put_o(h, o, i=i):
            mix_ref[i * chunk:(i + 1) * chunk, h * HD:(h + 1) * HD] = o.astype(mix_ref.dtype)

        _gdn_chunk_all_heads(y, beta_c, b_c, b_r, z, o_norm, i * chunk, chunk, lambda h, i=i: s0_ref[i, h], put_s, put_o)


def _gdn_sample(qkva, z, small, small_t, state_conv, state_ssm, conv_w, a_log, dt_bias, o_norm, n_new):
    n = qkva.shape[0]
    bsz = n // n_new
    rows = LANES
    n_seq = rows // n_new
    assert n_new == SUBLANES and n % rows == 0
    alr, dtr, alc, dtc = _gate_params(a_log, dt_bias)
    conv0 = jnp.pad(state_conv.astype(F32), ((0, 0), (SUBLANES - (CONV_WIDTH - 1), 0), (0, 0)))
    body = functools.partial(_gdn_sample_body, chunk=n_new)
    mix, s = pl.pallas_call(
        body,
        grid=(n // rows,),
        in_specs=[
            pl.BlockSpec((rows, 3 * A_DIM), lambda g: (g, 0)),
            pl.BlockSpec((rows, A_DIM), lambda g: (g, 0)),
            pl.BlockSpec((rows, LANES), lambda g: (g, 0)),
            pl.BlockSpec((LANES, rows), lambda g: (0, g)),
            pl.BlockSpec((n_seq, SUBLANES, 3 * A_DIM), lambda g: (g, 0, 0)),
            pl.BlockSpec((n_seq, HEADS, HD, HD), lambda g: (g, 0, 0, 0)),
            _const_spec((CONV_WIDTH, 3 * A_DIM)), _const_spec((1, LANES)), _const_spec((1, LANES)),
            _const_spec((LANES, 1)), _const_spec((LANES, 1)), _const_spec((1, HD)),
        ],
        out_specs=[
            pl.BlockSpec((rows, A_DIM), lambda g: (g, 0)),
            pl.BlockSpec((n_seq, HEADS, HD, HD), lambda g: (g, 0, 0, 0)),
        ],
        out_shape=[jax.ShapeDtypeStruct((n, A_DIM), BF16), jax.ShapeDtypeStruct((bsz, HEADS, HD, HD), F32)],
        compiler_params=pltpu.CompilerParams(dimension_semantics=("parallel",), vmem_limit_bytes=VMEM_LIMIT),
        name="gdn_sample",
    )(qkva, z, small, small_t, conv0, state_ssm.astype(F32), conv_w, alr, dtr, alc, dtc, o_norm.reshape(1, HD))
    return mix, s


def _rel_bucket(d):
    d = jnp.maximum(d, 0)
    max_exact = REL_BUCKETS // 2
    df = jnp.maximum(d, 1).astype(F32)
    large = max_exact + (jnp.log(df / max_exact) / math.log(REL_MAX_DIST / max_exact)
                         * (REL_BUCKETS - max_exact)).astype(I32)
    large = jnp.minimum(large, REL_BUCKETS - 1)
    return jnp.where(d < max_exact, d, large)


def _bias_lookup(bucket, rb_ref, h):
    acc = jnp.zeros(bucket.shape, F32)
    for b in range(REL_BUCKETS):
        acc = jnp.where(bucket == b, rb_ref[b, h], acc)
    return acc


def _bias_tab_body(rb_ref, out_ref):
    dlt = pl.program_id(0)
    i = lax.broadcasted_iota(I32, (LANES, LANES), 0)
    j = lax.broadcasted_iota(I32, (LANES, LANES), 1)
    bucket = _rel_bucket(dlt * LANES + i - j)
    for h in range(HEADS):
        out_ref[h, 0] = _bias_lookup(bucket, rb_ref, h)


def _bias_tab(rel_bias, n_tiles):
    return pl.pallas_call(
        _bias_tab_body,
        grid=(n_tiles,),
        in_specs=[pl.BlockSpec(memory_space=pltpu.SMEM)],
        out_specs=pl.BlockSpec((HEADS, 1, LANES, LANES), lambda t: (0, t, 0, 0)),
        out_shape=jax.ShapeDtypeStruct((HEADS, n_tiles, LANES, LANES), F32),
        name="bias_tab",
    )(rel_bias.astype(F32))


def _bias_sample_body(rb_ref, out_ref, *, past_len):
    blk = pl.program_id(0)
    q = lax.broadcasted_iota(I32, (SUBLANES, LANES), 0)
    s = blk * LANES + lax.broadcasted_iota(I32, (SUBLANES, LANES), 1)
    bucket = _rel_bucket(past_len + q - s)
    for h in range(HEADS):
        out_ref[h * SUBLANES:(h + 1) * SUBLANES, :] = _bias_lookup(bucket, rb_ref, h)


def _bias_sample(rel_bias, past_len, n_blocks):
    return pl.pallas_call(
        functools.partial(_bias_sample_body, past_len=past_len),
        grid=(n_blocks,),
        in_specs=[pl.BlockSpec(memory_space=pltpu.SMEM)],
        out_specs=pl.BlockSpec((HEADS * SUBLANES, LANES), lambda t: (0, t)),
        out_shape=jax.ShapeDtypeStruct((HEADS * SUBLANES, n_blocks * LANES), F32),
        name="bias_sample",
    )(rel_bias.astype(F32))


def _sort_key(score, admissible):
    score = jnp.where(score == 0.0, 0.0, score)
    bits = pltpu.bitcast(score, I32)
    key = jnp.where(bits < 0, bits ^ 0x7FFFFFFF, bits)
    return jnp.where(admissible, key, INT_MIN)


def _kth_largest_key(count_ge, rows, topk):
    c0 = jnp.broadcast_to(count_ge(jnp.zeros((rows, LANES), I32)), (rows, LANES))
    t0 = jnp.where(c0 >= topk, 0, INT_MIN).astype(I32)
    c0 = jnp.where(c0 >= topk, c0, float(2 ** 30))

    def bit_step(i, carry):
        t, c = carry
        cand = t + lax.shift_left(jnp.int32(1), 30 - i)
        cc = jnp.broadcast_to(count_ge(cand), (rows, LANES))
        ok = cc >= topk
        return jnp.where(ok, cand, t), jnp.where(ok, cc, c)

    return lax.fori_loop(0, 31, bit_step, (t0, c0))


def _tie_mask(key, thr, need, running, tri):
    eq = (key == thr) & (key != INT_MIN)
    pref = _mm(jnp.where(eq, 1.0, 0.0).astype(BF16), tri) + running
    sel = (key > thr) | (eq & (pref <= need))
    return sel, jnp.broadcast_to(pref[:, LANES - 1:LANES], pref.shape)


def _tri_incl():
    r = lax.broadcasted_iota(I32, (LANES, LANES), 0)
    c = lax.broadcasted_iota(I32, (LANES, LANES), 1)
    return jnp.where(r <= c, 1.0, 0.0).astype(BF16)


DSA_BUCKETS = (5, 9, 13, 17)


def _dsa_prompt_tile(j, nblk, qi_ref, small_ref, kidx_ref, bias_ref, out_ref,
                     qh_scr, kh_scr, vh_scr, keys_scr, mbias_scr, o_scr, *, lo, topk):
    tq = LANES
    wid = nblk * LANES
    t_pos = j * tq + lax.broadcasted_iota(I32, (tq, wid), 0)
    s_pos = lax.broadcasted_iota(I32, (tq, wid), 1)
    w = small_ref[0][:, S_W:S_W + HEADS] * (HEADS ** -0.5)
    qi = qi_ref[0]
    kx = kidx_ref[0, 0:wid, S_KIDX:S_KIDX + HD]
    acc = jnp.zeros((tq, wid), F32)
    for h in range(HEADS):
        acc = acc + jnp.maximum(_nt(qi[:, h * HD:(h + 1) * HD], kx), 0.0) * w[:, h:h + 1]
    keys_scr[:, 0:wid] = _sort_key(acc, (s_pos <= t_pos) & (s_pos >= lo))

    def count_ge(cand):
        return jnp.sum(jnp.where(keys_scr[:, 0:wid] >= cand[:, 0:1], 1.0, 0.0), axis=1, keepdims=True)

    thr, cnt = _kth_largest_key(count_ge, tq, topk)
    keys = keys_scr[:, 0:wid]
    mbias_scr[:, 0:wid] = jnp.where(keys >= jnp.maximum(thr[:, 0:1], INT_MIN + 1), 0.0, NEG)

    @pl.when(jnp.max(jnp.where(cnt < float(2 ** 30), cnt, 0.0)) > topk)
    def _():
        need = topk - jnp.sum(jnp.where(keys_scr[:, 0:wid] > thr[:, 0:1], 1.0, 0.0), axis=1, keepdims=True)
        tri = _tri_incl()
        running = jnp.zeros((tq, LANES), F32)
        for jb in range(nblk):
            cols = slice(jb * LANES, (jb + 1) * LANES)
            sel, running = _tie_mask(keys_scr[:, cols], thr, need, running, tri)
            mbias_scr[:, cols] = jnp.where(sel, 0.0, NEG)

    def head_attn(h, carry):
        bias = jnp.concatenate([bias_ref[h, jnp.maximum(j - jb, 0)] for jb in range(nblk)], axis=1)
        lg = _nt(qh_scr[h], kh_scr[h, 0:wid, :]) + bias + mbias_scr[:, 0:wid]
        p = jnp.exp(lg - jnp.max(lg, axis=1, keepdims=True))
        inv = 1.0 / jnp.sum(p, axis=1, keepdims=True)
        o_scr[h] = _mm(p.astype(BF16), vh_scr[h, 0:wid, :]) * inv
        return carry

    lax.fori_loop(0, HEADS, head_attn, 0)
    for h in range(HEADS):
        out_ref[0, :, h * HD:(h + 1) * HD] = o_scr[h].astype(out_ref.dtype)


def _dsa_prompt_body(qb_ref, qi_ref, small_ref, kbf_ref, vbf_ref, kidx_ref, bias_ref, out_ref,
                     qh_scr, kh_scr, vh_scr, keys_scr, mbias_scr, o_scr, *, lo, topk, buckets):
    j = pl.program_id(1)

    @pl.when(j == 1)
    def _():
        for h in range(HEADS):
            kh_scr[h] = kbf_ref[0, :, h * HD:(h + 1) * HD]
            vh_scr[h] = vbf_ref[0, :, h * HD:(h + 1) * HD]

    @pl.when(j > 0)
    def _():
        for h in range(HEADS):
            qh_scr[h] = qb_ref[0, :, h * HD:(h + 1) * HD]

    lower = 0
    for nblk in buckets:
        @pl.when((j > lower) & (j + 1 <= nblk))
        def _(nblk=nblk):
            _dsa_prompt_tile(j, nblk, qi_ref, small_ref, kidx_ref, bias_ref, out_ref,
                             qh_scr, kh_scr, vh_scr, keys_scr, mbias_scr, o_scr, lo=lo, topk=topk)
        lower = nblk - 1


def _dsa_prompt(qb, qi, small, kbf, vbf, smallbf, bias_tab, bsz, t_pad, lo, topk):
    tq = LANES
    nq = t_pad // tq
    buckets = tuple(b for b in DSA_BUCKETS if b < nq) + (nq,)
    r3 = lambda a: a.reshape(bsz, t_pad, a.shape[-1])
    tile = lambda w: pl.BlockSpec((1, tq, w), lambda b, j: (b, j, 0))
    whole = lambda w: pl.BlockSpec((1, t_pad, w), lambda b, j: (b, 0, 0))
    return pl.pallas_call(
        functools.partial(_dsa_prompt_body, lo=lo, topk=topk, buckets=buckets),
        grid=(bsz, nq),
        in_specs=[tile(B_DIM), tile(HEADS * HD), tile(LANES), whole(B_DIM), whole(B_DIM), whole(LANES),
                  _const_spec(bias_tab.shape)],
        out_specs=pl.BlockSpec((1, tq, B_DIM), lambda b, j: (b, jnp.maximum(j - 1, 0), 0)),
        out_shape=jax.ShapeDtypeStruct((bsz, t_pad - FRONT, B_DIM), BF16),
        scratch_shapes=[pltpu.VMEM((HEADS, tq, HD), BF16), pltpu.VMEM((HEADS, t_pad, HD), BF16),
                        pltpu.VMEM((HEADS, t_pad, HD), BF16), pltpu.VMEM((tq, t_pad), I32),
                        pltpu.VMEM((tq, t_pad), F32), pltpu.VMEM((HEADS, tq, HD), F32)],
        compiler_params=pltpu.CompilerParams(dimension_semantics=("parallel", "arbitrary"),
                                             vmem_limit_bytes=VMEM_LIMIT),
        name="dsa_prompt",
    )(r3(qb), r3(qi), r3(small), r3(kbf), r3(vbf), r3(smallbf), bias_tab)


PAGES_PER_STEP = 8


def _dsa_sample_body(pt_ref, *refs, n_new, past_len, topk):
    npp = PAGES_PER_STEP
    kidx_refs, k_refs, v_refs = refs[:npp], refs[npp:2 * npp], refs[2 * npp:3 * npp]
    (qi_ref, qbd_ref, w_ref, knew_ref, vnew_ref, kidxnew_ref, bias_ref, out_ref,
     keys_scr, logit_scr, v_scr) = refs[3 * npp:]
    del pt_ref
    s = pl.program_id(1)
    n_page_steps = past_len // (PAGE * npp)
    n_blocks = past_len // PAGE + 1
    qi = qi_ref[0]
    qbd = qbd_ref[0]
    w = w_ref[0] * (HEADS ** -0.5)
    q_row = lax.broadcasted_iota(I32, (n_new, LANES), 0)
    lane = lax.broadcasted_iota(I32, (n_new, LANES), 1)

    def do_block(off, kx, kp, vp, admissible):
        rel = jnp.maximum(_nt(qi, kx), 0.0) * w
        sc = rel[0:n_new]
        for h in range(1, HEADS):
            sc = sc + rel[h * n_new:(h + 1) * n_new]
        keys_scr[:, pl.ds(off, LANES)] = _sort_key(sc, admissible)
        logit_scr[:, pl.ds(off, LANES)] = _nt(qbd, kp) + bias_ref[:, pl.ds(off, LANES)]
        v_scr[pl.ds(off, LANES), :] = vp

    @pl.when(s < n_page_steps)
    def _():
        for i in range(npp):
            off = pl.multiple_of((s * npp + i) * PAGE, PAGE)
            do_block(off, kidx_refs[i][0].astype(BF16), k_refs[i][0].astype(BF16), v_refs[i][0].astype(BF16), True)

    @pl.when(s == n_page_steps)
    def _():
        pad = lambda a: jnp.concatenate([a, jnp.zeros((PAGE - n_new, a.shape[1]), a.dtype)], axis=0)
        kx = pad(kidxnew_ref[0])[:, S_KIDX:S_KIDX + HD]
        do_block(past_len, kx, pad(knew_ref[0]), pad(vnew_ref[0]), lane <= q_row)

        keys = keys_scr[...]
        thr, _ = _kth_largest_key(
            lambda cand: jnp.sum(jnp.where(keys >= cand[:, 0:1], 1.0, 0.0), axis=1, keepdims=True), n_new, topk)
        need = topk - jnp.sum(jnp.where(keys > thr[:, 0:1], 1.0, 0.0), axis=1, keepdims=True)
        tri = _tri_incl()

        def mask_blk(jb, running):
            cols = pl.ds(pl.multiple_of(jb * LANES, LANES), LANES)
            sel, running = _tie_mask(keys_scr[:, cols], thr, need, running, tri)
            mb = jnp.where(sel, 0.0, NEG)
            logit_scr[:, cols] = logit_scr[:, cols] + jnp.concatenate([mb] * HEADS, axis=0)
            return running

        lax.fori_loop(0, n_blocks, mask_blk, jnp.zeros((n_new, LANES), F32))
        lg = logit_scr[...]
        m = jnp.max(lg, axis=1, keepdims=True)
        p = jnp.exp(lg - m)
        ssum = jnp.sum(p, axis=1, keepdims=True)
        pv = _mm(p.astype(BF16), v_scr[...]) / ssum
        for h in range(HEADS):
            out_ref[0, :, h * HD:(h + 1) * HD] = pv[h * n_new:(h + 1) * n_new, h * HD:(h + 1) * HD].astype(out_ref.dtype)


def _dsa_sample(qb, qi, small, kbf, vbf, smallbf, cache_k, cache_v, cache_kidx, page_table, bias_s, n_new, topk):
    dbs, n_pages = page_table.shape
    past_len = n_pages * PAGE
    npp = PAGES_PER_STEP
    assert n_pages % npp == 0 and n_new == SUBLANES
    n_page_steps = n_pages // npp
    n_blocks = n_pages + 1
    n_pool = cache_k.shape[0]
    qi_rows = qi.reshape(dbs, n_new, HEADS, HD).transpose(0, 2, 1, 3).reshape(dbs, HEADS * n_new, HD)
    q4 = qb.reshape(dbs, n_new, HEADS, HD).transpose(0, 2, 1, 3)
    eye = jnp.eye(HEADS, dtype=qb.dtype)
    qbd = (q4[:, :, :, None, :] * eye[None, :, None, :, None]).reshape(dbs, HEADS * n_new, B_DIM)
    w_rows = small.reshape(dbs, n_new, LANES)[:, :, S_W:S_W + HEADS].transpose(0, 2, 1).reshape(dbs, HEADS * n_new, 1)
    w_rows = jnp.broadcast_to(w_rows, (dbs, HEADS * n_new, LANES))

    def page_spec(width, i):
        def imap(b, s, pt):
            return (pt[b, jnp.minimum(s, n_page_steps - 1) * npp + i], 0, 0)
        return pl.BlockSpec((1, PAGE, width), imap)

    per_seq = lambda r, w: pl.BlockSpec((1, r, w), lambda b, s, pt: (b, 0, 0))
    in_specs = ([page_spec(HD, i) for i in range(npp)] + [page_spec(B_DIM, i) for i in range(npp)]
                + [page_spec(B_DIM, i) for i in range(npp)]
                + [per_seq(HEADS * n_new, HD), per_seq(HEADS * n_new, B_DIM), per_seq(HEADS * n_new, LANES),
                   per_seq(n_new, B_DIM), per_seq(n_new, B_DIM), per_seq(n_new, LANES),
                   pl.BlockSpec(bias_s.shape, lambda b, s, pt: (0, 0), pipeline_mode=pl.Buffered(1))])
    grid_spec = pltpu.PrefetchScalarGridSpec(
        num_scalar_prefetch=1,
        grid=(dbs, n_page_steps + 1),
        in_specs=in_specs,
        out_specs=pl.BlockSpec((1, n_new, B_DIM), lambda b, s, pt: (b, 0, 0)),
        scratch_shapes=[pltpu.VMEM((n_new, n_blocks * LANES), I32), pltpu.VMEM((HEADS * n_new, n_blocks * LANES), F32),
                        pltpu.VMEM((n_blocks * LANES, B_DIM), BF16)],
    )
    ck = cache_k.reshape(n_pool, PAGE, B_DIM)
    cv = cache_v.reshape(n_pool, PAGE, B_DIM)
    r3 = lambda a: a.reshape(dbs, n_new, a.shape[-1])
    return pl.pallas_call(
        functools.partial(_dsa_sample_body, n_new=n_new, past_len=past_len, topk=topk),
        grid_spec=grid_spec,
        out_shape=jax.ShapeDtypeStruct((dbs, n_new, B_DIM), BF16),
        compiler_params=pltpu.CompilerParams(dimension_semantics=("parallel", "arbitrary"),
                                             vmem_limit_bytes=VMEM_LIMIT),
        name="dsa_sample",
    )(page_table, *([cache_kidx] * npp), *([ck] * npp), *([cv] * npp),
      qi_rows, qbd, w_rows, r3(kbf), r3(vbf), r3(smallbf), bias_s)


def _rms(x, g):
    return x * lax.rsqrt(jnp.mean(x * x, axis=-1, keepdims=True) + EPS) * g


def _mlp_body(x_ref, ma_ref, ob_ref, woa_ref, wob_ref, gffn_ref, wg_ref, wu_ref, wd_ref, gfin_ref, y_ref):
    h = x_ref[...] + (_mm(ma_ref[...], woa_ref[...]) + _mm(ob_ref[...], wob_ref[...]))
    hn = _rms(h, gffn_ref[...]).astype(BF16)
    act = _silu(_mm(hn, wg_ref[...])) * _mm(hn, wu_ref[...])
    y_ref[...] = _rms(h + _mm(act.astype(BF16), wd_ref[...]), gfin_ref[...])


def _mlp(x2d, mix_a, o_b, w_out, g_ffn, w_gate, w_up, w_down, g_final, tm):
    n = x2d.shape[0]
    assert n % tm == 0
    row = lambda w: pl.BlockSpec((tm, w), lambda i: (i, 0))
    woa, wob = w_out[:A_DIM].astype(BF16), w_out[A_DIM:].astype(BF16)
    return pl.pallas_call(
        _mlp_body,
        grid=(n // tm,),
        in_specs=[row(D_MODEL), row(A_DIM), row(B_DIM), _const_spec(woa.shape), _const_spec(wob.shape),
                  _const_spec((1, D_MODEL)), _const_spec(w_gate.shape), _const_spec(w_up.shape),
                  _const_spec(w_down.shape), _const_spec((1, D_MODEL))],
        out_specs=row(D_MODEL),
        out_shape=jax.ShapeDtypeStruct((n, D_MODEL), F32),
        compiler_params=pltpu.CompilerParams(dimension_semantics=("parallel",), vmem_limit_bytes=VMEM_LIMIT),
        name="mlp",
    )(x2d, mix_a, o_b, woa, wob, g_ffn.reshape(1, D_MODEL), w_gate.astype(BF16), w_up.astype(BF16),
      w_down.astype(BF16), g_final.reshape(1, D_MODEL))


def kernel(x_prompt, x_sample, cache_k, cache_v, cache_kidx, state_conv, state_ssm, page_table, meta_tokens, norm_mix, w_in, conv_w, a_log, dt_bias, o_norm, w_out, rel_bias, norm_ffn, w_gate, w_up, w_down, norm_final):
    assert w_in.shape[0] == 1, "single-layer stack"
    bsz, seq, _ = x_prompt.shape
    dbs, n_new, _ = x_sample.shape
    t_pad = FRONT + seq
    lo = FRONT - N_META
    past_len = page_table.shape[1] * PAGE
    topk_p = min(TOPK_MAX, seq // 4)
    topk_s = min(TOPK_MAX, (past_len + n_new) // 4)
    w_parts = _split_w_in(w_in[0])
    mlp_w = (w_out[0], norm_ffn[0], w_gate[0], w_up[0], w_down[0], norm_final)
    gdn_w = (conv_w[0], a_log[0], dt_bias[0], o_norm[0])

    meta = jnp.broadcast_to(meta_tokens.astype(F32)[None], (bsz, N_META, D_MODEL))
    xp = jnp.concatenate([jnp.zeros((bsz, lo, D_MODEL), F32), meta, x_prompt], axis=1).reshape(bsz * t_pad, D_MODEL)
    qkva, z, qb, k, v, kbf, vbf, qi, small, smallbf, small_t = _proj(xp, norm_mix[0], w_parts, 512)
    mix_a, ssm_p = _gdn_prompt(qkva, z, small, small_t, *gdn_w, bsz, t_pad, lo)
    bias_tab = _bias_tab(rel_bias, t_pad // LANES)
    o_b = _dsa_prompt(qb, qi, small, kbf, vbf, smallbf, bias_tab, bsz, t_pad, lo, topk_p)
    y_p = _mlp(x_prompt.reshape(bsz * seq, D_MODEL), mix_a.reshape(bsz * seq, A_DIM), o_b.reshape(bsz * seq, B_DIM),
               *mlp_w, 512).reshape(bsz, seq, D_MODEL)
    real = lambda a: a.reshape(bsz, t_pad, a.shape[-1])[:, lo:]
    k_p = real(k).reshape(1, bsz, N_META + seq, HEADS, HD)
    v_p = real(v).reshape(1, bsz, N_META + seq, HEADS, HD)
    kidx_p = real(small)[:, :, S_KIDX:S_KIDX + HD][None]
    conv_p = real(qkva)[:, -(CONV_WIDTH - 1):][None]

    xs = x_sample.reshape(dbs * n_new, D_MODEL)
    qkva_s, z_s, qb_s, k_s, v_s, kbf_s, vbf_s, qi_s, small_s, smallbf_s, small_t_s = _proj(xs, norm_mix[0], w_parts, 512)
    mix_a_s, ssm_s = _gdn_sample(qkva_s, z_s, small_s, small_t_s, state_conv[0], state_ssm[0], *gdn_w, n_new)
    bias_s = _bias_sample(rel_bias, past_len, past_len // PAGE + 1)
    o_b_s = _dsa_sample(qb_s, qi_s, small_s, kbf_s, vbf_s, smallbf_s, cache_k[0], cache_v[0], cache_kidx[0], page_table,
                        bias_s, n_new, topk_s)
    y_s = _mlp(xs, mix_a_s, o_b_s.reshape(dbs * n_new, B_DIM), *mlp_w, 512).reshape(dbs, n_new, D_MODEL)
    k_sn = k_s.reshape(1, dbs, n_new, HEADS, HD)
    v_sn = v_s.reshape(1, dbs, n_new, HEADS, HD)
    kidx_s = small_s.reshape(dbs, n_new, LANES)[:, :, S_KIDX:S_KIDX + HD][None]
    conv_s = jnp.concatenate([state_conv[0].astype(F32), qkva_s.reshape(dbs, n_new, 3 * A_DIM)],
                             axis=1)[:, -(CONV_WIDTH - 1):][None]
    return (y_p, y_s, k_p, v_p, kidx_p, conv_p, ssm_p[None], k_sn, v_sn, kidx_s, conv_s, ssm_s[None])
```

```python
import functools
import math

import jax
import jax.numpy as jnp
from jax import lax
from jax.experimental import pallas as pl
from jax.experimental.pallas import tpu as pltpu

F32 = jnp.float32
BF16 = jnp.bfloat16
I32 = jnp.int32
HI = lax.Precision.HIGHEST

D_MODEL = 1024
N_META = 16
HEADS = 8
HD = 64
A_DIM = HEADS * HD
B_DIM = HEADS * HD
CONV_WIDTH = 4
GDN_CHUNK = 64
TOPK_MAX = 256
PAGE = 128
REL_BUCKETS = 32
REL_MAX_DIST = 1024
D_FF = 2816
EPS = 1e-6
LANES = 128
SUBLANES = 8
FRONT = 128
NEG = -1e30
INT_MIN = -2 ** 31
VMEM_LIMIT = 56 * 1024 * 1024

S_KIDX, S_BETA, S_A, S_W = 0, 64, 72, 80


def _mm(a, b):
    return jnp.dot(a, b, preferred_element_type=F32)


def _mm_hi(a, b):
    return jnp.dot(a, b, preferred_element_type=F32, precision=HI)


def _split_bf16(x):
    hi = x.astype(BF16)
    return hi, (x - hi.astype(F32)).astype(BF16)


def _mm_solve(a, b):
    a_hi, a_lo = _split_bf16(a)
    b_hi, b_lo = _split_bf16(b)
    return _mm(a_hi, b_hi) + (_mm(a_hi, b_lo) + _mm(a_lo, b_hi))


def _nt(a, b, precision=None):
    return lax.dot_general(a, b, (((1,), (1,)), ((), ())), preferred_element_type=F32, precision=precision)


def _tn(a, b, precision=None):
    return lax.dot_general(a, b, (((0,), (0,)), ((), ())), preferred_element_type=F32, precision=precision)


def _const_spec(shape):
    nd = len(shape)
    return pl.BlockSpec(shape, lambda *_: (0,) * nd, pipeline_mode=pl.Buffered(1))


def _silu(x):
    return x * jax.nn.sigmoid(x)


def _proj_body(x_ref, g_ref, wa_ref, wz_ref, wb_ref, wqi_ref, ws_ref, wst_ref,
               qkva_ref, z_ref, qb_ref, k_ref, v_ref, kbf_ref, vbf_ref, qi_ref, small_ref, smallbf_ref, smallt_ref):
    x = x_ref[...]
    ms = jnp.mean(x * x, axis=-1, keepdims=True)
    h = (x * lax.rsqrt(ms + EPS) * g_ref[...]).astype(BF16)
    qkva_ref[...] = _mm(h, wa_ref[...])
    z_ref[...] = _mm(h, wz_ref[...])
    qkvb = _mm(h, wb_ref[...])
    qb_ref[...] = (qkvb[:, :B_DIM] * 0.125).astype(BF16)
    k = qkvb[:, B_DIM:2 * B_DIM]
    v = qkvb[:, 2 * B_DIM:]
    k_ref[...] = k
    v_ref[...] = v
    kbf_ref[...] = k.astype(BF16)
    vbf_ref[...] = v.astype(BF16)
    qi_ref[...] = (_mm(h, wqi_ref[...]) * 0.125).astype(BF16)
    s = _mm(h, ws_ref[...])
    small_ref[...] = s
    smallbf_ref[...] = s.astype(BF16)
    smallt_ref[...] = _nt(wst_ref[...], h)


def _split_w_in(w_in):
    c = 0
    parts = []
    for n in (3 * A_DIM, A_DIM, HEADS, HEADS, 3 * B_DIM, HEADS * HD, HD, HEADS):
        parts.append(w_in[:, c:c + n])
        c += n
    w_qkva, w_z, w_beta, w_a, w_qkvb, w_qi, w_kidx, w_w = parts
    pad = jnp.zeros((w_in.shape[0], LANES - (HD + 3 * HEADS)), w_in.dtype)
    w_small = jnp.concatenate([w_kidx, w_beta, w_a, w_w, pad], axis=1)
    bf = lambda t: t.astype(BF16)
    return bf(w_qkva), bf(w_z), bf(w_qkvb), bf(w_qi), bf(w_small), bf(w_small.T)


def _proj(x2d, g, w_parts, tm):
    n = x2d.shape[0]
    assert n % tm == 0 and tm % LANES == 0
    wa, wz, wb, wqi, ws, wst = w_parts
    row = lambda w: pl.BlockSpec((tm, w), lambda i: (i, 0))
    outs = [
        (jax.ShapeDtypeStruct((n, 3 * A_DIM), F32), row(3 * A_DIM)),
        (jax.ShapeDtypeStruct((n, A_DIM), F32), row(A_DIM)),
        (jax.ShapeDtypeStruct((n, B_DIM), BF16), row(B_DIM)),
        (jax.ShapeDtypeStruct((n, B_DIM), F32), row(B_DIM)),
        (jax.ShapeDtypeStruct((n, B_DIM), F32), row(B_DIM)),
        (jax.ShapeDtypeStruct((n, B_DIM), BF16), row(B_DIM)),
        (jax.ShapeDtypeStruct((n, B_DIM), BF16), row(B_DIM)),
        (jax.ShapeDtypeStruct((n, HEADS * HD), BF16), row(HEADS * HD)),
        (jax.ShapeDtypeStruct((n, LANES), F32), row(LANES)),
        (jax.ShapeDtypeStruct((n, LANES), BF16), row(LANES)),
        (jax.ShapeDtypeStruct((LANES, n), F32), pl.BlockSpec((LANES, tm), lambda i: (0, i))),
    ]
    return pl.pallas_call(
        _proj_body,
        grid=(n // tm,),
        in_specs=[row(D_MODEL), _const_spec((1, D_MODEL)), _const_spec(wa.shape), _const_spec(wz.shape),
                  _const_spec(wb.shape), _const_spec(wqi.shape), _const_spec(ws.shape), _const_spec(wst.shape)],
        out_specs=[o[1] for o in outs],
        out_shape=[o[0] for o in outs],
        compiler_params=pltpu.CompilerParams(dimension_semantics=("parallel",), vmem_limit_bytes=VMEM_LIMIT),
        name="proj",
    )(x2d, g.reshape(1, D_MODEL), wa, wz, wb, wqi, ws, wst)


def _gdn_prep(qkv, tail, conv_w, small, small_t, a_log_row, dt_row, a_log_col, dt_col, valid_col, valid_row, chunk):
    rows = qkv.shape[0]
    n_chunks = rows // chunk
    ys = []
    for c in range(n_chunks):
        ext = jnp.concatenate([tail[c], qkv[c * chunk:(c + 1) * chunk]], axis=0)
        y = sum(ext[SUBLANES - (CONV_WIDTH - 1) + j: SUBLANES - (CONV_WIDTH - 1) + j + chunk] * conv_w[j:j + 1]
                for j in range(CONV_WIDTH))
        ys.append(_silu(y))
    y = ys[0] if n_chunks == 1 else jnp.concatenate(ys, axis=0)
    beta_c = jnp.where(valid_col, jax.nn.sigmoid(small), 0.0)
    beta_r = jnp.where(valid_row, jax.nn.sigmoid(small_t), 0.0)
    g_c = jnp.where(valid_col, -jnp.exp(a_log_row) * jax.nn.softplus(small + dt_row), 0.0)
    g_r = jnp.where(valid_row, -jnp.exp(a_log_col) * jax.nn.softplus(small_t + dt_col), 0.0)
    ri = lax.broadcasted_iota(I32, (rows, rows), 0)
    ci = lax.broadcasted_iota(I32, (rows, rows), 1)
    same = (ri // chunk) == (ci // chunk)
    low = jnp.where(same & (ci <= ri), 1.0, 0.0).astype(F32)
    b_c = _mm_hi(low, g_c)
    b_r = _nt(g_r, low, precision=HI)
    return y, beta_c, b_c, b_r


def _gdn_head(qh, kh, vh, beta_c, b_c, b_r, s0, o_norm, zh, chunk):
    c = chunk
    qn = qh * lax.rsqrt(jnp.sum(qh * qh, axis=-1, keepdims=True) + EPS) * (HD ** -0.5)
    kn = kh * lax.rsqrt(jnp.sum(kh * kh, axis=-1, keepdims=True) + EPS)
    ri = lax.broadcasted_iota(I32, (c, c), 0)
    ci = lax.broadcasted_iota(I32, (c, c), 1)
    incl = ci <= ri
    strict = ci < ri
    decay = jnp.where(incl, jnp.exp(jnp.where(incl, b_c - b_r, 0.0)), 0.0)
    kb, qb16, s0b = kn.astype(BF16), qn.astype(BF16), s0.astype(BF16)
    kk = _nt(kb, kb)
    qk = _nt(qb16, kb)
    a_mat = jnp.where(strict, beta_c * kk * decay, 0.0)
    eb = jnp.exp(b_c)
    y = jnp.concatenate([beta_c * vh, (beta_c * eb) * kn], axis=1)
    bk = -a_mat
    n_levels = int(math.log2(c))
    for lvl in range(n_levels):
        if lvl + 1 < n_levels:
            prod = _mm_solve(bk, jnp.concatenate([y, bk], axis=1))
            y, bk = y + prod[:, :2 * HD], prod[:, 2 * HD:]
        else:
            y = y + _mm_solve(bk, y)
    u, w = y[:, :HD], y[:, HD:]
    delta = u - _mm(w.astype(BF16), s0b)
    deltab = delta.astype(BF16)
    o = eb * _mm(qb16, s0b) + _mm((qk * decay).astype(BF16), deltab)
    b_last = b_c[c - 1:c, :]
    s_new = jnp.exp(b_last) * s0 + _tn((kn * jnp.exp(b_last - b_c)).astype(BF16), deltab)
    on = o * lax.rsqrt(jnp.mean(o * o, axis=-1, keepdims=True) + EPS) * o_norm
    return s_new, on * _silu(zh)


def _bdot(a, b, ca, cb):
    return lax.dot_general(a, b, (((ca,), (cb,)), ((0,), (0,))), preferred_element_type=F32)


def _bmm_solve(a, b):
    a_hi, a_lo = _split_bf16(a)
    b_hi, b_lo = _split_bf16(b)
    return _bdot(a_hi, b_hi, 2, 1) + (_bdot(a_hi, b_lo, 2, 1) + _bdot(a_lo, b_hi, 2, 1))


def _gdn_solve(q, k, v, beta, b_c, b_r, chunk):
    c = chunk
    qn = q * lax.rsqrt(jnp.sum(q * q, axis=-1, keepdims=True) + EPS) * (HD ** -0.5)
    kn = k * lax.rsqrt(jnp.sum(k * k, axis=-1, keepdims=True) + EPS)
    ri = lax.broadcasted_iota(I32, (1, c, c), 1)
    ci = lax.broadcasted_iota(I32, (1, c, c), 2)
    incl = ci <= ri
    decay = jnp.where(incl, jnp.exp(jnp.where(incl, b_c - b_r, 0.0)), 0.0)
    kb, qb16 = kn.astype(BF16), qn.astype(BF16)
    kk = _bdot(kb, kb, 2, 2)
    qk = _bdot(qb16, kb, 2, 2)
    bk = -jnp.where(ci < ri, beta * kk * decay, 0.0)
    eb = jnp.exp(b_c)
    y = jnp.concatenate([beta * v, (beta * eb) * kn], axis=2)
    n_levels = int(math.log2(c))
    for lvl in range(n_levels):
        if lvl + 1 < n_levels:
            prod = _bmm_solve(bk, jnp.concatenate([y, bk], axis=2))
            y, bk = y + prod[:, :, :2 * HD], prod[:, :, 2 * HD:]
        else:
            y = y + _bmm_solve(bk, y)
    b_last = b_c[:, c - 1:c, :]
    return (y[:, :, :HD], y[:, :, HD:].astype(BF16), (qk * decay).astype(BF16), qb16,
            (kn * jnp.exp(b_last - b_c)).astype(BF16), eb, jnp.exp(b_last))


def _gdn_apply(u, w, attn, qb16, kd, eb, eb_last, s0, o_norm, z):
    s0b = s0.astype(BF16)
    delta = u - _bdot(w, s0b, 2, 1)
    deltab = delta.astype(BF16)
    o = eb * _bdot(qb16, s0b, 2, 1) + _bdot(attn, deltab, 2, 1)
    s_new = eb_last * s0 + _bdot(kd, deltab, 1, 1)
    on = o * lax.rsqrt(jnp.mean(o * o, axis=-1, keepdims=True) + EPS) * o_norm
    return s_new, on * _silu(z)


def _gdn_chunk_all_heads(y, beta_c, b_c, b_r, z, o_norm, r0, chunk, get_s0, put_s, put_o):
    for h in range(HEADS):
        sl = slice(r0, r0 + chunk)
        qh = y[sl, h * HD:(h + 1) * HD]
        kh = y[sl, A_DIM + h * HD:A_DIM + (h + 1) * HD]
        vh = y[sl, 2 * A_DIM + h * HD:2 * A_DIM + (h + 1) * HD]
        bc = beta_c[sl, S_BETA + h:S_BETA + h + 1]
        bcol = b_c[sl, S_A + h:S_A + h + 1]
        brow = b_r[S_A + h:S_A + h + 1, r0:r0 + chunk]
        s_new, out = _gdn_head(qh, kh, vh, bc, bcol, brow, get_s0(h), o_norm, z[sl, h * HD:(h + 1) * HD], chunk)
        put_s(h, s_new)
        put_o(h, out)


def _gdn_prompt_body(qkv_ref, z_ref, small_ref, smallt_ref, convw_ref, alr_ref, dtr_ref, alc_ref, dtc_ref, onorm_ref,
                     mix_ref, s_ref, tail_scr, *, rows, lo):
    t = pl.program_id(1)

    @pl.when(t == 0)
    def _():
        s_ref[...] = jnp.zeros_like(s_ref)
        tail_scr[...] = jnp.zeros_like(tail_scr)

    chunk = GDN_CHUNK
    qkv = qkv_ref[0]
    tails = [tail_scr[...]] + [qkv[c * chunk - SUBLANES:c * chunk] for c in range(1, rows // chunk)]
    pos_c = t * rows + lax.broadcasted_iota(I32, (rows, LANES), 0)
    pos_r = t * rows + lax.broadcasted_iota(I32, (LANES, rows), 1)
    y, beta_c, b_c, b_r = _gdn_prep(qkv, tails, convw_ref[...], small_ref[0], smallt_ref[...],
                                    alr_ref[...], dtr_ref[...], alc_ref[...], dtc_ref[...],
                                    pos_c >= lo, pos_r >= lo, chunk)
    tail_scr[...] = qkv[rows - SUBLANES:rows]
    z = z_ref[0]
    n_chunks = rows // chunk
    units = [(c, h) for c in range(n_chunks) for h in range(HEADS)]
    rs = lambda c: slice(c * chunk, (c + 1) * chunk)
    stack = lambda a, off: jnp.stack([a[rs(c), off + h * HD:off + (h + 1) * HD] for c, h in units])
    col = lambda a, off: jnp.stack([a[rs(c), off + h:off + h + 1] for c, h in units])
    sol = _gdn_solve(stack(y, 0), stack(y, A_DIM), stack(y, 2 * A_DIM), col(beta_c, S_BETA), col(b_c, S_A),
                     jnp.stack([b_r[S_A + h:S_A + h + 1, rs(c)] for c, h in units]), chunk)
    z3 = stack(z, 0)
    o_norm = onorm_ref[...]
    for c in range(n_chunks):
        g = slice(c * HEADS, (c + 1) * HEADS)
        s_new, out = _gdn_apply(*[t[g] for t in sol], s_ref[0], o_norm, z3[g])
        s_ref[0] = s_new
        for h in range(HEADS):
            mix_ref[0, rs(c), h * HD:(h + 1) * HD] = out[h].astype(mix_ref.dtype)


def _gate_params(a_log, dt_bias):
    row = lambda v: jnp.zeros((1, LANES), F32).at[0, S_A:S_A + HEADS].set(v.astype(F32))
    return row(a_log), row(dt_bias), row(a_log).T, row(dt_bias).T


def _gdn_prompt(qkva, z, small, small_t, conv_w, a_log, dt_bias, o_norm, bsz, t_pad, lo):
    rows = LANES
    nt = t_pad // rows
    alr, dtr, alc, dtc = _gate_params(a_log, dt_bias)
    body = functools.partial(_gdn_prompt_body, rows=rows, lo=lo)
    mix, s = pl.pallas_call(
        body,
        grid=(bsz, nt),
        in_specs=[
            pl.BlockSpec((1, rows, 3 * A_DIM), lambda b, t: (b, t, 0)),
            pl.BlockSpec((1, rows, A_DIM), lambda b, t: (b, t, 0)),
            pl.BlockSpec((1, rows, LANES), lambda b, t: (b, t, 0)),
            pl.BlockSpec((LANES, rows), lambda b, t: (0, b * nt + t)),
            _const_spec((CONV_WIDTH, 3 * A_DIM)), _const_spec((1, LANES)), _const_spec((1, LANES)),
            _const_spec((LANES, 1)), _const_spec((LANES, 1)), _const_spec((1, HD)),
        ],
        out_specs=[
            pl.BlockSpec((1, rows, A_DIM), lambda b, t: (b, jnp.maximum(t - 1, 0), 0)),
            pl.BlockSpec((1, HEADS, HD, HD), lambda b, t: (b, 0, 0, 0)),
        ],
        out_shape=[jax.ShapeDtypeStruct((bsz, t_pad - FRONT, A_DIM), BF16),
                   jax.ShapeDtypeStruct((bsz, HEADS, HD, HD), F32)],
        scratch_shapes=[pltpu.VMEM((SUBLANES, 3 * A_DIM), F32)],
        compiler_params=pltpu.CompilerParams(dimension_semantics=("parallel", "arbitrary"),
                                             vmem_limit_bytes=VMEM_LIMIT),
        name="gdn_prompt",
    )(qkva.reshape(bsz, t_pad, 3 * A_DIM), z.reshape(bsz, t_pad, A_DIM), small.reshape(bsz, t_pad, LANES), small_t,
      conv_w, alr, dtr, alc, dtc, o_norm.reshape(1, HD))
    return mix, s


def _gdn_sample_body(qkv_ref, z_ref, small_ref, smallt_ref, conv0_ref, s0_ref, convw_ref, alr_ref, dtr_ref, alc_ref,
                     dtc_ref, onorm_ref, mix_ref, s_ref, *, chunk):
    rows = LANES
    n_seq = rows // chunk
    qkv = qkv_ref[...]
    tails = [conv0_ref[i] for i in range(n_seq)]
    y, beta_c, b_c, b_r = _gdn_prep(qkv, tails, convw_ref[...], small_ref[...], smallt_ref[...],
                                    alr_ref[...], dtr_ref[...], alc_ref[...], dtc_ref[...], True, True, chunk)
    z = z_ref[...]
    o_norm = onorm_ref[...]
    for i in range(n_seq):
        def put_s(h, s, i=i):
            s_ref[i, h] = s

        def put_o(h, o, i=i):
            mix_ref[i * chunk:(i + 1) * chunk, h * HD:(h + 1) * HD] = o.astype(mix_ref.dtype)

        _gdn_chunk_all_heads(y, beta_c, b_c, b_r, z, o_norm, i * chunk, chunk, lambda h, i=i: s0_ref[i, h], put_s, put_o)


def _gdn_sample(qkva, z, small, small_t, state_conv, state_ssm, conv_w, a_log, dt_bias, o_norm, n_new):
    n = qkva.shape[0]
    bsz = n // n_new
    rows = LANES
    n_seq = rows // n_new
    assert n_new == SUBLANES and n % rows == 0
    alr, dtr, alc, dtc = _gate_params(a_log, dt_bias)
    conv0 = jnp.pad(state_conv.astype(F32), ((0, 0), (SUBLANES - (CONV_WIDTH - 1), 0), (0, 0)))
    body = functools.partial(_gdn_sample_body, chunk=n_new)
    mix, s = pl.pallas_call(
        body,
        grid=(n // rows,),
        in_specs=[
            pl.BlockSpec((rows, 3 * A_DIM), lambda g: (g, 0)),
            pl.BlockSpec((rows, A_DIM), lambda g: (g, 0)),
            pl.BlockSpec((rows, LANES), lambda g: (g, 0)),
            pl.BlockSpec((LANES, rows), lambda g: (0, g)),
            pl.BlockSpec((n_seq, SUBLANES, 3 * A_DIM), lambda g: (g, 0, 0)),
            pl.BlockSpec((n_seq, HEADS, HD, HD), lambda g: (g, 0, 0, 0)),
            _const_spec((CONV_WIDTH, 3 * A_DIM)), _const_spec((1, LANES)), _const_spec((1, LANES)),
            _const_spec((LANES, 1)), _const_spec((LANES, 1)), _const_spec((1, HD)),
        ],
        out_specs=[
            pl.BlockSpec((rows, A_DIM), lambda g: (g, 0)),
            pl.BlockSpec((n_seq, HEADS, HD, HD), lambda g: (g, 0, 0, 0)),
        ],
        out_shape=[jax.ShapeDtypeStruct((n, A_DIM), BF16), jax.ShapeDtypeStruct((bsz, HEADS, HD, HD), F32)],
        compiler_params=pltpu.CompilerParams(dimension_semantics=("parallel",), vmem_limit_bytes=VMEM_LIMIT),
        name="gdn_sample",
    )(qkva, z, small, small_t, conv0, state_ssm.astype(F32), conv_w, alr, dtr, alc, dtc, o_norm.reshape(1, HD))
    return mix, s


def _rel_bucket(d):
    d = jnp.maximum(d, 0)
    max_exact = REL_BUCKETS // 2
    df = jnp.maximum(d, 1).astype(F32)
    large = max_exact + (jnp.log(df / max_exact) / math.log(REL_MAX_DIST / max_exact)
                         * (REL_BUCKETS - max_exact)).astype(I32)
    large = jnp.minimum(large, REL_BUCKETS - 1)
    return jnp.where(d < max_exact, d, large)


def _bias_lookup(bucket, rb_ref, h):
    acc = jnp.zeros(bucket.shape, F32)
    for b in range(REL_BUCKETS):
        acc = jnp.where(bucket == b, rb_ref[b, h], acc)
    return acc


def _bias_tab_body(rb_ref, out_ref):
    dlt = pl.program_id(0)
    i = lax.broadcasted_iota(I32, (LANES, LANES), 0)
    j = lax.broadcasted_iota(I32, (LANES, LANES), 1)
    bucket = _rel_bucket(dlt * LANES + i - j)
    for h in range(HEADS):
        out_ref[h, 0] = _bias_lookup(bucket, rb_ref, h)


def _bias_tab(rel_bias, n_tiles):
    return pl.pallas_call(
        _bias_tab_body,
        grid=(n_tiles,),
        in_specs=[pl.BlockSpec(memory_space=pltpu.SMEM)],
        out_specs=pl.BlockSpec((HEADS, 1, LANES, LANES), lambda t: (0, t, 0, 0)),
        out_shape=jax.ShapeDtypeStruct((HEADS, n_tiles, LANES, LANES), F32),
        name="bias_tab",
    )(rel_bias.astype(F32))


def _bias_sample_body(rb_ref, out_ref, *, past_len):
    blk = pl.program_id(0)
    q = lax.broadcasted_iota(I32, (SUBLANES, LANES), 0)
    s = blk * LANES + lax.broadcasted_iota(I32, (SUBLANES, LANES), 1)
    bucket = _rel_bucket(past_len + q - s)
    for h in range(HEADS):
        out_ref[h * SUBLANES:(h + 1) * SUBLANES, :] = _bias_lookup(bucket, rb_ref, h)


def _bias_sample(rel_bias, past_len, n_blocks):
    return pl.pallas_call(
        functools.partial(_bias_sample_body, past_len=past_len),
        grid=(n_blocks,),
        in_specs=[pl.BlockSpec(memory_space=pltpu.SMEM)],
        out_specs=pl.BlockSpec((HEADS * SUBLANES, LANES), lambda t: (0, t)),
        out_shape=jax.ShapeDtypeStruct((HEADS * SUBLANES, n_blocks * LANES), F32),
        name="bias_sample",
    )(rel_bias.astype(F32))


def _sort_key(score, admissible):
    score = jnp.where(score == 0.0, 0.0, score)
    bits = pltpu.bitcast(score, I32)
    key = jnp.where(bits < 0, bits ^ 0x7FFFFFFF, bits)
    return jnp.where(admissible, key, INT_MIN)


def _kth_largest_key(count_ge, rows, topk):
    c0 = jnp.broadcast_to(count_ge(jnp.zeros((rows, LANES), I32)), (rows, LANES))
    t0 = jnp.where(c0 >= topk, 0, INT_MIN).astype(I32)
    c0 = jnp.where(c0 >= topk, c0, float(2 ** 30))

    def bit_step(i, carry):
        t, c = carry
        cand = t + lax.shift_left(jnp.int32(1), 30 - i)
        cc = jnp.broadcast_to(count_ge(cand), (rows, LANES))
        ok = cc >= topk
        return jnp.where(ok, cand, t), jnp.where(ok, cc, c)

    return lax.fori_loop(0, 31, bit_step, (t0, c0))


def _tie_mask(key, thr, need, running, tri):
    eq = (key == thr) & (key != INT_MIN)
    pref = _mm(jnp.where(eq, 1.0, 0.0).astype(BF16), tri) + running
    sel = (key > thr) | (eq & (pref <= need))
    return sel, jnp.broadcast_to(pref[:, LANES - 1:LANES], pref.shape)


def _tri_incl():
    r = lax.broadcasted_iota(I32, (LANES, LANES), 0)
    c = lax.broadcasted_iota(I32, (LANES, LANES), 1)
    return jnp.where(r <= c, 1.0, 0.0).astype(BF16)


DSA_BUCKETS = (5, 9, 13, 17)


def _dsa_prompt_tile(j, nblk, qi_ref, small_ref, kidx_ref, bias_ref, out_ref,
                     qh_scr, kh_scr, vh_scr, keys_scr, mbias_scr, o_scr, *, lo, topk):
    tq = LANES
    wid = nblk * LANES
    t_pos = j * tq + lax.broadcasted_iota(I32, (tq, wid), 0)
    s_pos = lax.broadcasted_iota(I32, (tq, wid), 1)
    w = small_ref[0][:, S_W:S_W + HEADS] * (HEADS ** -0.5)
    qi = qi_ref[0]
    kx = kidx_ref[0, 0:wid, S_KIDX:S_KIDX + HD]
    acc = jnp.zeros((tq, wid), F32)
    for h in range(HEADS):
        acc = acc + jnp.maximum(_nt(qi[:, h * HD:(h + 1) * HD], kx), 0.0) * w[:, h:h + 1]
    keys_scr[:, 0:wid] = _sort_key(acc, (s_pos <= t_pos) & (s_pos >= lo))

    def count_ge(cand):
        return jnp.sum(jnp.where(keys_scr[:, 0:wid] >= cand[:, 0:1], 1.0, 0.0), axis=1, keepdims=True)

    thr, cnt = _kth_largest_key(count_ge, tq, topk)
    keys = keys_scr[:, 0:wid]
    mbias_scr[:, 0:wid] = jnp.where(keys >= jnp.maximum(thr[:, 0:1], INT_MIN + 1), 0.0, NEG)

    @pl.when(jnp.max(jnp.where(cnt < float(2 ** 30), cnt, 0.0)) > topk)
    def _():
        need = topk - jnp.sum(jnp.where(keys_scr[:, 0:wid] > thr[:, 0:1], 1.0, 0.0), axis=1, keepdims=True)
        tri = _tri_incl()
        running = jnp.zeros((tq, LANES), F32)
        for jb in range(nblk):
            cols = slice(jb * LANES, (jb + 1) * LANES)
            sel, running = _tie_mask(keys_scr[:, cols], thr, need, running, tri)
            mbias_scr[:, cols] = jnp.where(sel, 0.0, NEG)

    def head_attn(h, carry):
        bias = jnp.concatenate([bias_ref[h, jnp.maximum(j - jb, 0)] for jb in range(nblk)], axis=1)
        lg = _nt(qh_scr[h], kh_scr[h, 0:wid, :]) + bias + mbias_scr[:, 0:wid]
        p = jnp.exp(lg - jnp.max(lg, axis=1, keepdims=True))
        inv = 1.0 / jnp.sum(p, axis=1, keepdims=True)
        o_scr[h] = _mm(p.astype(BF16), vh_scr[h, 0:wid, :]) * inv
        return carry

    lax.fori_loop(0, HEADS, head_attn, 0)
    for h in range(HEADS):
        out_ref[0, :, h * HD:(h + 1) * HD] = o_scr[h].astype(out_ref.dtype)


def _dsa_prompt_body(qb_ref, qi_ref, small_ref, kbf_ref, vbf_ref, kidx_ref, bias_ref, out_ref,
                     qh_scr, kh_scr, vh_scr, keys_scr, mbias_scr, o_scr, *, lo, topk, buckets):
    j = pl.program_id(1)

    @pl.when(j == 1)
    def _():
        for h in range(HEADS):
            kh_scr[h] = kbf_ref[0, :, h * HD:(h + 1) * HD]
            vh_scr[h] = vbf_ref[0, :, h * HD:(h + 1) * HD]

    @pl.when(j > 0)
    def _():
        for h in range(HEADS):
            qh_scr[h] = qb_ref[0, :, h * HD:(h + 1) * HD]

    lower = 0
    for nblk in buckets:
        @pl.when((j > lower) & (j + 1 <= nblk))
        def _(nblk=nblk):
            _dsa_prompt_tile(j, nblk, qi_ref, small_ref, kidx_ref, bias_ref, out_ref,
                             qh_scr, kh_scr, vh_scr, keys_scr, mbias_scr, o_scr, lo=lo, topk=topk)
        lower = nblk - 1


def _dsa_prompt(qb, qi, small, kbf, vbf, smallbf, bias_tab, bsz, t_pad, lo, topk):
    tq = LANES
    nq = t_pad // tq
    buckets = tuple(b for b in DSA_BUCKETS if b < nq) + (nq,)
    r3 = lambda a: a.reshape(bsz, t_pad, a.shape[-1])
    tile = lambda w: pl.BlockSpec((1, tq, w), lambda b, j: (b, j, 0))
    whole = lambda w: pl.BlockSpec((1, t_pad, w), lambda b, j: (b, 0, 0))
    return pl.pallas_call(
        functools.partial(_dsa_prompt_body, lo=lo, topk=topk, buckets=buckets),
        grid=(bsz, nq),
        in_specs=[tile(B_DIM), tile(HEADS * HD), tile(LANES), whole(B_DIM), whole(B_DIM), whole(LANES),
                  _const_spec(bias_tab.shape)],
        out_specs=pl.BlockSpec((1, tq, B_DIM), lambda b, j: (b, jnp.maximum(j - 1, 0), 0)),
        out_shape=jax.ShapeDtypeStruct((bsz, t_pad - FRONT, B_DIM), BF16),
        scratch_shapes=[pltpu.VMEM((HEADS, tq, HD), BF16), pltpu.VMEM((HEADS, t_pad, HD), BF16),
                        pltpu.VMEM((HEADS, t_pad, HD), BF16), pltpu.VMEM((tq, t_pad), I32),
                        pltpu.VMEM((tq, t_pad), F32), pltpu.VMEM((HEADS, tq, HD), F32)],
        compiler_params=pltpu.CompilerParams(dimension_semantics=("parallel", "arbitrary"),
                                             vmem_limit_bytes=VMEM_LIMIT),
        name="dsa_prompt",
    )(r3(qb), r3(qi), r3(small), r3(kbf), r3(vbf), r3(smallbf), bias_tab)


PAGES_PER_STEP = 8


def _dsa_sample_body(pt_ref, *refs, n_new, past_len, topk):
    npp = PAGES_PER_STEP
    kidx_refs, k_refs, v_refs = refs[:npp], refs[npp:2 * npp], refs[2 * npp:3 * npp]
    (qi_ref, wq_ref, w_ref, knew_ref, vnew_ref, kidxnew_ref, bias_ref, out_ref,
     keys_scr, logit_scr, v_scr) = refs[3 * npp:]
    del pt_ref
    s = pl.program_id(1)
    n_page_steps = past_len // (PAGE * npp)
    n_blocks = past_len // PAGE + 1
    qi = qi_ref[0]
    w = w_ref[0] * (HEADS ** -0.5)
    q_row = lax.broadcasted_iota(I32, (n_new, LANES), 0)
    lane = lax.broadcasted_iota(I32, (n_new, LANES), 1)

    def do_block(off, kx, k_head, v_head, admissible):
        rel = jnp.maximum(_nt(qi, kx), 0.0) * w
        sc = rel[0:n_new]
        for h in range(1, HEADS):
            sc = sc + rel[h * n_new:(h + 1) * n_new]
        keys_scr[:, pl.ds(off, LANES)] = _sort_key(sc, admissible)
        lg = _nt(wq_ref[0, 0], k_head(0))
        for h in range(1, HEADS):
            lg = lg + _nt(wq_ref[0, h], k_head(h))
        logit_scr[:, pl.ds(off, LANES)] = lg + bias_ref[:, pl.ds(off, LANES)]
        for h in range(HEADS):
            v_scr[h, pl.ds(off, LANES), :] = v_head(h)

    @pl.when(s < n_page_steps)
    def _():
        for i in range(npp):
            off = pl.multiple_of((s * npp + i) * PAGE, PAGE)
            head_rows = lambda ref, h: ref[0, pl.ds(h, PAGE, stride=HEADS), :].astype(BF16)
            do_block(off, kidx_refs[i][0].astype(BF16), functools.partial(head_rows, k_refs[i]),
                     functools.partial(head_rows, v_refs[i]), True)

    @pl.when(s == n_page_steps)
    def _():
        pad = lambda a: jnp.concatenate([a, jnp.zeros((PAGE - n_new, a.shape[1]), a.dtype)], axis=0)
        kx = pad(kidxnew_ref[0])[:, S_KIDX:S_KIDX + HD]
        new_rows = lambda ref, h: pad(ref[0, :, h * HD:(h + 1) * HD])
        do_block(past_len, kx, functools.partial(new_rows, knew_ref), functools.partial(new_rows, vnew_ref),
                 lane <= q_row)

        keys = keys_scr[...]
        thr, _ = _kth_largest_key(
            lambda cand: jnp.sum(jnp.where(keys >= cand[:, 0:1], 1.0, 0.0), axis=1, keepdims=True), n_new, topk)
        need = topk - jnp.sum(jnp.where(keys > thr[:, 0:1], 1.0, 0.0), axis=1, keepdims=True)
        tri = _tri_incl()

        def mask_blk(jb, running):
            cols = pl.ds(pl.multiple_of(jb * LANES, LANES), LANES)
            sel, running = _tie_mask(keys_scr[:, cols], thr, need, running, tri)
            mb = jnp.where(sel, 0.0, NEG)
            logit_scr[:, cols] = logit_scr[:, cols] + jnp.concatenate([mb] * HEADS, axis=0)
            return running

        lax.fori_loop(0, n_blocks, mask_blk, jnp.zeros((n_new, LANES), F32))
        lg = logit_scr[...]
        m = jnp.max(lg, axis=1, keepdims=True)
        p = jnp.exp(lg - m)
        inv = 1.0 / jnp.sum(p, axis=1, keepdims=True)
        pb = p.astype(BF16)
        for h in range(HEADS):
            rows = slice(h * n_new, (h + 1) * n_new)
            out_ref[0, :, h * HD:(h + 1) * HD] = (_mm(pb[rows], v_scr[h]) * inv[rows]).astype(out_ref.dtype)


def _dsa_sample(qb, qi, small, kbf, vbf, smallbf, cache_k, cache_v, cache_kidx, page_table, bias_s, n_new, topk):
    dbs, n_pages = page_table.shape
    past_len = n_pages * PAGE
    npp = PAGES_PER_STEP
    assert n_pages % npp == 0 and n_new == SUBLANES
    n_page_steps = n_pages // npp
    n_blocks = n_pages + 1
    n_pool = cache_k.shape[0]
    qi_rows = qi.reshape(dbs, n_new, HEADS, HD).transpose(0, 2, 1, 3).reshape(dbs, HEADS * n_new, HD)
    q4 = qb.reshape(dbs, n_new, HEADS, HD).transpose(0, 2, 1, 3)
    eye = jnp.eye(HEADS, dtype=qb.dtype)
    wq = (q4[:, :, None, :, :] * eye[None, :, :, None, None]).reshape(dbs, HEADS, HEADS * n_new, HD)
    w_rows = small.reshape(dbs, n_new, LANES)[:, :, S_W:S_W + HEADS].transpose(0, 2, 1).reshape(dbs, HEADS * n_new, 1)
    w_rows = jnp.broadcast_to(w_rows, (dbs, HEADS * n_new, LANES))

    def page_spec(rows, i):
        def imap(b, s, pt):
            return (pt[b, jnp.minimum(s, n_page_steps - 1) * npp + i], 0, 0)
        return pl.BlockSpec((1, rows, HD), imap)

    per_seq = lambda r, w: pl.BlockSpec((1, r, w), lambda b, s, pt: (b, 0, 0))
    in_specs = ([page_spec(PAGE, i) for i in range(npp)] + [page_spec(PAGE * HEADS, i) for i in range(npp)]
                + [page_spec(PAGE * HEADS, i) for i in range(npp)]
                + [per_seq(HEADS * n_new, HD),
                   pl.BlockSpec((1, HEADS, HEADS * n_new, HD), lambda b, s, pt: (b, 0, 0, 0)),
                   per_seq(HEADS * n_new, LANES), per_seq(n_new, B_DIM), per_seq(n_new, B_DIM), per_seq(n_new, LANES),
                   pl.BlockSpec(bias_s.shape, lambda b, s, pt: (0, 0), pipeline_mode=pl.Buffered(1))])
    grid_spec = pltpu.PrefetchScalarGridSpec(
        num_scalar_prefetch=1,
        grid=(dbs, n_page_steps + 1),
        in_specs=in_specs,
        out_specs=pl.BlockSpec((1, n_new, B_DIM), lambda b, s, pt: (b, 0, 0)),
        scratch_shapes=[pltpu.VMEM((n_new, n_blocks * LANES), I32), pltpu.VMEM((HEADS * n_new, n_blocks * LANES), F32),
                        pltpu.VMEM((HEADS, n_blocks * LANES, HD), BF16)],
    )
    ck = cache_k.reshape(n_pool, PAGE * HEADS, HD)
    cv = cache_v.reshape(n_pool, PAGE * HEADS, HD)
    r3 = lambda a: a.reshape(dbs, n_new, a.shape[-1])
    return pl.pallas_call(
        functools.partial(_dsa_sample_body, n_new=n_new, past_len=past_len, topk=topk),
        grid_spec=grid_spec,
        out_shape=jax.ShapeDtypeStruct((dbs, n_new, B_DIM), BF16),
        compiler_params=pltpu.CompilerParams(dimension_semantics=("parallel", "arbitrary"),
                                             vmem_limit_bytes=VMEM_LIMIT),
        name="dsa_sample",
    )(page_table, *([cache_kidx] * npp), *([ck] * npp), *([cv] * npp),
      qi_rows, wq, w_rows, r3(kbf), r3(vbf), r3(smallbf), bias_s)


def _rms(x, g):
    return x * lax.rsqrt(jnp.mean(x * x, axis=-1, keepdims=True) + EPS) * g


def _mlp_body(x_ref, ma_ref, ob_ref, woa_ref, wob_ref, gffn_ref, wg_ref, wu_ref, wd_ref, gfin_ref, y_ref):
    h = x_ref[...] + (_mm(ma_ref[...], woa_ref[...]) + _mm(ob_ref[...], wob_ref[...]))
    hn = _rms(h, gffn_ref[...]).astype(BF16)
    act = _silu(_mm(hn, wg_ref[...])) * _mm(hn, wu_ref[...])
    y_ref[...] = _rms(h + _mm(act.astype(BF16), wd_ref[...]), gfin_ref[...])


def _mlp(x2d, mix_a, o_b, w_out, g_ffn, w_gate, w_up, w_down, g_final, tm):
    n = x2d.shape[0]
    assert n % tm == 0
    row = lambda w: pl.BlockSpec((tm, w), lambda i: (i, 0))
    woa, wob = w_out[:A_DIM].astype(BF16), w_out[A_DIM:].astype(BF16)
    return pl.pallas_call(
        _mlp_body,
        grid=(n // tm,),
        in_specs=[row(D_MODEL), row(A_DIM), row(B_DIM), _const_spec(woa.shape), _const_spec(wob.shape),
                  _const_spec((1, D_MODEL)), _const_spec(w_gate.shape), _const_spec(w_up.shape),
                  _const_spec(w_down.shape), _const_spec((1, D_MODEL))],
        out_specs=row(D_MODEL),
        out_shape=jax.ShapeDtypeStruct((n, D_MODEL), F32),
        compiler_params=pltpu.CompilerParams(dimension_semantics=("parallel",), vmem_limit_bytes=VMEM_LIMIT),
        name="mlp",
    )(x2d, mix_a, o_b, woa, wob, g_ffn.reshape(1, D_MODEL), w_gate.astype(BF16), w_up.astype(BF16),
      w_down.astype(BF16), g_final.reshape(1, D_MODEL))


def kernel(x_prompt, x_sample, cache_k, cache_v, cache_kidx, state_conv, state_ssm, page_table, meta_tokens, norm_mix, w_in, conv_w, a_log, dt_bias, o_norm, w_out, rel_bias, norm_ffn, w_gate, w_up, w_down, norm_final):
    assert w_in.shape[0] == 1, "single-layer stack"
    bsz, seq, _ = x_prompt.shape
    dbs, n_new, _ = x_sample.shape
    t_pad = FRONT + seq
    lo = FRONT - N_META
    past_len = page_table.shape[1] * PAGE
    topk_p = min(TOPK_MAX, seq // 4)
    topk_s = min(TOPK_MAX, (past_len + n_new) // 4)
    w_parts = _split_w_in(w_in[0])
    mlp_w = (w_out[0], norm_ffn[0], w_gate[0], w_up[0], w_down[0], norm_final)
    gdn_w = (conv_w[0], a_log[0], dt_bias[0], o_norm[0])

    meta = jnp.broadcast_to(meta_tokens.astype(F32)[None], (bsz, N_META, D_MODEL))
    xp = jnp.concatenate([jnp.zeros((bsz, lo, D_MODEL), F32), meta, x_prompt], axis=1).reshape(bsz * t_pad, D_MODEL)
    qkva, z, qb, k, v, kbf, vbf, qi, small, smallbf, small_t = _proj(xp, norm_mix[0], w_parts, 512)
    mix_a, ssm_p = _gdn_prompt(qkva, z, small, small_t, *gdn_w, bsz, t_pad, lo)
    bias_tab = _bias_tab(rel_bias, t_pad // LANES)
    o_b = _dsa_prompt(qb, qi, small, kbf, vbf, smallbf, bias_tab, bsz, t_pad, lo, topk_p)
    y_p = _mlp(x_prompt.reshape(bsz * seq, D_MODEL), mix_a.reshape(bsz * seq, A_DIM), o_b.reshape(bsz * seq, B_DIM),
               *mlp_w, 512).reshape(bsz, seq, D_MODEL)
    real = lambda a: a.reshape(bsz, t_pad, a.shape[-1])[:, lo:]
    k_p = real(k).reshape(1, bsz, N_META + seq, HEADS, HD)
    v_p = real(v).reshape(1, bsz, N_META + seq, HEADS, HD)
    kidx_p = real(small)[:, :, S_KIDX:S_KIDX + HD][None]
    conv_p = real(qkva)[:, -(CONV_WIDTH - 1):][None]

    xs = x_sample.reshape(dbs * n_new, D_MODEL)
    qkva_s, z_s, qb_s, k_s, v_s, kbf_s, vbf_s, qi_s, small_s, smallbf_s, small_t_s = _proj(xs, norm_mix[0], w_parts, 512)
    mix_a_s, ssm_s = _gdn_sample(qkva_s, z_s, small_s, small_t_s, state_conv[0], state_ssm[0], *gdn_w, n_new)
    bias_s = _bias_sample(rel_bias, past_len, past_len // PAGE + 1)
    o_b_s = _dsa_sample(qb_s, qi_s, small_s, kbf_s, vbf_s, smallbf_s, cache_k[0], cache_v[0], cache_kidx[0], page_table,
                        bias_s, n_new, topk_s)
    y_s = _mlp(xs, mix_a_s, o_b_s.reshape(dbs * n_new, B_DIM), *mlp_w, 512).reshape(dbs, n_new, D_MODEL)
    k_sn = k_s.reshape(1, dbs, n_new, HEADS, HD)
    v_sn = v_s.reshape(1, dbs, n_new, HEADS, HD)
    kidx_s = small_s.reshape(dbs, n_new, LANES)[:, :, S_KIDX:S_KIDX + HD][None]
    conv_s = jnp.concatenate([state_conv[0].astype(F32), qkva_s.reshape(dbs, n_new, 3 * A_DIM)],
                             axis=1)[:, -(CONV_WIDTH - 1):][None]
    return (y_p, y_s, k_p, v_p, kidx_p, conv_p, ssm_p[None], k_sn, v_sn, kidx_s, conv_s, ssm_s[None])
```

```python
import functools
import math

import jax
import jax.numpy as jnp
from jax import lax
from jax.experimental import pallas as pl
from jax.experimental.pallas import tpu as pltpu

F32 = jnp.float32
BF16 = jnp.bfloat16
I32 = jnp.int32
HI = lax.Precision.HIGHEST

D_MODEL = 1024
N_META = 16
HEADS = 8
HD = 64
A_DIM = HEADS * HD
B_DIM = HEADS * HD
CONV_WIDTH = 4
GDN_CHUNK = 64
TOPK_MAX = 256
PAGE = 128
REL_BUCKETS = 32
REL_MAX_DIST = 1024
D_FF = 2816
EPS = 1e-6
LANES = 128
SUBLANES = 8
FRONT = 128
NEG = -1e30
INT_MIN = -2 ** 31
VMEM_LIMIT = 56 * 1024 * 1024

S_KIDX, S_BETA, S_A, S_W = 0, 64, 72, 80


def _mm(a, b):
    return jnp.dot(a, b, preferred_element_type=F32)


def _mm_hi(a, b):
    return jnp.dot(a, b, preferred_element_type=F32, precision=HI)


def _split_bf16(x):
    hi = x.astype(BF16)
    return hi, (x - hi.astype(F32)).astype(BF16)


def _mm_solve(a, b):
    a_hi, a_lo = _split_bf16(a)
    b_hi, b_lo = _split_bf16(b)
    return _mm(a_hi, b_hi) + (_mm(a_hi, b_lo) + _mm(a_lo, b_hi))


def _nt(a, b, precision=None):
    return lax.dot_general(a, b, (((1,), (1,)), ((), ())), preferred_element_type=F32, precision=precision)


def _tn(a, b, precision=None):
    return lax.dot_general(a, b, (((0,), (0,)), ((), ())), preferred_element_type=F32, precision=precision)


def _const_spec(shape):
    nd = len(shape)
    return pl.BlockSpec(shape, lambda *_: (0,) * nd, pipeline_mode=pl.Buffered(1))


def _silu(x):
    return x * jax.nn.sigmoid(x)


def _proj_body(x_ref, g_ref, wa_ref, wz_ref, wb_ref, wqi_ref, ws_ref, wst_ref,
               qkva_ref, z_ref, qb_ref, k_ref, v_ref, kbf_ref, vbf_ref, qi_ref, small_ref, smallbf_ref, smallt_ref,
               smalltbf_ref, kt_ref):
    x = x_ref[...]
    ms = jnp.mean(x * x, axis=-1, keepdims=True)
    h = (x * lax.rsqrt(ms + EPS) * g_ref[...]).astype(BF16)
    qkva_ref[...] = _mm(h, wa_ref[...])
    z_ref[...] = _mm(h, wz_ref[...])
    qkvb = _mm(h, wb_ref[...])
    qb_ref[...] = (qkvb[:, :B_DIM] * 0.125).astype(BF16)
    k = qkvb[:, B_DIM:2 * B_DIM]
    v = qkvb[:, 2 * B_DIM:]
    k_ref[...] = k
    v_ref[...] = v
    kbf_ref[...] = k.astype(BF16)
    vbf_ref[...] = v.astype(BF16)
    qi_ref[...] = (_mm(h, wqi_ref[...]) * 0.125).astype(BF16)
    s = _mm(h, ws_ref[...])
    small_ref[...] = s
    smallbf_ref[...] = s.astype(BF16)
    t = _nt(wst_ref[...], h)
    smallt_ref[...] = t[:LANES]
    smalltbf_ref[...] = t[:LANES].astype(BF16)
    kt_ref[...] = t[LANES:].astype(BF16)


def _split_w_in(w_in):
    c = 0
    parts = []
    for n in (3 * A_DIM, A_DIM, HEADS, HEADS, 3 * B_DIM, HEADS * HD, HD, HEADS):
        parts.append(w_in[:, c:c + n])
        c += n
    w_qkva, w_z, w_beta, w_a, w_qkvb, w_qi, w_kidx, w_w = parts
    pad = jnp.zeros((w_in.shape[0], LANES - (HD + 3 * HEADS)), w_in.dtype)
    w_small = jnp.concatenate([w_kidx, w_beta, w_a, w_w, pad], axis=1)
    bf = lambda t: t.astype(BF16)
    w_t = jnp.concatenate([w_small, w_qkvb[:, B_DIM:2 * B_DIM]], axis=1).T
    return bf(w_qkva), bf(w_z), bf(w_qkvb), bf(w_qi), bf(w_small), bf(w_t)


def _proj(x2d, g, w_parts, tm):
    n = x2d.shape[0]
    assert n % tm == 0 and tm % LANES == 0
    wa, wz, wb, wqi, ws, wst = w_parts
    row = lambda w: pl.BlockSpec((tm, w), lambda i: (i, 0))
    outs = [
        (jax.ShapeDtypeStruct((n, 3 * A_DIM), F32), row(3 * A_DIM)),
        (jax.ShapeDtypeStruct((n, A_DIM), F32), row(A_DIM)),
        (jax.ShapeDtypeStruct((n, B_DIM), BF16), row(B_DIM)),
        (jax.ShapeDtypeStruct((n, B_DIM), F32), row(B_DIM)),
        (jax.ShapeDtypeStruct((n, B_DIM), F32), row(B_DIM)),
        (jax.ShapeDtypeStruct((n, B_DIM), BF16), row(B_DIM)),
        (jax.ShapeDtypeStruct((n, B_DIM), BF16), row(B_DIM)),
        (jax.ShapeDtypeStruct((n, HEADS * HD), BF16), row(HEADS * HD)),
        (jax.ShapeDtypeStruct((n, LANES), F32), row(LANES)),
        (jax.ShapeDtypeStruct((n, LANES), BF16), row(LANES)),
        (jax.ShapeDtypeStruct((LANES, n), F32), pl.BlockSpec((LANES, tm), lambda i: (0, i))),
        (jax.ShapeDtypeStruct((LANES, n), BF16), pl.BlockSpec((LANES, tm), lambda i: (0, i))),
        (jax.ShapeDtypeStruct((B_DIM, n), BF16), pl.BlockSpec((B_DIM, tm), lambda i: (0, i))),
    ]
    return pl.pallas_call(
        _proj_body,
        grid=(n // tm,),
        in_specs=[row(D_MODEL), _const_spec((1, D_MODEL)), _const_spec(wa.shape), _const_spec(wz.shape),
                  _const_spec(wb.shape), _const_spec(wqi.shape), _const_spec(ws.shape), _const_spec(wst.shape)],
        out_specs=[o[1] for o in outs],
        out_shape=[o[0] for o in outs],
        compiler_params=pltpu.CompilerParams(dimension_semantics=("parallel",), vmem_limit_bytes=VMEM_LIMIT),
        name="proj",
    )(x2d, g.reshape(1, D_MODEL), wa, wz, wb, wqi, ws, wst)


def _gdn_prep(qkv, tail, conv_w, small, small_t, a_log_row, dt_row, a_log_col, dt_col, valid_col, valid_row, chunk):
    rows = qkv.shape[0]
    n_chunks = rows // chunk
    ys = []
    for c in range(n_chunks):
        ext = jnp.concatenate([tail[c], qkv[c * chunk:(c + 1) * chunk]], axis=0)
        y = sum(ext[SUBLANES - (CONV_WIDTH - 1) + j: SUBLANES - (CONV_WIDTH - 1) + j + chunk] * conv_w[j:j + 1]
                for j in range(CONV_WIDTH))
        ys.append(_silu(y))
    y = ys[0] if n_chunks == 1 else jnp.concatenate(ys, axis=0)
    beta_c = jnp.where(valid_col, jax.nn.sigmoid(small), 0.0)
    beta_r = jnp.where(valid_row, jax.nn.sigmoid(small_t), 0.0)
    g_c = jnp.where(valid_col, -jnp.exp(a_log_row) * jax.nn.softplus(small + dt_row), 0.0)
    g_r = jnp.where(valid_row, -jnp.exp(a_log_col) * jax.nn.softplus(small_t + dt_col), 0.0)
    ri = lax.broadcasted_iota(I32, (rows, rows), 0)
    ci = lax.broadcasted_iota(I32, (rows, rows), 1)
    same = (ri // chunk) == (ci // chunk)
    low = jnp.where(same & (ci <= ri), 1.0, 0.0).astype(F32)
    b_c = _mm_hi(low, g_c)
    b_r = _nt(g_r, low, precision=HI)
    return y, beta_c, b_c, b_r


def _gdn_head(qh, kh, vh, beta_c, b_c, b_r, s0, o_norm, zh, chunk):
    c = chunk
    qn = qh * lax.rsqrt(jnp.sum(qh * qh, axis=-1, keepdims=True) + EPS) * (HD ** -0.5)
    kn = kh * lax.rsqrt(jnp.sum(kh * kh, axis=-1, keepdims=True) + EPS)
    ri = lax.broadcasted_iota(I32, (c, c), 0)
    ci = lax.broadcasted_iota(I32, (c, c), 1)
    incl = ci <= ri
    strict = ci < ri
    decay = jnp.where(incl, jnp.exp(jnp.where(incl, b_c - b_r, 0.0)), 0.0)
    kb, qb16, s0b = kn.astype(BF16), qn.astype(BF16), s0.astype(BF16)
    kk = _nt(kb, kb)
    qk = _nt(qb16, kb)
    a_mat = jnp.where(strict, beta_c * kk * decay, 0.0)
    eb = jnp.exp(b_c)
    y = jnp.concatenate([beta_c * vh, (beta_c * eb) * kn], axis=1)
    bk = -a_mat
    n_levels = int(math.log2(c))
    for lvl in range(n_levels):
        if lvl + 1 < n_levels:
            prod = _mm_solve(bk, jnp.concatenate([y, bk], axis=1))
            y, bk = y + prod[:, :2 * HD], prod[:, 2 * HD:]
        else:
            y = y + _mm_solve(bk, y)
    u, w = y[:, :HD], y[:, HD:]
    delta = u - _mm(w.astype(BF16), s0b)
    deltab = delta.astype(BF16)
    o = eb * _mm(qb16, s0b) + _mm((qk * decay).astype(BF16), deltab)
    b_last = b_c[c - 1:c, :]
    s_new = jnp.exp(b_last) * s0 + _tn((kn * jnp.exp(b_last - b_c)).astype(BF16), deltab)
    on = o * lax.rsqrt(jnp.mean(o * o, axis=-1, keepdims=True) + EPS) * o_norm
    return s_new, on * _silu(zh)


def _bdot(a, b, ca, cb):
    return lax.dot_general(a, b, (((ca,), (cb,)), ((0,), (0,))), preferred_element_type=F32)


def _bmm_solve(a, b):
    a_hi, a_lo = _split_bf16(a)
    b_hi, b_lo = _split_bf16(b)
    return _bdot(a_hi, b_hi, 2, 1) + (_bdot(a_hi, b_lo, 2, 1) + _bdot(a_lo, b_hi, 2, 1))


def _gdn_solve(q, k, v, beta, b_c, b_r, chunk):
    c = chunk
    qn = q * lax.rsqrt(jnp.sum(q * q, axis=-1, keepdims=True) + EPS) * (HD ** -0.5)
    kn = k * lax.rsqrt(jnp.sum(k * k, axis=-1, keepdims=True) + EPS)
    ri = lax.broadcasted_iota(I32, (1, c, c), 1)
    ci = lax.broadcasted_iota(I32, (1, c, c), 2)
    incl = ci <= ri
    decay = jnp.where(incl, jnp.exp(jnp.where(incl, b_c - b_r, 0.0)), 0.0)
    kb, qb16 = kn.astype(BF16), qn.astype(BF16)
    kk = _bdot(kb, kb, 2, 2)
    qk = _bdot(qb16, kb, 2, 2)
    bk = -jnp.where(ci < ri, beta * kk * decay, 0.0)
    eb = jnp.exp(b_c)
    y = jnp.concatenate([beta * v, (beta * eb) * kn], axis=2)
    n_levels = int(math.log2(c))
    for lvl in range(n_levels):
        if lvl + 1 < n_levels:
            prod = _bmm_solve(bk, jnp.concatenate([y, bk], axis=2))
            y, bk = y + prod[:, :, :2 * HD], prod[:, :, 2 * HD:]
        else:
            y = y + _bmm_solve(bk, y)
    b_last = b_c[:, c - 1:c, :]
    return (y[:, :, :HD], y[:, :, HD:].astype(BF16), (qk * decay).astype(BF16), qb16,
            (kn * jnp.exp(b_last - b_c)).astype(BF16), eb, jnp.exp(b_last))


def _gdn_apply(u, w, attn, qb16, kd, eb, eb_last, s0, o_norm, z):
    s0b = s0.astype(BF16)
    delta = u - _bdot(w, s0b, 2, 1)
    deltab = delta.astype(BF16)
    o = eb * _bdot(qb16, s0b, 2, 1) + _bdot(attn, deltab, 2, 1)
    s_new = eb_last * s0 + _bdot(kd, deltab, 1, 1)
    on = o * lax.rsqrt(jnp.mean(o * o, axis=-1, keepdims=True) + EPS) * o_norm
    return s_new, on * _silu(z)


def _gdn_chunk_all_heads(y, beta_c, b_c, b_r, z, o_norm, r0, chunk, get_s0, put_s, put_o):
    for h in range(HEADS):
        sl = slice(r0, r0 + chunk)
        qh = y[sl, h * HD:(h + 1) * HD]
        kh = y[sl, A_DIM + h * HD:A_DIM + (h + 1) * HD]
        vh = y[sl, 2 * A_DIM + h * HD:2 * A_DIM + (h + 1) * HD]
        bc = beta_c[sl, S_BETA + h:S_BETA + h + 1]
        bcol = b_c[sl, S_A + h:S_A + h + 1]
        brow = b_r[S_A + h:S_A + h + 1, r0:r0 + chunk]
        s_new, out = _gdn_head(qh, kh, vh, bc, bcol, brow, get_s0(h), o_norm, z[sl, h * HD:(h + 1) * HD], chunk)
        put_s(h, s_new)
        put_o(h, out)


def _gdn_prompt_body(qkv_ref, z_ref, small_ref, smallt_ref, convw_ref, alr_ref, dtr_ref, alc_ref, dtc_ref, onorm_ref,
                     mix_ref, s_ref, tail_scr, *, rows, lo):
    t = pl.program_id(1)

    @pl.when(t == 0)
    def _():
        s_ref[...] = jnp.zeros_like(s_ref)
        tail_scr[...] = jnp.zeros_like(tail_scr)

    chunk = GDN_CHUNK
    qkv = qkv_ref[0]
    tails = [tail_scr[...]] + [qkv[c * chunk - SUBLANES:c * chunk] for c in range(1, rows // chunk)]
    pos_c = t * rows + lax.broadcasted_iota(I32, (rows, LANES), 0)
    pos_r = t * rows + lax.broadcasted_iota(I32, (LANES, rows), 1)
    y, beta_c, b_c, b_r = _gdn_prep(qkv, tails, convw_ref[...], small_ref[0], smallt_ref[...],
                                    alr_ref[...], dtr_ref[...], alc_ref[...], dtc_ref[...],
                                    pos_c >= lo, pos_r >= lo, chunk)
    tail_scr[...] = qkv[rows - SUBLANES:rows]
    z = z_ref[0]
    n_chunks = rows // chunk
    units = [(c, h) for c in range(n_chunks) for h in range(HEADS)]
    rs = lambda c: slice(c * chunk, (c + 1) * chunk)
    stack = lambda a, off: jnp.stack([a[rs(c), off + h * HD:off + (h + 1) * HD] for c, h in units])
    col = lambda a, off: jnp.stack([a[rs(c), off + h:off + h + 1] for c, h in units])
    sol = _gdn_solve(stack(y, 0), stack(y, A_DIM), stack(y, 2 * A_DIM), col(beta_c, S_BETA), col(b_c, S_A),
                     jnp.stack([b_r[S_A + h:S_A + h + 1, rs(c)] for c, h in units]), chunk)
    z3 = stack(z, 0)
    o_norm = onorm_ref[...]
    for c in range(n_chunks):
        g = slice(c * HEADS, (c + 1) * HEADS)
        s_new, out = _gdn_apply(*[t[g] for t in sol], s_ref[0], o_norm, z3[g])
        s_ref[0] = s_new
        for h in range(HEADS):
            mix_ref[0, rs(c), h * HD:(h + 1) * HD] = out[h].astype(mix_ref.dtype)


def _gate_params(a_log, dt_bias):
    row = lambda v: jnp.zeros((1, LANES), F32).at[0, S_A:S_A + HEADS].set(v.astype(F32))
    return row(a_log), row(dt_bias), row(a_log).T, row(dt_bias).T


def _gdn_prompt(qkva, z, small, small_t, conv_w, a_log, dt_bias, o_norm, bsz, t_pad, lo):
    rows = LANES
    nt = t_pad // rows
    alr, dtr, alc, dtc = _gate_params(a_log, dt_bias)
    body = functools.partial(_gdn_prompt_body, rows=rows, lo=lo)
    mix, s = pl.pallas_call(
        body,
        grid=(bsz, nt),
        in_specs=[
            pl.BlockSpec((1, rows, 3 * A_DIM), lambda b, t: (b, t, 0)),
            pl.BlockSpec((1, rows, A_DIM), lambda b, t: (b, t, 0)),
            pl.BlockSpec((1, rows, LANES), lambda b, t: (b, t, 0)),
            pl.BlockSpec((LANES, rows), lambda b, t: (0, b * nt + t)),
            _const_spec((CONV_WIDTH, 3 * A_DIM)), _const_spec((1, LANES)), _const_spec((1, LANES)),
            _const_spec((LANES, 1)), _const_spec((LANES, 1)), _const_spec((1, HD)),
        ],
        out_specs=[
            pl.BlockSpec((1, rows, A_DIM), lambda b, t: (b, jnp.maximum(t - 1, 0), 0)),
            pl.BlockSpec((1, HEADS, HD, HD), lambda b, t: (b, 0, 0, 0)),
        ],
        out_shape=[jax.ShapeDtypeStruct((bsz, t_pad - FRONT, A_DIM), BF16),
                   jax.ShapeDtypeStruct((bsz, HEADS, HD, HD), F32)],
        scratch_shapes=[pltpu.VMEM((SUBLANES, 3 * A_DIM), F32)],
        compiler_params=pltpu.CompilerParams(dimension_semantics=("parallel", "arbitrary"),
                                             vmem_limit_bytes=VMEM_LIMIT),
        name="gdn_prompt",
    )(qkva.reshape(bsz, t_pad, 3 * A_DIM), z.reshape(bsz, t_pad, A_DIM), small.reshape(bsz, t_pad, LANES), small_t,
      conv_w, alr, dtr, alc, dtc, o_norm.reshape(1, HD))
    return mix, s


def _gdn_sample_body(qkv_ref, z_ref, small_ref, smallt_ref, conv0_ref, s0_ref, convw_ref, alr_ref, dtr_ref, alc_ref,
                     dtc_ref, onorm_ref, mix_ref, s_ref, *, chunk):
    rows = LANES
    n_seq = rows // chunk
    qkv = qkv_ref[...]
    tails = [conv0_ref[i] for i in range(n_seq)]
    y, beta_c, b_c, b_r = _gdn_prep(qkv, tails, convw_ref[...], small_ref[...], smallt_ref[...],
                                    alr_ref[...], dtr_ref[...], alc_ref[...], dtc_ref[...], True, True, chunk)
    z = z_ref[...]
    o_norm = onorm_ref[...]
    for i in range(n_seq):
        def put_s(h, s, i=i):
            s_ref[i, h] = s

        def put_o(h, o, i=i):
            mix_ref[i * chunk:(i + 1) * chunk, h * HD:(h + 1) * HD] = o.astype(mix_ref.dtype)

        _gdn_chunk_all_heads(y, beta_c, b_c, b_r, z, o_norm, i * chunk, chunk, lambda h, i=i: s0_ref[i, h], put_s, put_o)


def _gdn_sample(qkva, z, small, small_t, state_conv, state_ssm, conv_w, a_log, dt_bias, o_norm, n_new):
    n = qkva.shape[0]
    bsz = n // n_new
    rows = LANES
    n_seq = rows // n_new
    assert n_new == SUBLANES and n % rows == 0
    alr, dtr, alc, dtc = _gate_params(a_log, dt_bias)
    conv0 = jnp.pad(state_conv.astype(F32), ((0, 0), (SUBLANES - (CONV_WIDTH - 1), 0), (0, 0)))
    body = functools.partial(_gdn_sample_body, chunk=n_new)
    mix, s = pl.pallas_call(
        body,
        grid=(n // rows,),
        in_specs=[
            pl.BlockSpec((rows, 3 * A_DIM), lambda g: (g, 0)),
            pl.BlockSpec((rows, A_DIM), lambda g: (g, 0)),
            pl.BlockSpec((rows, LANES), lambda g: (g, 0)),
            pl.BlockSpec((LANES, rows), lambda g: (0, g)),
            pl.BlockSpec((n_seq, SUBLANES, 3 * A_DIM), lambda g: (g, 0, 0)),
            pl.BlockSpec((n_seq, HEADS, HD, HD), lambda g: (g, 0, 0, 0)),
            _const_spec((CONV_WIDTH, 3 * A_DIM)), _const_spec((1, LANES)), _const_spec((1, LANES)),
            _const_spec((LANES, 1)), _const_spec((LANES, 1)), _const_spec((1, HD)),
        ],
        out_specs=[
            pl.BlockSpec((rows, A_DIM), lambda g: (g, 0)),
            pl.BlockSpec((n_seq, HEADS, HD, HD), lambda g: (g, 0, 0, 0)),
        ],
        out_shape=[jax.ShapeDtypeStruct((n, A_DIM), BF16), jax.ShapeDtypeStruct((bsz, HEADS, HD, HD), F32)],
        compiler_params=pltpu.CompilerParams(dimension_semantics=("parallel",), vmem_limit_bytes=VMEM_LIMIT),
        name="gdn_sample",
    )(qkva, z, small, small_t, conv0, state_ssm.astype(F32), conv_w, alr, dtr, alc, dtc, o_norm.reshape(1, HD))
    return mix, s


def _rel_bucket(d):
    d = jnp.maximum(d, 0)
    max_exact = REL_BUCKETS // 2
    df = jnp.maximum(d, 1).astype(F32)
    large = max_exact + (jnp.log(df / max_exact) / math.log(REL_MAX_DIST / max_exact)
                         * (REL_BUCKETS - max_exact)).astype(I32)
    large = jnp.minimum(large, REL_BUCKETS - 1)
    return jnp.where(d < max_exact, d, large)


def _bias_lookup(bucket, rb_ref, h):
    acc = jnp.zeros(bucket.shape, F32)
    for b in range(REL_BUCKETS):
        acc = jnp.where(bucket == b, rb_ref[b, h], acc)
    return acc


def _bias_tab_body(rb_ref, out_ref):
    dlt = pl.program_id(0)
    i = lax.broadcasted_iota(I32, (LANES, LANES), 0)
    j = lax.broadcasted_iota(I32, (LANES, LANES), 1)
    bucket = _rel_bucket(dlt * LANES + i - j)
    for h in range(HEADS):
        out_ref[h, 0] = _bias_lookup(bucket, rb_ref, h)


def _bias_tab(rel_bias, n_tiles):
    return pl.pallas_call(
        _bias_tab_body,
        grid=(n_tiles,),
        in_specs=[pl.BlockSpec(memory_space=pltpu.SMEM)],
        out_specs=pl.BlockSpec((HEADS, 1, LANES, LANES), lambda t: (0, t, 0, 0)),
        out_shape=jax.ShapeDtypeStruct((HEADS, n_tiles, LANES, LANES), F32),
        name="bias_tab",
    )(rel_bias.astype(F32))


def _bias_sample_body(rb_ref, out_ref, *, past_len):
    blk = pl.program_id(0)
    q = lax.broadcasted_iota(I32, (SUBLANES, LANES), 0)
    s = blk * LANES + lax.broadcasted_iota(I32, (SUBLANES, LANES), 1)
    bucket = _rel_bucket(past_len + q - s)
    for h in range(HEADS):
        out_ref[h * SUBLANES:(h + 1) * SUBLANES, :] = _bias_lookup(bucket, rb_ref, h)


def _bias_sample(rel_bias, past_len, n_blocks):
    return pl.pallas_call(
        functools.partial(_bias_sample_body, past_len=past_len),
        grid=(n_blocks,),
        in_specs=[pl.BlockSpec(memory_space=pltpu.SMEM)],
        out_specs=pl.BlockSpec((HEADS * SUBLANES, LANES), lambda t: (0, t)),
        out_shape=jax.ShapeDtypeStruct((HEADS * SUBLANES, n_blocks * LANES), F32),
        name="bias_sample",
    )(rel_bias.astype(F32))


def _sort_key(score, admissible):
    score = jnp.where(score == 0.0, 0.0, score)
    bits = pltpu.bitcast(score, I32)
    key = jnp.where(bits < 0, bits ^ 0x7FFFFFFF, bits)
    return jnp.where(admissible, key, INT_MIN)


def _kth_largest_key(count_ge, rows, topk):
    c0 = jnp.broadcast_to(count_ge(jnp.zeros((rows, LANES), I32)), (rows, LANES))
    t0 = jnp.where(c0 >= topk, 0, INT_MIN).astype(I32)
    c0 = jnp.where(c0 >= topk, c0, float(2 ** 30))

    def bit_step(i, carry):
        t, c = carry
        cand = t + lax.shift_left(jnp.int32(1), 30 - i)
        cc = jnp.broadcast_to(count_ge(cand), (rows, LANES))
        ok = cc >= topk
        return jnp.where(ok, cand, t), jnp.where(ok, cc, c)

    return lax.fori_loop(0, 31, bit_step, (t0, c0))


def _tie_mask(key, thr, need, running, tri):
    eq = (key == thr) & (key != INT_MIN)
    pref = _mm(jnp.where(eq, 1.0, 0.0).astype(BF16), tri) + running
    sel = (key > thr) | (eq & (pref <= need))
    return sel, jnp.broadcast_to(pref[:, LANES - 1:LANES], pref.shape)


def _tri_incl():
    r = lax.broadcasted_iota(I32, (LANES, LANES), 0)
    c = lax.broadcasted_iota(I32, (LANES, LANES), 1)
    return jnp.where(r <= c, 1.0, 0.0).astype(BF16)


DSA_BUCKETS = (5, 9, 13, 17)


def _dsa_prompt_tile(j, nblk, qi_ref, small_ref, kt_ref, kidxt_ref, bias_ref, out_ref,
                     qh_scr, vh_scr, keys_scr, mbias_scr, o_scr, *, lo, topk):
    tq = LANES
    wid = nblk * LANES
    t_pos = j * tq + lax.broadcasted_iota(I32, (tq, wid), 0)
    s_pos = lax.broadcasted_iota(I32, (tq, wid), 1)
    w = small_ref[0][:, S_W:S_W + HEADS] * (HEADS ** -0.5)
    qi = qi_ref[0]
    kxt = kidxt_ref[S_KIDX:S_KIDX + HD, 0:wid]
    acc = jnp.zeros((tq, wid), F32)
    for h in range(HEADS):
        acc = acc + jnp.maximum(_mm(qi[:, h * HD:(h + 1) * HD], kxt), 0.0) * w[:, h:h + 1]
    keys_scr[:, 0:wid] = _sort_key(acc, (s_pos <= t_pos) & (s_pos >= lo))

    def count_ge(cand):
        return jnp.sum(jnp.where(keys_scr[:, 0:wid] >= cand[:, 0:1], 1.0, 0.0), axis=1, keepdims=True)

    thr, cnt = _kth_largest_key(count_ge, tq, topk)
    keys = keys_scr[:, 0:wid]
    mbias_scr[:, 0:wid] = jnp.where(keys >= jnp.maximum(thr[:, 0:1], INT_MIN + 1), 0.0, NEG)

    @pl.when(jnp.max(jnp.where(cnt < float(2 ** 30), cnt, 0.0)) > topk)
    def _():
        need = topk - jnp.sum(jnp.where(keys_scr[:, 0:wid] > thr[:, 0:1], 1.0, 0.0), axis=1, keepdims=True)
        tri = _tri_incl()
        running = jnp.zeros((tq, LANES), F32)
        for jb in range(nblk):
            cols = slice(jb * LANES, (jb + 1) * LANES)
            sel, running = _tie_mask(keys_scr[:, cols], thr, need, running, tri)
            mbias_scr[:, cols] = jnp.where(sel, 0.0, NEG)

    def head_attn(h, carry):
        bias = jnp.concatenate([bias_ref[h, jnp.maximum(j - jb, 0)] for jb in range(nblk)], axis=1)
        kt = kt_ref[pl.ds(pl.multiple_of(h * HD, HD), HD), 0:wid]
        lg = _mm(qh_scr[h], kt) + bias + mbias_scr[:, 0:wid]
        p = jnp.exp(lg - jnp.max(lg, axis=1, keepdims=True))
        inv = 1.0 / jnp.sum(p, axis=1, keepdims=True)
        o_scr[h] = _mm(p.astype(BF16), vh_scr[h, 0:wid, :]) * inv
        return carry

    lax.fori_loop(0, HEADS, head_attn, 0)
    for h in range(HEADS):
        out_ref[0, :, h * HD:(h + 1) * HD] = o_scr[h].astype(out_ref.dtype)


def _dsa_prompt_body(qb_ref, qi_ref, small_ref, kt_ref, vbf_ref, kidxt_ref, bias_ref, out_ref,
                     qh_scr, vh_scr, keys_scr, mbias_scr, o_scr, *, lo, topk, buckets):
    j = pl.program_id(1)

    @pl.when(j == 1)
    def _():
        for h in range(HEADS):
            vh_scr[h] = vbf_ref[0, :, h * HD:(h + 1) * HD]

    @pl.when(j > 0)
    def _():
        for h in range(HEADS):
            qh_scr[h] = qb_ref[0, :, h * HD:(h + 1) * HD]

    lower = 0
    for nblk in buckets:
        @pl.when((j > lower) & (j + 1 <= nblk))
        def _(nblk=nblk):
            _dsa_prompt_tile(j, nblk, qi_ref, small_ref, kt_ref, kidxt_ref, bias_ref, out_ref,
                             qh_scr, vh_scr, keys_scr, mbias_scr, o_scr, lo=lo, topk=topk)
        lower = nblk - 1


def _dsa_prompt(qb, qi, small, kt, vbf, smalltbf, bias_tab, bsz, t_pad, lo, topk):
    tq = LANES
    nq = t_pad // tq
    buckets = tuple(b for b in DSA_BUCKETS if b < nq) + (nq,)
    r3 = lambda a: a.reshape(bsz, t_pad, a.shape[-1])
    tile = lambda w: pl.BlockSpec((1, tq, w), lambda b, j: (b, j, 0))
    seq_cols = lambda r: pl.BlockSpec((r, t_pad), lambda b, j: (0, b))
    return pl.pallas_call(
        functools.partial(_dsa_prompt_body, lo=lo, topk=topk, buckets=buckets),
        grid=(bsz, nq),
        in_specs=[tile(B_DIM), tile(HEADS * HD), tile(LANES), seq_cols(B_DIM),
                  pl.BlockSpec((1, t_pad, B_DIM), lambda b, j: (b, 0, 0)), seq_cols(LANES),
                  _const_spec(bias_tab.shape)],
        out_specs=pl.BlockSpec((1, tq, B_DIM), lambda b, j: (b, jnp.maximum(j - 1, 0), 0)),
        out_shape=jax.ShapeDtypeStruct((bsz, t_pad - FRONT, B_DIM), BF16),
        scratch_shapes=[pltpu.VMEM((HEADS, tq, HD), BF16), pltpu.VMEM((HEADS, t_pad, HD), BF16),
                        pltpu.VMEM((tq, t_pad), I32), pltpu.VMEM((tq, t_pad), F32),
                        pltpu.VMEM((HEADS, tq, HD), F32)],
        compiler_params=pltpu.CompilerParams(dimension_semantics=("parallel", "arbitrary"),
                                             vmem_limit_bytes=VMEM_LIMIT),
        name="dsa_prompt",
    )(r3(qb), r3(qi), r3(small), kt, r3(vbf), smalltbf, bias_tab)


PAGES_PER_STEP = 8


def _dsa_sample_body(pt_ref, *refs, n_new, past_len, topk):
    npp = PAGES_PER_STEP
    kidx_refs, k_refs, v_refs = refs[:npp], refs[npp:2 * npp], refs[2 * npp:3 * npp]
    (qi_ref, qbd_ref, w_ref, knew_ref, vnew_ref, kidxnew_ref, bias_ref, out_ref,
     keys_scr, logit_scr, vt_scr) = refs[3 * npp:]
    del pt_ref
    s = pl.program_id(1)
    n_page_steps = past_len // (PAGE * npp)
    n_blocks = past_len // PAGE + 1
    qi = qi_ref[0]
    qbd = qbd_ref[0]
    w = w_ref[0] * (HEADS ** -0.5)
    q_row = lax.broadcasted_iota(I32, (n_new, LANES), 0)
    lane = lax.broadcasted_iota(I32, (n_new, LANES), 1)

    def do_block(off, kxt, kt, vt, admissible):
        rel = jnp.maximum(_mm(qi, kxt), 0.0) * w
        sc = rel[0:n_new]
        for h in range(1, HEADS):
            sc = sc + rel[h * n_new:(h + 1) * n_new]
        keys_scr[:, pl.ds(off, LANES)] = _sort_key(sc, admissible)
        logit_scr[:, pl.ds(off, LANES)] = _mm(qbd, kt) + bias_ref[:, pl.ds(off, LANES)]
        vt_scr[:, pl.ds(off, LANES)] = vt

    @pl.when(s < n_page_steps)
    def _():
        for i in range(npp):
            off = pl.multiple_of((s * npp + i) * PAGE, PAGE)
            do_block(off, kidx_refs[i][0].astype(BF16), k_refs[i][0].reshape(B_DIM, PAGE).astype(BF16),
                     v_refs[i][0].reshape(B_DIM, PAGE).astype(BF16), True)

    @pl.when(s == n_page_steps)
    def _():
        tr = lambda a: jnp.concatenate(
            [a.astype(F32), jnp.zeros((PAGE - n_new, a.shape[1]), F32)], axis=0).T.astype(BF16)
        do_block(past_len, tr(kidxnew_ref[0])[S_KIDX:S_KIDX + HD], tr(knew_ref[0]), tr(vnew_ref[0]), lane <= q_row)

        keys = keys_scr[...]
        thr, _ = _kth_largest_key(
            lambda cand: jnp.sum(jnp.where(keys >= cand[:, 0:1], 1.0, 0.0), axis=1, keepdims=True), n_new, topk)
        need = topk - jnp.sum(jnp.where(keys > thr[:, 0:1], 1.0, 0.0), axis=1, keepdims=True)
        tri = _tri_incl()

        def mask_blk(jb, running):
            cols = pl.ds(pl.multiple_of(jb * LANES, LANES), LANES)
            sel, running = _tie_mask(keys_scr[:, cols], thr, need, running, tri)
            mb = jnp.where(sel, 0.0, NEG)
            logit_scr[:, cols] = logit_scr[:, cols] + jnp.concatenate([mb] * HEADS, axis=0)
            return running

        lax.fori_loop(0, n_blocks, mask_blk, jnp.zeros((n_new, LANES), F32))
        lg = logit_scr[...]
        m = jnp.max(lg, axis=1, keepdims=True)
        p = jnp.exp(lg - m)
        pb = (p * (1.0 / jnp.sum(p, axis=1, keepdims=True))).astype(BF16)
        out_ref[0] = _nt(vt_scr[...], pb)


def _dsa_sample(qb, qi, small, kbf, vbf, smallbf, cache_k, cache_v, cache_kidx, page_table, bias_s, n_new, topk):
    dbs, n_pages = page_table.shape
    past_len = n_pages * PAGE
    npp = PAGES_PER_STEP
    assert n_pages % npp == 0 and n_new == SUBLANES
    n_page_steps = n_pages // npp
    n_blocks = n_pages + 1
    n_pool = cache_k.shape[0]
    qi_rows = qi.reshape(dbs, n_new, HEADS, HD).transpose(0, 2, 1, 3).reshape(dbs, HEADS * n_new, HD)
    q4 = qb.reshape(dbs, n_new, HEADS, HD).transpose(0, 2, 1, 3)
    eye = jnp.eye(HEADS, dtype=qb.dtype)
    qbd = (q4[:, :, :, None, :] * eye[None, :, None, :, None]).reshape(dbs, HEADS * n_new, B_DIM)
    w_rows = small.reshape(dbs, n_new, LANES)[:, :, S_W:S_W + HEADS].transpose(0, 2, 1).reshape(dbs, HEADS * n_new, 1)
    w_rows = jnp.broadcast_to(w_rows, (dbs, HEADS * n_new, LANES))

    def page_spec(shape, i):
        def imap(b, s, pt):
            return (pt[b, jnp.minimum(s, n_page_steps - 1) * npp + i],) + (0,) * len(shape)
        return pl.BlockSpec((1,) + shape, imap)

    per_seq = lambda r, w: pl.BlockSpec((1, r, w), lambda b, s, pt: (b, 0, 0))
    in_specs = ([page_spec((HD, PAGE), i) for i in range(npp)] + [page_spec((HEADS, HD, PAGE), i) for i in range(npp)]
                + [page_spec((HEADS, HD, PAGE), i) for i in range(npp)]
                + [per_seq(HEADS * n_new, HD), per_seq(HEADS * n_new, B_DIM), per_seq(HEADS * n_new, LANES),
                   per_seq(n_new, B_DIM), per_seq(n_new, B_DIM), per_seq(n_new, LANES),
                   pl.BlockSpec(bias_s.shape, lambda b, s, pt: (0, 0), pipeline_mode=pl.Buffered(1))])
    grid_spec = pltpu.PrefetchScalarGridSpec(
        num_scalar_prefetch=1,
        grid=(dbs, n_page_steps + 1),
        in_specs=in_specs,
        out_specs=pl.BlockSpec((1, B_DIM, HEADS * n_new), lambda b, s, pt: (b, 0, 0)),
        scratch_shapes=[pltpu.VMEM((n_new, n_blocks * LANES), I32), pltpu.VMEM((HEADS * n_new, n_blocks * LANES), F32),
                        pltpu.VMEM((B_DIM, n_blocks * LANES), BF16)],
    )
    ck = cache_k.transpose(0, 2, 3, 1)
    cv = cache_v.transpose(0, 2, 3, 1)
    cki = cache_kidx.transpose(0, 2, 1)
    r3 = lambda a: a.reshape(dbs, n_new, a.shape[-1])
    out_t = pl.pallas_call(
        functools.partial(_dsa_sample_body, n_new=n_new, past_len=past_len, topk=topk),
        grid_spec=grid_spec,
        out_shape=jax.ShapeDtypeStruct((dbs, B_DIM, HEADS * n_new), F32),
        compiler_params=pltpu.CompilerParams(dimension_semantics=("parallel", "arbitrary"),
                                             vmem_limit_bytes=VMEM_LIMIT),
        name="dsa_sample",
    )(page_table, *([cki] * npp), *([ck] * npp), *([cv] * npp),
      qi_rows, qbd, w_rows, r3(kbf), r3(vbf), r3(smallbf), bias_s)
    o5 = out_t.reshape(dbs, HEADS, HD, HEADS, n_new)
    o = jnp.stack([o5[:, h, :, h, :] for h in range(HEADS)], axis=1)
    return o.transpose(0, 3, 1, 2).reshape(dbs, n_new, B_DIM).astype(BF16)


def _rms(x, g):
    return x * lax.rsqrt(jnp.mean(x * x, axis=-1, keepdims=True) + EPS) * g


def _mlp_body(x_ref, ma_ref, ob_ref, woa_ref, wob_ref, gffn_ref, wg_ref, wu_ref, wd_ref, gfin_ref, y_ref):
    h = x_ref[...] + (_mm(ma_ref[...], woa_ref[...]) + _mm(ob_ref[...], wob_ref[...]))
    hn = _rms(h, gffn_ref[...]).astype(BF16)
    act = _silu(_mm(hn, wg_ref[...])) * _mm(hn, wu_ref[...])
    y_ref[...] = _rms(h + _mm(act.astype(BF16), wd_ref[...]), gfin_ref[...])


def _mlp(x2d, mix_a, o_b, w_out, g_ffn, w_gate, w_up, w_down, g_final, tm):
    n = x2d.shape[0]
    assert n % tm == 0
    row = lambda w: pl.BlockSpec((tm, w), lambda i: (i, 0))
    woa, wob = w_out[:A_DIM].astype(BF16), w_out[A_DIM:].astype(BF16)
    return pl.pallas_call(
        _mlp_body,
        grid=(n // tm,),
        in_specs=[row(D_MODEL), row(A_DIM), row(B_DIM), _const_spec(woa.shape), _const_spec(wob.shape),
                  _const_spec((1, D_MODEL)), _const_spec(w_gate.shape), _const_spec(w_up.shape),
                  _const_spec(w_down.shape), _const_spec((1, D_MODEL))],
        out_specs=row(D_MODEL),
        out_shape=jax.ShapeDtypeStruct((n, D_MODEL), F32),
        compiler_params=pltpu.CompilerParams(dimension_semantics=("parallel",), vmem_limit_bytes=VMEM_LIMIT),
        name="mlp",
    )(x2d, mix_a, o_b, woa, wob, g_ffn.reshape(1, D_MODEL), w_gate.astype(BF16), w_up.astype(BF16),
      w_down.astype(BF16), g_final.reshape(1, D_MODEL))


def kernel(x_prompt, x_sample, cache_k, cache_v, cache_kidx, state_conv, state_ssm, page_table, meta_tokens, norm_mix, w_in, conv_w, a_log, dt_bias, o_norm, w_out, rel_bias, norm_ffn, w_gate, w_up, w_down, norm_final):
    assert w_in.shape[0] == 1, "single-layer stack"
    bsz, seq, _ = x_prompt.shape
    dbs, n_new, _ = x_sample.shape
    t_pad = FRONT + seq
    lo = FRONT - N_META
    past_len = page_table.shape[1] * PAGE
    topk_p = min(TOPK_MAX, seq // 4)
    topk_s = min(TOPK_MAX, (past_len + n_new) // 4)
    w_parts = _split_w_in(w_in[0])
    mlp_w = (w_out[0], norm_ffn[0], w_gate[0], w_up[0], w_down[0], norm_final)
    gdn_w = (conv_w[0], a_log[0], dt_bias[0], o_norm[0])

    meta = jnp.broadcast_to(meta_tokens.astype(F32)[None], (bsz, N_META, D_MODEL))
    xp = jnp.concatenate([jnp.zeros((bsz, lo, D_MODEL), F32), meta, x_prompt], axis=1).reshape(bsz * t_pad, D_MODEL)
    qkva, z, qb, k, v, _, vbf, qi, small, _, small_t, small_tbf, kt = _proj(xp, norm_mix[0], w_parts, 512)
    mix_a, ssm_p = _gdn_prompt(qkva, z, small, small_t, *gdn_w, bsz, t_pad, lo)
    bias_tab = _bias_tab(rel_bias, t_pad // LANES)
    o_b = _dsa_prompt(qb, qi, small, kt, vbf, small_tbf, bias_tab, bsz, t_pad, lo, topk_p)
    y_p = _mlp(x_prompt.reshape(bsz * seq, D_MODEL), mix_a.reshape(bsz * seq, A_DIM), o_b.reshape(bsz * seq, B_DIM),
               *mlp_w, 512).reshape(bsz, seq, D_MODEL)
    real = lambda a: a.reshape(bsz, t_pad, a.shape[-1])[:, lo:]
    k_p = real(k).reshape(1, bsz, N_META + seq, HEADS, HD)
    v_p = real(v).reshape(1, bsz, N_META + seq, HEADS, HD)
    kidx_p = real(small)[:, :, S_KIDX:S_KIDX + HD][None]
    conv_p = real(qkva)[:, -(CONV_WIDTH - 1):][None]

    xs = x_sample.reshape(dbs * n_new, D_MODEL)
    (qkva_s, z_s, qb_s, k_s, v_s, kbf_s, vbf_s, qi_s, small_s, smallbf_s, small_t_s, _, _) = _proj(
        xs, norm_mix[0], w_parts, 512)
    mix_a_s, ssm_s = _gdn_sample(qkva_s, z_s, small_s, small_t_s, state_conv[0], state_ssm[0], *gdn_w, n_new)
    bias_s = _bias_sample(rel_bias, past_len, past_len // PAGE + 1)
    o_b_s = _dsa_sample(qb_s, qi_s, small_s, kbf_s, vbf_s, smallbf_s, cache_k[0], cache_v[0], cache_kidx[0], page_table,
                        bias_s, n_new, topk_s)
    y_s = _mlp(xs, mix_a_s, o_b_s.reshape(dbs * n_new, B_DIM), *mlp_w, 512).reshape(dbs, n_new, D_MODEL)
    k_sn = k_s.reshape(1, dbs, n_new, HEADS, HD)
    v_sn = v_s.reshape(1, dbs, n_new, HEADS, HD)
    kidx_s = small_s.reshape(dbs, n_new, LANES)[:, :, S_KIDX:S_KIDX + HD][None]
    conv_s = jnp.concatenate([state_conv[0].astype(F32), qkva_s.reshape(dbs, n_new, 3 * A_DIM)],
                             axis=1)[:, -(CONV_WIDTH - 1):][None]
    return (y_p, y_s, k_p, v_p, kidx_p, conv_p, ssm_p[None], k_sn, v_sn, kidx_s, conv_s, ssm_s[None])
```

```python
import functools
import math

import jax
import jax.numpy as jnp
from jax import lax
from jax.experimental import pallas as pl
from jax.experimental.pallas import tpu as pltpu

F32 = jnp.float32
BF16 = jnp.bfloat16
I32 = jnp.int32
HI = lax.Precision.HIGHEST

D_MODEL = 1024
N_META = 16
HEADS = 8
HD = 64
A_DIM = HEADS * HD
B_DIM = HEADS * HD
CONV_WIDTH = 4
GDN_CHUNK = 64
TOPK_MAX = 256
PAGE = 128
REL_BUCKETS = 32
REL_MAX_DIST = 1024
D_FF = 2816
EPS = 1e-6
LANES = 128
SUBLANES = 8
FRONT = 128
NEG = -1e30
INT_MIN = -2 ** 31
VMEM_LIMIT = 56 * 1024 * 1024

S_KIDX, S_BETA, S_A, S_W = 0, 64, 72, 80


def _mm(a, b):
    return jnp.dot(a, b, preferred_element_type=F32)


def _mm_hi(a, b):
    return jnp.dot(a, b, preferred_element_type=F32, precision=HI)


def _split_bf16(x):
    hi = x.astype(BF16)
    return hi, (x - hi.astype(F32)).astype(BF16)


def _mm_solve(a, b):
    a_hi, a_lo = _split_bf16(a)
    b_hi, b_lo = _split_bf16(b)
    return _mm(a_hi, b_hi) + (_mm(a_hi, b_lo) + _mm(a_lo, b_hi))


def _nt(a, b, precision=None):
    return lax.dot_general(a, b, (((1,), (1,)), ((), ())), preferred_element_type=F32, precision=precision)


def _tn(a, b, precision=None):
    return lax.dot_general(a, b, (((0,), (0,)), ((), ())), preferred_element_type=F32, precision=precision)


def _const_spec(shape):
    nd = len(shape)
    return pl.BlockSpec(shape, lambda *_: (0,) * nd, pipeline_mode=pl.Buffered(1))


def _silu(x):
    return x * jax.nn.sigmoid(x)


def _proj_body(x_ref, g_ref, wa_ref, wz_ref, wb_ref, wqi_ref, ws_ref, wst_ref,
               qkva_ref, z_ref, qb_ref, k_ref, v_ref, kbf_ref, vbf_ref, qi_ref, small_ref, smallbf_ref, smallt_ref,
               smalltbf_ref, kt_ref):
    x = x_ref[...]
    ms = jnp.mean(x * x, axis=-1, keepdims=True)
    h = (x * lax.rsqrt(ms + EPS) * g_ref[...]).astype(BF16)
    qkva_ref[...] = _mm(h, wa_ref[...])
    z_ref[...] = _mm(h, wz_ref[...])
    qkvb = _mm(h, wb_ref[...])
    qb_ref[...] = (qkvb[:, :B_DIM] * 0.125).astype(BF16)
    k = qkvb[:, B_DIM:2 * B_DIM]
    v = qkvb[:, 2 * B_DIM:]
    k_ref[...] = k
    v_ref[...] = v
    kbf_ref[...] = k.astype(BF16)
    vbf_ref[...] = v.astype(BF16)
    qi_ref[...] = (_mm(h, wqi_ref[...]) * 0.125).astype(BF16)
    s = _mm(h, ws_ref[...])
    small_ref[...] = s
    smallbf_ref[...] = s.astype(BF16)
    t = _nt(wst_ref[...], h)
    smallt_ref[...] = t[:LANES]
    smalltbf_ref[...] = t[:LANES].astype(BF16)
    kt_ref[...] = t[LANES:].astype(BF16)


def _split_w_in(w_in):
    c = 0
    parts = []
    for n in (3 * A_DIM, A_DIM, HEADS, HEADS, 3 * B_DIM, HEADS * HD, HD, HEADS):
        parts.append(w_in[:, c:c + n])
        c += n
    w_qkva, w_z, w_beta, w_a, w_qkvb, w_qi, w_kidx, w_w = parts
    pad = jnp.zeros((w_in.shape[0], LANES - (HD + 3 * HEADS)), w_in.dtype)
    w_small = jnp.concatenate([w_kidx, w_beta, w_a, w_w, pad], axis=1)
    bf = lambda t: t.astype(BF16)
    w_t = jnp.concatenate([w_small, w_qkvb[:, B_DIM:2 * B_DIM]], axis=1).T
    return bf(w_qkva), bf(w_z), bf(w_qkvb), bf(w_qi), bf(w_small), bf(w_t)


def _proj(x2d, g, w_parts, tm):
    n = x2d.shape[0]
    assert n % tm == 0 and tm % LANES == 0
    wa, wz, wb, wqi, ws, wst = w_parts
    row = lambda w: pl.BlockSpec((tm, w), lambda i: (i, 0))
    outs = [
        (jax.ShapeDtypeStruct((n, 3 * A_DIM), F32), row(3 * A_DIM)),
        (jax.ShapeDtypeStruct((n, A_DIM), F32), row(A_DIM)),
        (jax.ShapeDtypeStruct((n, B_DIM), BF16), row(B_DIM)),
        (jax.ShapeDtypeStruct((n, B_DIM), F32), row(B_DIM)),
        (jax.ShapeDtypeStruct((n, B_DIM), F32), row(B_DIM)),
        (jax.ShapeDtypeStruct((n, B_DIM), BF16), row(B_DIM)),
        (jax.ShapeDtypeStruct((n, B_DIM), BF16), row(B_DIM)),
        (jax.ShapeDtypeStruct((n, HEADS * HD), BF16), row(HEADS * HD)),
        (jax.ShapeDtypeStruct((n, LANES), F32), row(LANES)),
        (jax.ShapeDtypeStruct((n, LANES), BF16), row(LANES)),
        (jax.ShapeDtypeStruct((LANES, n), F32), pl.BlockSpec((LANES, tm), lambda i: (0, i))),
        (jax.ShapeDtypeStruct((LANES, n), BF16), pl.BlockSpec((LANES, tm), lambda i: (0, i))),
        (jax.ShapeDtypeStruct((B_DIM, n), BF16), pl.BlockSpec((B_DIM, tm), lambda i: (0, i))),
    ]
    return pl.pallas_call(
        _proj_body,
        grid=(n // tm,),
        in_specs=[row(D_MODEL), _const_spec((1, D_MODEL)), _const_spec(wa.shape), _const_spec(wz.shape),
                  _const_spec(wb.shape), _const_spec(wqi.shape), _const_spec(ws.shape), _const_spec(wst.shape)],
        out_specs=[o[1] for o in outs],
        out_shape=[o[0] for o in outs],
        compiler_params=pltpu.CompilerParams(dimension_semantics=("parallel",), vmem_limit_bytes=VMEM_LIMIT),
        name="proj",
    )(x2d, g.reshape(1, D_MODEL), wa, wz, wb, wqi, ws, wst)


def _gdn_prep(qkv, tail, conv_w, small, small_t, a_log_row, dt_row, a_log_col, dt_col, valid_col, valid_row, chunk):
    rows = qkv.shape[0]
    n_chunks = rows // chunk
    ys = []
    for c in range(n_chunks):
        ext = jnp.concatenate([tail[c], qkv[c * chunk:(c + 1) * chunk]], axis=0)
        y = sum(ext[SUBLANES - (CONV_WIDTH - 1) + j: SUBLANES - (CONV_WIDTH - 1) + j + chunk] * conv_w[j:j + 1]
                for j in range(CONV_WIDTH))
        ys.append(_silu(y))
    y = ys[0] if n_chunks == 1 else jnp.concatenate(ys, axis=0)
    beta_c = jnp.where(valid_col, jax.nn.sigmoid(small), 0.0)
    beta_r = jnp.where(valid_row, jax.nn.sigmoid(small_t), 0.0)
    g_c = jnp.where(valid_col, -jnp.exp(a_log_row) * jax.nn.softplus(small + dt_row), 0.0)
    g_r = jnp.where(valid_row, -jnp.exp(a_log_col) * jax.nn.softplus(small_t + dt_col), 0.0)
    ri = lax.broadcasted_iota(I32, (rows, rows), 0)
    ci = lax.broadcasted_iota(I32, (rows, rows), 1)
    same = (ri // chunk) == (ci // chunk)
    low = jnp.where(same & (ci <= ri), 1.0, 0.0).astype(F32)
    b_c = _mm_hi(low, g_c)
    b_r = _nt(g_r, low, precision=HI)
    return y, beta_c, b_c, b_r


def _gdn_head(qh, kh, vh, beta_c, b_c, b_r, s0, o_norm, zh, chunk):
    c = chunk
    qn = qh * lax.rsqrt(jnp.sum(qh * qh, axis=-1, keepdims=True) + EPS) * (HD ** -0.5)
    kn = kh * lax.rsqrt(jnp.sum(kh * kh, axis=-1, keepdims=True) + EPS)
    ri = lax.broadcasted_iota(I32, (c, c), 0)
    ci = lax.broadcasted_iota(I32, (c, c), 1)
    incl = ci <= ri
    strict = ci < ri
    decay = jnp.where(incl, jnp.exp(jnp.where(incl, b_c - b_r, 0.0)), 0.0)
    kb, qb16, s0b = kn.astype(BF16), qn.astype(BF16), s0.astype(BF16)
    kk = _nt(kb, kb)
    qk = _nt(qb16, kb)
    a_mat = jnp.where(strict, beta_c * kk * decay, 0.0)
    eb = jnp.exp(b_c)
    y = jnp.concatenate([beta_c * vh, (beta_c * eb) * kn], axis=1)
    bk = -a_mat
    n_levels = int(math.log2(c))
    for lvl in range(n_levels):
        if lvl + 1 < n_levels:
            prod = _mm_solve(bk, jnp.concatenate([y, bk], axis=1))
            y, bk = y + prod[:, :2 * HD], prod[:, 2 * HD:]
        else:
            y = y + _mm_solve(bk, y)
    u, w = y[:, :HD], y[:, HD:]
    delta = u - _mm(w.astype(BF16), s0b)
    deltab = delta.astype(BF16)
    o = eb * _mm(qb16, s0b) + _mm((qk * decay).astype(BF16), deltab)
    b_last = b_c[c - 1:c, :]
    s_new = jnp.exp(b_last) * s0 + _tn((kn * jnp.exp(b_last - b_c)).astype(BF16), deltab)
    on = o * lax.rsqrt(jnp.mean(o * o, axis=-1, keepdims=True) + EPS) * o_norm
    return s_new, on * _silu(zh)


def _bdot(a, b, ca, cb):
    return lax.dot_general(a, b, (((ca,), (cb,)), ((0,), (0,))), preferred_element_type=F32)


def _bmm_solve(a, b):
    a_hi, a_lo = _split_bf16(a)
    b_hi, b_lo = _split_bf16(b)
    return _bdot(a_hi, b_hi, 2, 1) + (_bdot(a_hi, b_lo, 2, 1) + _bdot(a_lo, b_hi, 2, 1))


def _gdn_solve(q, k, v, beta, b_c, b_r, chunk):
    c = chunk
    qn = q * lax.rsqrt(jnp.sum(q * q, axis=-1, keepdims=True) + EPS) * (HD ** -0.5)
    kn = k * lax.rsqrt(jnp.sum(k * k, axis=-1, keepdims=True) + EPS)
    ri = lax.broadcasted_iota(I32, (1, c, c), 1)
    ci = lax.broadcasted_iota(I32, (1, c, c), 2)
    incl = ci <= ri
    decay = jnp.where(incl, jnp.exp(jnp.where(incl, b_c - b_r, 0.0)), 0.0)
    kb, qb16 = kn.astype(BF16), qn.astype(BF16)
    kk = _bdot(kb, kb, 2, 2)
    qk = _bdot(qb16, kb, 2, 2)
    bk = -jnp.where(ci < ri, beta * kk * decay, 0.0)
    eb = jnp.exp(b_c)
    y = jnp.concatenate([beta * v, (beta * eb) * kn], axis=2)
    n_levels = int(math.log2(c))
    for lvl in range(n_levels):
        if lvl + 1 < n_levels:
            prod = _bmm_solve(bk, jnp.concatenate([y, bk], axis=2))
            y, bk = y + prod[:, :, :2 * HD], prod[:, :, 2 * HD:]
        else:
            y = y + _bmm_solve(bk, y)
    b_last = b_c[:, c - 1:c, :]
    return (y[:, :, :HD], y[:, :, HD:].astype(BF16), (qk * decay).astype(BF16), qb16,
            (kn * jnp.exp(b_last - b_c)).astype(BF16), eb, jnp.exp(b_last))


def _gdn_apply(u, w, attn, qb16, kd, eb, eb_last, s0, o_norm, z):
    s0b = s0.astype(BF16)
    delta = u - _bdot(w, s0b, 2, 1)
    deltab = delta.astype(BF16)
    o = eb * _bdot(qb16, s0b, 2, 1) + _bdot(attn, deltab, 2, 1)
    s_new = eb_last * s0 + _bdot(kd, deltab, 1, 1)
    on = o * lax.rsqrt(jnp.mean(o * o, axis=-1, keepdims=True) + EPS) * o_norm
    return s_new, on * _silu(z)


def _gdn_chunk_all_heads(y, beta_c, b_c, b_r, z, o_norm, r0, chunk, get_s0, put_s, put_o):
    for h in range(HEADS):
        sl = slice(r0, r0 + chunk)
        qh = y[sl, h * HD:(h + 1) * HD]
        kh = y[sl, A_DIM + h * HD:A_DIM + (h + 1) * HD]
        vh = y[sl, 2 * A_DIM + h * HD:2 * A_DIM + (h + 1) * HD]
        bc = beta_c[sl, S_BETA + h:S_BETA + h + 1]
        bcol = b_c[sl, S_A + h:S_A + h + 1]
        brow = b_r[S_A + h:S_A + h + 1, r0:r0 + chunk]
        s_new, out = _gdn_head(qh, kh, vh, bc, bcol, brow, get_s0(h), o_norm, z[sl, h * HD:(h + 1) * HD], chunk)
        put_s(h, s_new)
        put_o(h, out)


def _gdn_prompt_body(qkv_ref, z_ref, small_ref, smallt_ref, convw_ref, alr_ref, dtr_ref, alc_ref, dtc_ref, onorm_ref,
                     mix_ref, s_ref, tail_scr, *, rows, lo):
    t = pl.program_id(1)

    @pl.when(t == 0)
    def _():
        s_ref[...] = jnp.zeros_like(s_ref)
        tail_scr[...] = jnp.zeros_like(tail_scr)

    chunk = GDN_CHUNK
    qkv = qkv_ref[0]
    tails = [tail_scr[...]] + [qkv[c * chunk - SUBLANES:c * chunk] for c in range(1, rows // chunk)]
    pos_c = t * rows + lax.broadcasted_iota(I32, (rows, LANES), 0)
    pos_r = t * rows + lax.broadcasted_iota(I32, (LANES, rows), 1)
    y, beta_c, b_c, b_r = _gdn_prep(qkv, tails, convw_ref[...], small_ref[0], smallt_ref[...],
                                    alr_ref[...], dtr_ref[...], alc_ref[...], dtc_ref[...],
                                    pos_c >= lo, pos_r >= lo, chunk)
    tail_scr[...] = qkv[rows - SUBLANES:rows]
    z = z_ref[0]
    n_chunks = rows // chunk
    units = [(c, h) for c in range(n_chunks) for h in range(HEADS)]
    rs = lambda c: slice(c * chunk, (c + 1) * chunk)
    stack = lambda a, off: jnp.stack([a[rs(c), off + h * HD:off + (h + 1) * HD] for c, h in units])
    col = lambda a, off: jnp.stack([a[rs(c), off + h:off + h + 1] for c, h in units])
    sol = _gdn_solve(stack(y, 0), stack(y, A_DIM), stack(y, 2 * A_DIM), col(beta_c, S_BETA), col(b_c, S_A),
                     jnp.stack([b_r[S_A + h:S_A + h + 1, rs(c)] for c, h in units]), chunk)
    z3 = stack(z, 0)
    o_norm = onorm_ref[...]
    for c in range(n_chunks):
        g = slice(c * HEADS, (c + 1) * HEADS)
        s_new, out = _gdn_apply(*[t[g] for t in sol], s_ref[0], o_norm, z3[g])
        s_ref[0] = s_new
        for h in range(HEADS):
            mix_ref[0, rs(c), h * HD:(h + 1) * HD] = out[h].astype(mix_ref.dtype)


def _gate_params(a_log, dt_bias):
    row = lambda v: jnp.zeros((1, LANES), F32).at[0, S_A:S_A + HEADS].set(v.astype(F32))
    return row(a_log), row(dt_bias), row(a_log).T, row(dt_bias).T


def _gdn_prompt(qkva, z, small, small_t, conv_w, a_log, dt_bias, o_norm, bsz, t_pad, lo):
    rows = LANES
    nt = t_pad // rows
    alr, dtr, alc, dtc = _gate_params(a_log, dt_bias)
    body = functools.partial(_gdn_prompt_body, rows=rows, lo=lo)
    mix, s = pl.pallas_call(
        body,
        grid=(bsz, nt),
        in_specs=[
            pl.BlockSpec((1, rows, 3 * A_DIM), lambda b, t: (b, t, 0)),
            pl.BlockSpec((1, rows, A_DIM), lambda b, t: (b, t, 0)),
            pl.BlockSpec((1, rows, LANES), lambda b, t: (b, t, 0)),
            pl.BlockSpec((LANES, rows), lambda b, t: (0, b * nt + t)),
            _const_spec((CONV_WIDTH, 3 * A_DIM)), _const_spec((1, LANES)), _const_spec((1, LANES)),
            _const_spec((LANES, 1)), _const_spec((LANES, 1)), _const_spec((1, HD)),
        ],
        out_specs=[
            pl.BlockSpec((1, rows, A_DIM), lambda b, t: (b, jnp.maximum(t - 1, 0), 0)),
            pl.BlockSpec((1, HEADS, HD, HD), lambda b, t: (b, 0, 0, 0)),
        ],
        out_shape=[jax.ShapeDtypeStruct((bsz, t_pad - FRONT, A_DIM), BF16),
                   jax.ShapeDtypeStruct((bsz, HEADS, HD, HD), F32)],
        scratch_shapes=[pltpu.VMEM((SUBLANES, 3 * A_DIM), F32)],
        compiler_params=pltpu.CompilerParams(dimension_semantics=("parallel", "arbitrary"),
                                             vmem_limit_bytes=VMEM_LIMIT),
        name="gdn_prompt",
    )(qkva.reshape(bsz, t_pad, 3 * A_DIM), z.reshape(bsz, t_pad, A_DIM), small.reshape(bsz, t_pad, LANES), small_t,
      conv_w, alr, dtr, alc, dtc, o_norm.reshape(1, HD))
    return mix, s


def _gdn_sample_body(qkv_ref, z_ref, small_ref, smallt_ref, conv0_ref, s0_ref, convw_ref, alr_ref, dtr_ref, alc_ref,
                     dtc_ref, onorm_ref, mix_ref, s_ref, *, chunk):
    rows = LANES
    n_seq = rows // chunk
    qkv = qkv_ref[...]
    tails = [conv0_ref[i] for i in range(n_seq)]
    y, beta_c, b_c, b_r = _gdn_prep(qkv, tails, convw_ref[...], small_ref[...], smallt_ref[...],
                                    alr_ref[...], dtr_ref[...], alc_ref[...], dtc_ref[...], True, True, chunk)
    z = z_ref[...]
    o_norm = onorm_ref[...]
    for i in range(n_seq):
        def put_s(h, s, i=i):
            s_ref[i, h] = s

        def put_o(h, o, i=i):
            mix_ref[i * chunk:(i + 1) * chunk, h * HD:(h + 1) * HD] = o.astype(mix_ref.dtype)

        _gdn_chunk_all_heads(y, beta_c, b_c, b_r, z, o_norm, i * chunk, chunk, lambda h, i=i: s0_ref[i, h], put_s, put_o)


def _gdn_sample(qkva, z, small, small_t, state_conv, state_ssm, conv_w, a_log, dt_bias, o_norm, n_new):
    n = qkva.shape[0]
    bsz = n // n_new
    rows = LANES
    n_seq = rows // n_new
    assert n_new == SUBLANES and n % rows == 0
    alr, dtr, alc, dtc = _gate_params(a_log, dt_bias)
    conv0 = jnp.pad(state_conv.astype(F32), ((0, 0), (SUBLANES - (CONV_WIDTH - 1), 0), (0, 0)))
    body = functools.partial(_gdn_sample_body, chunk=n_new)
    mix, s = pl.pallas_call(
        body,
        grid=(n // rows,),
        in_specs=[
            pl.BlockSpec((rows, 3 * A_DIM), lambda g: (g, 0)),
            pl.BlockSpec((rows, A_DIM), lambda g: (g, 0)),
            pl.BlockSpec((rows, LANES), lambda g: (g, 0)),
            pl.BlockSpec((LANES, rows), lambda g: (0, g)),
            pl.BlockSpec((n_seq, SUBLANES, 3 * A_DIM), lambda g: (g, 0, 0)),
            pl.BlockSpec((n_seq, HEADS, HD, HD), lambda g: (g, 0, 0, 0)),
            _const_spec((CONV_WIDTH, 3 * A_DIM)), _const_spec((1, LANES)), _const_spec((1, LANES)),
            _const_spec((LANES, 1)), _const_spec((LANES, 1)), _const_spec((1, HD)),
        ],
        out_specs=[
            pl.BlockSpec((rows, A_DIM), lambda g: (g, 0)),
            pl.BlockSpec((n_seq, HEADS, HD, HD), lambda g: (g, 0, 0, 0)),
        ],
        out_shape=[jax.ShapeDtypeStruct((n, A_DIM), BF16), jax.ShapeDtypeStruct((bsz, HEADS, HD, HD), F32)],
        compiler_params=pltpu.CompilerParams(dimension_semantics=("parallel",), vmem_limit_bytes=VMEM_LIMIT),
        name="gdn_sample",
    )(qkva, z, small, small_t, conv0, state_ssm.astype(F32), conv_w, alr, dtr, alc, dtc, o_norm.reshape(1, HD))
    return mix, s


def _rel_bucket(d):
    d = jnp.maximum(d, 0)
    max_exact = REL_BUCKETS // 2
    df = jnp.maximum(d, 1).astype(F32)
    large = max_exact + (jnp.log(df / max_exact) / math.log(REL_MAX_DIST / max_exact)
                         * (REL_BUCKETS - max_exact)).astype(I32)
    large = jnp.minimum(large, REL_BUCKETS - 1)
    return jnp.where(d < max_exact, d, large)


def _bias_lookup(bucket, rb_ref, h):
    acc = jnp.zeros(bucket.shape, F32)
    for b in range(REL_BUCKETS):
        acc = jnp.where(bucket == b, rb_ref[b, h], acc)
    return acc


def _bias_tab_body(rb_ref, out_ref):
    dlt = pl.program_id(0)
    i = lax.broadcasted_iota(I32, (LANES, LANES), 0)
    j = lax.broadcasted_iota(I32, (LANES, LANES), 1)
    bucket = _rel_bucket(dlt * LANES + i - j)
    for h in range(HEADS):
        out_ref[h, 0] = _bias_lookup(bucket, rb_ref, h)


def _bias_tab(rel_bias, n_tiles):
    return pl.pallas_call(
        _bias_tab_body,
        grid=(n_tiles,),
        in_specs=[pl.BlockSpec(memory_space=pltpu.SMEM)],
        out_specs=pl.BlockSpec((HEADS, 1, LANES, LANES), lambda t: (0, t, 0, 0)),
        out_shape=jax.ShapeDtypeStruct((HEADS, n_tiles, LANES, LANES), F32),
        name="bias_tab",
    )(rel_bias.astype(F32))


def _bias_sample_body(rb_ref, out_ref, *, past_len):
    blk = pl.program_id(0)
    q = lax.broadcasted_iota(I32, (SUBLANES, LANES), 0)
    s = blk * LANES + lax.broadcasted_iota(I32, (SUBLANES, LANES), 1)
    bucket = _rel_bucket(past_len + q - s)
    for h in range(HEADS):
        out_ref[h * SUBLANES:(h + 1) * SUBLANES, :] = _bias_lookup(bucket, rb_ref, h)


def _bias_sample(rel_bias, past_len, n_blocks):
    return pl.pallas_call(
        functools.partial(_bias_sample_body, past_len=past_len),
        grid=(n_blocks,),
        in_specs=[pl.BlockSpec(memory_space=pltpu.SMEM)],
        out_specs=pl.BlockSpec((HEADS * SUBLANES, LANES), lambda t: (0, t)),
        out_shape=jax.ShapeDtypeStruct((HEADS * SUBLANES, n_blocks * LANES), F32),
        name="bias_sample",
    )(rel_bias.astype(F32))


def _sort_key(score, admissible):
    score = jnp.where(score == 0.0, 0.0, score)
    bits = pltpu.bitcast(score, I32)
    key = jnp.where(bits < 0, bits ^ 0x7FFFFFFF, bits)
    return jnp.where(admissible, key, INT_MIN)


def _kth_largest_key(count_ge, rows, topk):
    c0 = jnp.broadcast_to(count_ge(jnp.zeros((rows, LANES), I32)), (rows, LANES))
    t0 = jnp.where(c0 >= topk, 0, INT_MIN).astype(I32)
    c0 = jnp.where(c0 >= topk, c0, float(2 ** 30))

    def bit_step(i, carry):
        t, c = carry
        cand = t + lax.shift_left(jnp.int32(1), 30 - i)
        cc = jnp.broadcast_to(count_ge(cand), (rows, LANES))
        ok = cc >= topk
        return jnp.where(ok, cand, t), jnp.where(ok, cc, c)

    return lax.fori_loop(0, 31, bit_step, (t0, c0))


def _tie_mask(key, thr, need, running, tri):
    eq = (key == thr) & (key != INT_MIN)
    pref = _mm(jnp.where(eq, 1.0, 0.0).astype(BF16), tri) + running
    sel = (key > thr) | (eq & (pref <= need))
    return sel, jnp.broadcast_to(pref[:, LANES - 1:LANES], pref.shape)


def _tri_incl():
    r = lax.broadcasted_iota(I32, (LANES, LANES), 0)
    c = lax.broadcasted_iota(I32, (LANES, LANES), 1)
    return jnp.where(r <= c, 1.0, 0.0).astype(BF16)


def _dsa_prompt_tiles(first, n_tiles, qi, w_raw, kt_ref, kidxt_ref, bias_ref, out_ref,
                      qh_scr, vh_scr, keys_scr, mbias_scr, o_scr, *, lo, topk):
    nblk = first + n_tiles
    tq = n_tiles * LANES
    wid = nblk * LANES
    t_pos = first * LANES + lax.broadcasted_iota(I32, (tq, wid), 0)
    s_pos = lax.broadcasted_iota(I32, (tq, wid), 1)
    w = w_raw * (HEADS ** -0.5)
    kxt = kidxt_ref[S_KIDX:S_KIDX + HD, 0:wid]
    acc = jnp.zeros((tq, wid), F32)
    for h in range(HEADS):
        acc = acc + jnp.maximum(_mm(qi[:, h * HD:(h + 1) * HD], kxt), 0.0) * w[:, h:h + 1]
    keys_scr[:, 0:wid] = _sort_key(acc, (s_pos <= t_pos) & (s_pos >= lo))

    def count_ge(cand):
        return jnp.sum(jnp.where(keys_scr[:, 0:wid] >= cand[:, 0:1], 1.0, 0.0), axis=1, keepdims=True)

    thr, cnt = _kth_largest_key(count_ge, tq, topk)
    keys = keys_scr[:, 0:wid]
    mbias_scr[:, 0:wid] = jnp.where(keys >= jnp.maximum(thr[:, 0:1], INT_MIN + 1), 0.0, NEG)

    @pl.when(jnp.max(jnp.where(cnt < float(2 ** 30), cnt, 0.0)) > topk)
    def _():
        need = topk - jnp.sum(jnp.where(keys_scr[:, 0:wid] > thr[:, 0:1], 1.0, 0.0), axis=1, keepdims=True)
        tri = _tri_incl()
        running = jnp.zeros((tq, LANES), F32)
        for jb in range(nblk):
            cols = slice(jb * LANES, (jb + 1) * LANES)
            sel, running = _tie_mask(keys_scr[:, cols], thr, need, running, tri)
            mbias_scr[:, cols] = jnp.where(sel, 0.0, NEG)

    def head_attn(h, carry):
        bias = jnp.concatenate(
            [jnp.concatenate([bias_ref[h, max(first + r - jb, 0)] for jb in range(nblk)], axis=1)
             for r in range(n_tiles)], axis=0)
        kt = kt_ref[pl.ds(pl.multiple_of(h * HD, HD), HD), 0:wid]
        lg = _mm(qh_scr[h], kt) + bias + mbias_scr[:, 0:wid]
        p = jnp.exp(lg - jnp.max(lg, axis=1, keepdims=True))
        inv = 1.0 / jnp.sum(p, axis=1, keepdims=True)
        o_scr[h] = _mm(p.astype(BF16), vh_scr[h, 0:wid, :]) * inv
        return carry

    lax.fori_loop(0, HEADS, head_attn, 0)
    for h in range(HEADS):
        out_ref[0, :, h * HD:(h + 1) * HD] = o_scr[h].astype(out_ref.dtype)


def _dsa_prompt_body(*refs, lo, topk, n_tiles, n_steps):
    qb_refs, qi_refs, small_refs = refs[:n_tiles], refs[n_tiles:2 * n_tiles], refs[2 * n_tiles:3 * n_tiles]
    (kt_ref, vbf_ref, kidxt_ref, bias_ref, out_ref, qh_scr, vh_scr, keys_scr, mbias_scr, o_scr) = refs[3 * n_tiles:]
    js = pl.program_id(1)

    @pl.when(js == 0)
    def _():
        for h in range(HEADS):
            vh_scr[h] = vbf_ref[0, :, h * HD:(h + 1) * HD]

    for r in range(n_tiles):
        for h in range(HEADS):
            qh_scr[h, r * LANES:(r + 1) * LANES, :] = qb_refs[r][0, :, h * HD:(h + 1) * HD]
    qi = jnp.concatenate([ref[0] for ref in qi_refs], axis=0)
    w_raw = jnp.concatenate([ref[0][:, S_W:S_W + HEADS] for ref in small_refs], axis=0)
    for step in range(n_steps):
        @pl.when(js == step)
        def _(step=step):
            _dsa_prompt_tiles(1 + step * n_tiles, n_tiles, qi, w_raw, kt_ref, kidxt_ref, bias_ref, out_ref,
                              qh_scr, vh_scr, keys_scr, mbias_scr, o_scr, lo=lo, topk=topk)


DSA_TILES_PER_STEP = 4


def _dsa_prompt(qb, qi, small, kt, vbf, smalltbf, bias_tab, bsz, t_pad, lo, topk):
    n_real = t_pad // LANES - 1
    n_tiles = max(d for d in (1, 2, DSA_TILES_PER_STEP) if n_real % d == 0)
    n_steps = n_real // n_tiles
    rows = n_tiles * LANES
    r3 = lambda a: a.reshape(bsz, t_pad, a.shape[-1])
    tiles = lambda w: [pl.BlockSpec((1, LANES, w), functools.partial(lambda b, js, r: (b, 1 + js * n_tiles + r, 0), r=r))
                       for r in range(n_tiles)]
    seq_cols = lambda r: pl.BlockSpec((r, t_pad), lambda b, js: (0, b))
    return pl.pallas_call(
        functools.partial(_dsa_prompt_body, lo=lo, topk=topk, n_tiles=n_tiles, n_steps=n_steps),
        grid=(bsz, n_steps),
        in_specs=tiles(B_DIM) + tiles(HEADS * HD) + tiles(LANES) + [
            seq_cols(B_DIM), pl.BlockSpec((1, t_pad, B_DIM), lambda b, js: (b, 0, 0)), seq_cols(LANES),
            _const_spec(bias_tab.shape)],
        out_specs=pl.BlockSpec((1, rows, B_DIM), lambda b, js: (b, js, 0)),
        out_shape=jax.ShapeDtypeStruct((bsz, t_pad - FRONT, B_DIM), BF16),
        scratch_shapes=[pltpu.VMEM((HEADS, rows, HD), BF16), pltpu.VMEM((HEADS, t_pad, HD), BF16),
                        pltpu.VMEM((rows, t_pad), I32), pltpu.VMEM((rows, t_pad), F32),
                        pltpu.VMEM((HEADS, rows, HD), F32)],
        compiler_params=pltpu.CompilerParams(dimension_semantics=("parallel", "arbitrary"),
                                             vmem_limit_bytes=VMEM_LIMIT),
        name="dsa_prompt",
    )(*([r3(qb)] * n_tiles), *([r3(qi)] * n_tiles), *([r3(small)] * n_tiles), kt, r3(vbf), smalltbf, bias_tab)


PAGES_PER_STEP = 8


def _dsa_sample_body(pt_ref, *refs, n_new, past_len, topk):
    npp = PAGES_PER_STEP
    kidx_refs, k_refs, v_refs = refs[:npp], refs[npp:2 * npp], refs[2 * npp:3 * npp]
    (qi_ref, qbd_ref, w_ref, knew_ref, vnew_ref, kidxnew_ref, bias_ref, out_ref,
     keys_scr, mb_scr, logit_scr, vt_scr) = refs[3 * npp:]
    del pt_ref
    s = pl.program_id(1)
    n_page_steps = past_len // (PAGE * npp)
    n_blocks = past_len // PAGE + 1
    qi = qi_ref[0]
    qbd = qbd_ref[0]
    w = w_ref[0] * (HEADS ** -0.5)
    q_row = lax.broadcasted_iota(I32, (n_new, LANES), 0)
    lane = lax.broadcasted_iota(I32, (n_new, LANES), 1)

    def do_block(off, kxt, kt, vt, admissible):
        rel = jnp.maximum(_mm(qi, kxt), 0.0) * w
        sc = rel[0:n_new]
        for h in range(1, HEADS):
            sc = sc + rel[h * n_new:(h + 1) * n_new]
        keys_scr[:, pl.ds(off, LANES)] = _sort_key(sc, admissible)
        logit_scr[:, pl.ds(off, LANES)] = _mm(qbd, kt) + bias_ref[:, pl.ds(off, LANES)]
        vt_scr[:, pl.ds(off, LANES)] = vt

    @pl.when(s < n_page_steps)
    def _():
        for i in range(npp):
            off = pl.multiple_of((s * npp + i) * PAGE, PAGE)
            do_block(off, kidx_refs[i][0].astype(BF16), k_refs[i][0].reshape(B_DIM, PAGE).astype(BF16),
                     v_refs[i][0].reshape(B_DIM, PAGE).astype(BF16), True)

    @pl.when(s == n_page_steps)
    def _():
        tr = lambda a: jnp.concatenate(
            [a.astype(F32), jnp.zeros((PAGE - n_new, a.shape[1]), F32)], axis=0).T.astype(BF16)
        do_block(past_len, tr(kidxnew_ref[0])[S_KIDX:S_KIDX + HD], tr(knew_ref[0]), tr(vnew_ref[0]), lane <= q_row)

        keys = keys_scr[...]
        thr, cnt = _kth_largest_key(
            lambda cand: jnp.sum(jnp.where(keys >= cand[:, 0:1], 1.0, 0.0), axis=1, keepdims=True), n_new, topk)
        mb_scr[...] = jnp.where(keys >= jnp.maximum(thr[:, 0:1], INT_MIN + 1), 0.0, NEG)

        @pl.when(jnp.max(jnp.where(cnt < float(2 ** 30), cnt, 0.0)) > topk)
        def _():
            need = topk - jnp.sum(jnp.where(keys > thr[:, 0:1], 1.0, 0.0), axis=1, keepdims=True)
            tri = _tri_incl()

            def mask_blk(jb, running):
                cols = pl.ds(pl.multiple_of(jb * LANES, LANES), LANES)
                sel, running = _tie_mask(keys_scr[:, cols], thr, need, running, tri)
                mb_scr[:, cols] = jnp.where(sel, 0.0, NEG)
                return running

            lax.fori_loop(0, n_blocks, mask_blk, jnp.zeros((n_new, LANES), F32))

        lg = logit_scr[...] + jnp.concatenate([mb_scr[...]] * HEADS, axis=0)
        m = jnp.max(lg, axis=1, keepdims=True)
        p = jnp.exp(lg - m)
        pb = (p * (1.0 / jnp.sum(p, axis=1, keepdims=True))).astype(BF16)
        out_ref[0] = _nt(vt_scr[...], pb)


def _dsa_sample(qb, qi, small, kbf, vbf, smallbf, cache_k, cache_v, cache_kidx, page_table, bias_s, n_new, topk):
    dbs, n_pages = page_table.shape
    past_len = n_pages * PAGE
    npp = PAGES_PER_STEP
    assert n_pages % npp == 0 and n_new == SUBLANES
    n_page_steps = n_pages // npp
    n_blocks = n_pages + 1
    n_pool = cache_k.shape[0]
    qi_rows = qi.reshape(dbs, n_new, HEADS, HD).transpose(0, 2, 1, 3).reshape(dbs, HEADS * n_new, HD)
    q4 = qb.reshape(dbs, n_new, HEADS, HD).transpose(0, 2, 1, 3)
    eye = jnp.eye(HEADS, dtype=qb.dtype)
    qbd = (q4[:, :, :, None, :] * eye[None, :, None, :, None]).reshape(dbs, HEADS * n_new, B_DIM)
    w_rows = small.reshape(dbs, n_new, LANES)[:, :, S_W:S_W + HEADS].transpose(0, 2, 1).reshape(dbs, HEADS * n_new, 1)
    w_rows = jnp.broadcast_to(w_rows, (dbs, HEADS * n_new, LANES))

    def page_spec(shape, i):
        def imap(b, s, pt):
            return (pt[b, jnp.minimum(s, n_page_steps - 1) * npp + i],) + (0,) * len(shape)
        return pl.BlockSpec((1,) + shape, imap)

    per_seq = lambda r, w: pl.BlockSpec((1, r, w), lambda b, s, pt: (b, 0, 0))
    in_specs = ([page_spec((HD, PAGE), i) for i in range(npp)] + [page_spec((HEADS, HD, PAGE), i) for i in range(npp)]
                + [page_spec((HEADS, HD, PAGE), i) for i in range(npp)]
                + [per_seq(HEADS * n_new, HD), per_seq(HEADS * n_new, B_DIM), per_seq(HEADS * n_new, LANES),
                   per_seq(n_new, B_DIM), per_seq(n_new, B_DIM), per_seq(n_new, LANES),
                   pl.BlockSpec(bias_s.shape, lambda b, s, pt: (0, 0), pipeline_mode=pl.Buffered(1))])
    grid_spec = pltpu.PrefetchScalarGridSpec(
        num_scalar_prefetch=1,
        grid=(dbs, n_page_steps + 1),
        in_specs=in_specs,
        out_specs=pl.BlockSpec((1, B_DIM, HEADS * n_new), lambda b, s, pt: (b, 0, 0)),
        scratch_shapes=[pltpu.VMEM((n_new, n_blocks * LANES), I32), pltpu.VMEM((n_new, n_blocks * LANES), F32),
                        pltpu.VMEM((HEADS * n_new, n_blocks * LANES), F32),
                        pltpu.VMEM((B_DIM, n_blocks * LANES), BF16)],
    )
    ck = cache_k.transpose(0, 2, 3, 1)
    cv = cache_v.transpose(0, 2, 3, 1)
    cki = cache_kidx.transpose(0, 2, 1)
    r3 = lambda a: a.reshape(dbs, n_new, a.shape[-1])
    out_t = pl.pallas_call(
        functools.partial(_dsa_sample_body, n_new=n_new, past_len=past_len, topk=topk),
        grid_spec=grid_spec,
        out_shape=jax.ShapeDtypeStruct((dbs, B_DIM, HEADS * n_new), F32),
        compiler_params=pltpu.CompilerParams(dimension_semantics=("parallel", "arbitrary"),
                                             vmem_limit_bytes=VMEM_LIMIT),
        name="dsa_sample",
    )(page_table, *([cki] * npp), *([ck] * npp), *([cv] * npp),
      qi_rows, qbd, w_rows, r3(kbf), r3(vbf), r3(smallbf), bias_s)
    o5 = out_t.reshape(dbs, HEADS, HD, HEADS, n_new)
    o = jnp.stack([o5[:, h, :, h, :] for h in range(HEADS)], axis=1)
    return o.transpose(0, 3, 1, 2).reshape(dbs, n_new, B_DIM).astype(BF16)


def _rms(x, g):
    return x * lax.rsqrt(jnp.mean(x * x, axis=-1, keepdims=True) + EPS) * g


def _mlp_body(x_ref, ma_ref, ob_ref, woa_ref, wob_ref, gffn_ref, wg_ref, wu_ref, wd_ref, gfin_ref, y_ref):
    h = x_ref[...] + (_mm(ma_ref[...], woa_ref[...]) + _mm(ob_ref[...], wob_ref[...]))
    hn = _rms(h, gffn_ref[...]).astype(BF16)
    act = _silu(_mm(hn, wg_ref[...])) * _mm(hn, wu_ref[...])
    y_ref[...] = _rms(h + _mm(act.astype(BF16), wd_ref[...]), gfin_ref[...])


def _mlp(x2d, mix_a, o_b, w_out, g_ffn, w_gate, w_up, w_down, g_final, tm):
    n = x2d.shape[0]
    assert n % tm == 0
    row = lambda w: pl.BlockSpec((tm, w), lambda i: (i, 0))
    woa, wob = w_out[:A_DIM].astype(BF16), w_out[A_DIM:].astype(BF16)
    return pl.pallas_call(
        _mlp_body,
        grid=(n // tm,),
        in_specs=[row(D_MODEL), row(A_DIM), row(B_DIM), _const_spec(woa.shape), _const_spec(wob.shape),
                  _const_spec((1, D_MODEL)), _const_spec(w_gate.shape), _const_spec(w_up.shape),
                  _const_spec(w_down.shape), _const_spec((1, D_MODEL))],
        out_specs=row(D_MODEL),
        out_shape=jax.ShapeDtypeStruct((n, D_MODEL), F32),
        compiler_params=pltpu.CompilerParams(dimension_semantics=("parallel",), vmem_limit_bytes=VMEM_LIMIT),
        name="mlp",
    )(x2d, mix_a, o_b, woa, wob, g_ffn.reshape(1, D_MODEL), w_gate.astype(BF16), w_up.astype(BF16),
      w_down.astype(BF16), g_final.reshape(1, D_MODEL))


def kernel(x_prompt, x_sample, cache_k, cache_v, cache_kidx, state_conv, state_ssm, page_table, meta_tokens, norm_mix, w_in, conv_w, a_log, dt_bias, o_norm, w_out, rel_bias, norm_ffn, w_gate, w_up, w_down, norm_final):
    assert w_in.shape[0] == 1, "single-layer stack"
    bsz, seq, _ = x_prompt.shape
    dbs, n_new, _ = x_sample.shape
    t_pad = FRONT + seq
    lo = FRONT - N_META
    past_len = page_table.shape[1] * PAGE
    topk_p = min(TOPK_MAX, seq // 4)
    topk_s = min(TOPK_MAX, (past_len + n_new) // 4)
    w_parts = _split_w_in(w_in[0])
    mlp_w = (w_out[0], norm_ffn[0], w_gate[0], w_up[0], w_down[0], norm_final)
    gdn_w = (conv_w[0], a_log[0], dt_bias[0], o_norm[0])

    meta = jnp.broadcast_to(meta_tokens.astype(F32)[None], (bsz, N_META, D_MODEL))
    xp = jnp.concatenate([jnp.zeros((bsz, lo, D_MODEL), F32), meta, x_prompt], axis=1).reshape(bsz * t_pad, D_MODEL)
    qkva, z, qb, k, v, _, vbf, qi, small, _, small_t, small_tbf, kt = _proj(xp, norm_mix[0], w_parts, 512)
    mix_a, ssm_p = _gdn_prompt(qkva, z, small, small_t, *gdn_w, bsz, t_pad, lo)
    bias_tab = _bias_tab(rel_bias, t_pad // LANES)
    o_b = _dsa_prompt(qb, qi, small, kt, vbf, small_tbf, bias_tab, bsz, t_pad, lo, topk_p)
    y_p = _mlp(x_prompt.reshape(bsz * seq, D_MODEL), mix_a.reshape(bsz * seq, A_DIM), o_b.reshape(bsz * seq, B_DIM),
               *mlp_w, 512).reshape(bsz, seq, D_MODEL)
    real = lambda a: a.reshape(bsz, t_pad, a.shape[-1])[:, lo:]
    k_p = real(k).reshape(1, bsz, N_META + seq, HEADS, HD)
    v_p = real(v).reshape(1, bsz, N_META + seq, HEADS, HD)
    kidx_p = real(small)[:, :, S_KIDX:S_KIDX + HD][None]
    conv_p = real(qkva)[:, -(CONV_WIDTH - 1):][None]

    xs = x_sample.reshape(dbs * n_new, D_MODEL)
    (qkva_s, z_s, qb_s, k_s, v_s, kbf_s, vbf_s, qi_s, small_s, smallbf_s, small_t_s, _, _) = _proj(
        xs, norm_mix[0], w_parts, 512)
    mix_a_s, ssm_s = _gdn_sample(qkva_s, z_s, small_s, small_t_s, state_conv[0], state_ssm[0], *gdn_w, n_new)
    bias_s = _bias_sample(rel_bias, past_len, past_len // PAGE + 1)
    o_b_s = _dsa_sample(qb_s, qi_s, small_s, kbf_s, vbf_s, smallbf_s, cache_k[0], cache_v[0], cache_kidx[0], page_table,
                        bias_s, n_new, topk_s)
    y_s = _mlp(xs, mix_a_s, o_b_s.reshape(dbs * n_new, B_DIM), *mlp_w, 512).reshape(dbs, n_new, D_MODEL)
    k_sn = k_s.reshape(1, dbs, n_new, HEADS, HD)
    v_sn = v_s.reshape(1, dbs, n_new, HEADS, HD)
    kidx_s = small_s.reshape(dbs, n_new, LANES)[:, :, S_KIDX:S_KIDX + HD][None]
    conv_s = jnp.concatenate([state_conv[0].astype(F32), qkva_s.reshape(dbs, n_new, 3 * A_DIM)],
                             axis=1)[:, -(CONV_WIDTH - 1):][None]
    return (y_p, y_s, k_p, v_p, kidx_p, conv_p, ssm_p[None], k_sn, v_sn, kidx_s, conv_s, ssm_s[None])
```

```python
import functools
import math

import jax
import jax.numpy as jnp
from jax import lax
from jax.experimental import pallas as pl
from jax.experimental.pallas import tpu as pltpu

F32 = jnp.float32
BF16 = jnp.bfloat16
I32 = jnp.int32
HI = lax.Precision.HIGHEST

D_MODEL = 1024
N_META = 16
HEADS = 8
HD = 64
A_DIM = HEADS * HD
B_DIM = HEADS * HD
CONV_WIDTH = 4
GDN_CHUNK = 64
TOPK_MAX = 256
PAGE = 128
REL_BUCKETS = 32
REL_MAX_DIST = 1024
D_FF = 2816
EPS = 1e-6
LANES = 128
SUBLANES = 8
FRONT = 128
NEG = -1e30
INT_MIN = -2 ** 31
VMEM_LIMIT = 56 * 1024 * 1024

S_KIDX, S_BETA, S_A, S_W = 0, 64, 72, 80


def _mm(a, b):
    return jnp.dot(a, b, preferred_element_type=F32)


def _mm_hi(a, b):
    return jnp.dot(a, b, preferred_element_type=F32, precision=HI)


def _split_bf16(x):
    hi = x.astype(BF16)
    return hi, (x - hi.astype(F32)).astype(BF16)


def _nt(a, b, precision=None):
    return lax.dot_general(a, b, (((1,), (1,)), ((), ())), preferred_element_type=F32, precision=precision)


def _const_spec(shape):
    nd = len(shape)
    return pl.BlockSpec(shape, lambda *_: (0,) * nd, pipeline_mode=pl.Buffered(1))


def _silu(x):
    return x * jax.nn.sigmoid(x)


def _proj_body(x_ref, g_ref, wa_ref, wz_ref, wb_ref, wqi_ref, ws_ref, wst_ref,
               qkva_ref, z_ref, qb_ref, k_ref, v_ref, kbf_ref, vbf_ref, qi_ref, small_ref, smallbf_ref, smallt_ref,
               smalltbf_ref, kt_ref):
    x = x_ref[...]
    ms = jnp.mean(x * x, axis=-1, keepdims=True)
    h = (x * lax.rsqrt(ms + EPS) * g_ref[...]).astype(BF16)
    qkva_ref[...] = _mm(h, wa_ref[...])
    z_ref[...] = _mm(h, wz_ref[...])
    qkvb = _mm(h, wb_ref[...])
    qb_ref[...] = (qkvb[:, :B_DIM] * 0.125).astype(BF16)
    k = qkvb[:, B_DIM:2 * B_DIM]
    v = qkvb[:, 2 * B_DIM:]
    k_ref[...] = k
    v_ref[...] = v
    kbf_ref[...] = k.astype(BF16)
    vbf_ref[...] = v.astype(BF16)
    qi_ref[...] = (_mm(h, wqi_ref[...]) * 0.125).astype(BF16)
    s = _mm(h, ws_ref[...])
    small_ref[...] = s
    smallbf_ref[...] = s.astype(BF16)
    t = _nt(wst_ref[...], h)
    smallt_ref[...] = t[:LANES]
    smalltbf_ref[...] = t[:LANES].astype(BF16)
    kt_ref[...] = t[LANES:].astype(BF16)


def _split_w_in(w_in):
    c = 0
    parts = []
    for n in (3 * A_DIM, A_DIM, HEADS, HEADS, 3 * B_DIM, HEADS * HD, HD, HEADS):
        parts.append(w_in[:, c:c + n])
        c += n
    w_qkva, w_z, w_beta, w_a, w_qkvb, w_qi, w_kidx, w_w = parts
    pad = jnp.zeros((w_in.shape[0], LANES - (HD + 3 * HEADS)), w_in.dtype)
    w_small = jnp.concatenate([w_kidx, w_beta, w_a, w_w, pad], axis=1)
    bf = lambda t: t.astype(BF16)
    w_t = jnp.concatenate([w_small, w_qkvb[:, B_DIM:2 * B_DIM]], axis=1).T
    return bf(w_qkva), bf(w_z), bf(w_qkvb), bf(w_qi), bf(w_small), bf(w_t)


def _proj(x2d, g, w_parts, tm):
    n = x2d.shape[0]
    assert n % tm == 0 and tm % LANES == 0
    wa, wz, wb, wqi, ws, wst = w_parts
    row = lambda w: pl.BlockSpec((tm, w), lambda i: (i, 0))
    outs = [
        (jax.ShapeDtypeStruct((n, 3 * A_DIM), F32), row(3 * A_DIM)),
        (jax.ShapeDtypeStruct((n, A_DIM), F32), row(A_DIM)),
        (jax.ShapeDtypeStruct((n, B_DIM), BF16), row(B_DIM)),
        (jax.ShapeDtypeStruct((n, B_DIM), F32), row(B_DIM)),
        (jax.ShapeDtypeStruct((n, B_DIM), F32), row(B_DIM)),
        (jax.ShapeDtypeStruct((n, B_DIM), BF16), row(B_DIM)),
        (jax.ShapeDtypeStruct((n, B_DIM), BF16), row(B_DIM)),
        (jax.ShapeDtypeStruct((n, HEADS * HD), BF16), row(HEADS * HD)),
        (jax.ShapeDtypeStruct((n, LANES), F32), row(LANES)),
        (jax.ShapeDtypeStruct((n, LANES), BF16), row(LANES)),
        (jax.ShapeDtypeStruct((LANES, n), F32), pl.BlockSpec((LANES, tm), lambda i: (0, i))),
        (jax.ShapeDtypeStruct((LANES, n), BF16), pl.BlockSpec((LANES, tm), lambda i: (0, i))),
        (jax.ShapeDtypeStruct((B_DIM, n), BF16), pl.BlockSpec((B_DIM, tm), lambda i: (0, i))),
    ]
    return pl.pallas_call(
        _proj_body,
        grid=(n // tm,),
        in_specs=[row(D_MODEL), _const_spec((1, D_MODEL)), _const_spec(wa.shape), _const_spec(wz.shape),
                  _const_spec(wb.shape), _const_spec(wqi.shape), _const_spec(ws.shape), _const_spec(wst.shape)],
        out_specs=[o[1] for o in outs],
        out_shape=[o[0] for o in outs],
        compiler_params=pltpu.CompilerParams(dimension_semantics=("parallel",), vmem_limit_bytes=VMEM_LIMIT),
        name="proj",
    )(x2d, g.reshape(1, D_MODEL), wa, wz, wb, wqi, ws, wst)


def _gdn_prep(qkv, tail, conv_w, small, small_t, a_log_row, dt_row, a_log_col, dt_col, valid_col, valid_row, chunk):
    rows = qkv.shape[0]
    n_chunks = rows // chunk
    ys = []
    for c in range(n_chunks):
        ext = jnp.concatenate([tail[c], qkv[c * chunk:(c + 1) * chunk]], axis=0)
        y = sum(ext[SUBLANES - (CONV_WIDTH - 1) + j: SUBLANES - (CONV_WIDTH - 1) + j + chunk] * conv_w[j:j + 1]
                for j in range(CONV_WIDTH))
        ys.append(_silu(y))
    y = ys[0] if n_chunks == 1 else jnp.concatenate(ys, axis=0)
    beta_c = jnp.where(valid_col, jax.nn.sigmoid(small), 0.0)
    beta_r = jnp.where(valid_row, jax.nn.sigmoid(small_t), 0.0)
    g_c = jnp.where(valid_col, -jnp.exp(a_log_row) * jax.nn.softplus(small + dt_row), 0.0)
    g_r = jnp.where(valid_row, -jnp.exp(a_log_col) * jax.nn.softplus(small_t + dt_col), 0.0)
    ri = lax.broadcasted_iota(I32, (rows, rows), 0)
    ci = lax.broadcasted_iota(I32, (rows, rows), 1)
    same = (ri // chunk) == (ci // chunk)
    low = jnp.where(same & (ci <= ri), 1.0, 0.0).astype(F32)
    b_c = _mm_hi(low, g_c)
    b_r = _nt(g_r, low, precision=HI)
    return y, beta_c, b_c, b_r


def _bdot(a, b, ca, cb):
    return lax.dot_general(a, b, (((ca,), (cb,)), ((0,), (0,))), preferred_element_type=F32)


def _bmm_solve(a, b):
    a_hi, a_lo = _split_bf16(a)
    b_hi, b_lo = _split_bf16(b)
    return _bdot(a_hi, b_hi, 2, 1) + (_bdot(a_hi, b_lo, 2, 1) + _bdot(a_lo, b_hi, 2, 1))


def _gdn_solve(q, k, v, beta, b_c, b_r, chunk):
    c = chunk
    qn = q * lax.rsqrt(jnp.sum(q * q, axis=-1, keepdims=True) + EPS) * (HD ** -0.5)
    kn = k * lax.rsqrt(jnp.sum(k * k, axis=-1, keepdims=True) + EPS)
    ri = lax.broadcasted_iota(I32, (1, c, c), 1)
    ci = lax.broadcasted_iota(I32, (1, c, c), 2)
    incl = ci <= ri
    decay = jnp.where(incl, jnp.exp(jnp.where(incl, b_c - b_r, 0.0)), 0.0)
    kb, qb16 = kn.astype(BF16), qn.astype(BF16)
    kk = _bdot(kb, kb, 2, 2)
    qk = _bdot(qb16, kb, 2, 2)
    bk = -jnp.where(ci < ri, beta * kk * decay, 0.0)
    eb = jnp.exp(b_c)
    y = jnp.concatenate([beta * v, (beta * eb) * kn], axis=2)
    n_levels = int(math.log2(c))
    for lvl in range(n_levels):
        if lvl + 1 < n_levels:
            prod = _bmm_solve(bk, jnp.concatenate([y, bk], axis=2))
            y, bk = y + prod[:, :, :2 * HD], prod[:, :, 2 * HD:]
        else:
            y = y + _bmm_solve(bk, y)
    b_last = b_c[:, c - 1:c, :]
    return (y[:, :, :HD], y[:, :, HD:].astype(BF16), (qk * decay).astype(BF16), qb16,
            (kn * jnp.exp(b_last - b_c)).astype(BF16), eb, jnp.exp(b_last))


def _gdn_apply(u, w, attn, qb16, kd, eb, eb_last, s0, o_norm, z):
    s0b = s0.astype(BF16)
    delta = u - _bdot(w, s0b, 2, 1)
    deltab = delta.astype(BF16)
    o = eb * _bdot(qb16, s0b, 2, 1) + _bdot(attn, deltab, 2, 1)
    s_new = eb_last * s0 + _bdot(kd, deltab, 1, 1)
    on = o * lax.rsqrt(jnp.mean(o * o, axis=-1, keepdims=True) + EPS) * o_norm
    return s_new, on * _silu(z)


def _gdn_prompt_body(*refs, rows, lo, nb):
    qkv_ref, z_ref, small_ref = refs[:3]
    smallt_refs = refs[3:3 + nb]
    convw_ref, alr_ref, dtr_ref, alc_ref, dtc_ref, onorm_ref, mix_ref, s_ref, tail_scr = refs[3 + nb:]
    t = pl.program_id(1)

    @pl.when(t == 0)
    def _():
        s_ref[...] = jnp.zeros_like(s_ref)
        tail_scr[...] = jnp.zeros_like(tail_scr)

    chunk = GDN_CHUNK
    n_chunks = rows // chunk
    pos_c = t * rows + lax.broadcasted_iota(I32, (rows, LANES), 0)
    pos_r = t * rows + lax.broadcasted_iota(I32, (LANES, rows), 1)
    preps = []
    for s in range(nb):
        qkv = qkv_ref[s]
        tails = [tail_scr[s]] + [qkv[c * chunk - SUBLANES:c * chunk] for c in range(1, n_chunks)]
        preps.append(_gdn_prep(qkv, tails, convw_ref[...], small_ref[s], smallt_refs[s][...],
                               alr_ref[...], dtr_ref[...], alc_ref[...], dtc_ref[...],
                               pos_c >= lo, pos_r >= lo, chunk) + (z_ref[s],))
        tail_scr[s] = qkv[rows - SUBLANES:rows]
    units = [(c, s, h) for c in range(n_chunks) for s in range(nb) for h in range(HEADS)]
    rs = lambda c: slice(c * chunk, (c + 1) * chunk)
    stack = lambda k, off: jnp.stack([preps[s][k][rs(c), off + h * HD:off + (h + 1) * HD] for c, s, h in units])
    col = lambda k, off: jnp.stack([preps[s][k][rs(c), off + h:off + h + 1] for c, s, h in units])
    sol = _gdn_solve(stack(0, 0), stack(0, A_DIM), stack(0, 2 * A_DIM), col(1, S_BETA), col(2, S_A),
                     jnp.stack([preps[s][3][S_A + h:S_A + h + 1, rs(c)] for c, s, h in units]), chunk)
    z3 = stack(4, 0)
    o_norm = onorm_ref[...]
    per_chunk = nb * HEADS
    for c in range(n_chunks):
        g = slice(c * per_chunk, (c + 1) * per_chunk)
        s_new, out = _gdn_apply(*[a[g] for a in sol], s_ref[...].reshape(per_chunk, HD, HD), o_norm, z3[g])
        s_ref[...] = s_new.reshape(nb, HEADS, HD, HD)
        for s in range(nb):
            for h in range(HEADS):
                mix_ref[s, rs(c), h * HD:(h + 1) * HD] = out[s * HEADS + h].astype(mix_ref.dtype)


GDN_SEQS_PER_STEP = 2


def _gate_params(a_log, dt_bias):
    row = lambda v: jnp.zeros((1, LANES), F32).at[0, S_A:S_A + HEADS].set(v.astype(F32))
    return row(a_log), row(dt_bias), row(a_log).T, row(dt_bias).T


def _gdn_prompt(qkva, z, small, small_t, conv_w, a_log, dt_bias, o_norm, bsz, t_pad, lo):
    rows = LANES
    nt = t_pad // rows
    nb = GDN_SEQS_PER_STEP if bsz % GDN_SEQS_PER_STEP == 0 else 1
    alr, dtr, alc, dtc = _gate_params(a_log, dt_bias)
    body = functools.partial(_gdn_prompt_body, rows=rows, lo=lo, nb=nb)
    seqs = lambda w: pl.BlockSpec((nb, rows, w), lambda b, t: (b, t, 0))
    mix, s = pl.pallas_call(
        body,
        grid=(bsz // nb, nt),
        in_specs=[seqs(3 * A_DIM), seqs(A_DIM), seqs(LANES)] + [
            pl.BlockSpec((LANES, rows), functools.partial(lambda b, t, s: (0, (b * nb + s) * nt + t), s=s))
            for s in range(nb)] + [
            _const_spec((CONV_WIDTH, 3 * A_DIM)), _const_spec((1, LANES)), _const_spec((1, LANES)),
            _const_spec((LANES, 1)), _const_spec((LANES, 1)), _const_spec((1, HD)),
        ],
        out_specs=[
            pl.BlockSpec((nb, rows, A_DIM), lambda b, t: (b, jnp.maximum(t - 1, 0), 0)),
            pl.BlockSpec((nb, HEADS, HD, HD), lambda b, t: (b, 0, 0, 0)),
        ],
        out_shape=[jax.ShapeDtypeStruct((bsz, t_pad - FRONT, A_DIM), BF16),
                   jax.ShapeDtypeStruct((bsz, HEADS, HD, HD), F32)],
        scratch_shapes=[pltpu.VMEM((nb, SUBLANES, 3 * A_DIM), F32)],
        compiler_params=pltpu.CompilerParams(dimension_semantics=("parallel", "arbitrary"),
                                             vmem_limit_bytes=VMEM_LIMIT),
        name="gdn_prompt",
    )(qkva.reshape(bsz, t_pad, 3 * A_DIM), z.reshape(bsz, t_pad, A_DIM), small.reshape(bsz, t_pad, LANES),
      *([small_t] * nb), conv_w, alr, dtr, alc, dtc, o_norm.reshape(1, HD))
    return mix, s


def _gdn_sample_body(qkv_ref, z_ref, small_ref, smallt_ref, conv0_ref, s0_ref, convw_ref, alr_ref, dtr_ref, alc_ref,
                     dtc_ref, onorm_ref, mix_ref, s_ref, *, chunk):
    rows = LANES
    n_seq = rows // chunk
    qkv = qkv_ref[...]
    tails = [conv0_ref[i] for i in range(n_seq)]
    y, beta_c, b_c, b_r = _gdn_prep(qkv, tails, convw_ref[...], small_ref[...], smallt_ref[...],
                                    alr_ref[...], dtr_ref[...], alc_ref[...], dtc_ref[...], True, True, chunk)
    z = z_ref[...]
    per_seq = lambda a: a.reshape(n_seq, chunk, a.shape[-1])
    stack = lambda a, off: jnp.concatenate([per_seq(a[:, off + h * HD:off + (h + 1) * HD]) for h in range(HEADS)])
    col = lambda a, off: jnp.concatenate([per_seq(a[:, off + h:off + h + 1]) for h in range(HEADS)])
    b_r3 = jnp.stack([b_r[S_A + h:S_A + h + 1, i * chunk:(i + 1) * chunk] for h in range(HEADS) for i in range(n_seq)])
    sol = _gdn_solve(stack(y, 0), stack(y, A_DIM), stack(y, 2 * A_DIM), col(beta_c, S_BETA), col(b_c, S_A), b_r3, chunk)
    s0 = jnp.concatenate([s0_ref[:, h] for h in range(HEADS)])
    s_new, out = _gdn_apply(*sol, s0, onorm_ref[...], stack(z, 0))
    for h in range(HEADS):
        g = slice(h * n_seq, (h + 1) * n_seq)
        s_ref[:, h] = s_new[g]
        mix_ref[:, h * HD:(h + 1) * HD] = out[g].reshape(rows, HD).astype(mix_ref.dtype)


def _gdn_sample(qkva, z, small, small_t, state_conv, state_ssm, conv_w, a_log, dt_bias, o_norm, n_new):
    n = qkva.shape[0]
    bsz = n // n_new
    rows = LANES
    n_seq = rows // n_new
    assert n_new == SUBLANES and n % rows == 0
    alr, dtr, alc, dtc = _gate_params(a_log, dt_bias)
    conv0 = jnp.pad(state_conv.astype(F32), ((0, 0), (SUBLANES - (CONV_WIDTH - 1), 0), (0, 0)))
    body = functools.partial(_gdn_sample_body, chunk=n_new)
    mix, s = pl.pallas_call(
        body,
        grid=(n // rows,),
        in_specs=[
            pl.BlockSpec((rows, 3 * A_DIM), lambda g: (g, 0)),
            pl.BlockSpec((rows, A_DIM), lambda g: (g, 0)),
            pl.BlockSpec((rows, LANES), lambda g: (g, 0)),
            pl.BlockSpec((LANES, rows), lambda g: (0, g)),
            pl.BlockSpec((n_seq, SUBLANES, 3 * A_DIM), lambda g: (g, 0, 0)),
            pl.BlockSpec((n_seq, HEADS, HD, HD), lambda g: (g, 0, 0, 0)),
            _const_spec((CONV_WIDTH, 3 * A_DIM)), _const_spec((1, LANES)), _const_spec((1, LANES)),
            _const_spec((LANES, 1)), _const_spec((LANES, 1)), _const_spec((1, HD)),
        ],
        out_specs=[
            pl.BlockSpec((rows, A_DIM), lambda g: (g, 0)),
            pl.BlockSpec((n_seq, HEADS, HD, HD), lambda g: (g, 0, 0, 0)),
        ],
        out_shape=[jax.ShapeDtypeStruct((n, A_DIM), BF16), jax.ShapeDtypeStruct((bsz, HEADS, HD, HD), F32)],
        compiler_params=pltpu.CompilerParams(dimension_semantics=("parallel",), vmem_limit_bytes=VMEM_LIMIT),
        name="gdn_sample",
    )(qkva, z, small, small_t, conv0, state_ssm.astype(F32), conv_w, alr, dtr, alc, dtc, o_norm.reshape(1, HD))
    return mix, s


def _rel_bucket(d):
    d = jnp.maximum(d, 0)
    max_exact = REL_BUCKETS // 2
    df = jnp.maximum(d, 1).astype(F32)
    large = max_exact + (jnp.log(df / max_exact) / math.log(REL_MAX_DIST / max_exact)
                         * (REL_BUCKETS - max_exact)).astype(I32)
    large = jnp.minimum(large, REL_BUCKETS - 1)
    return jnp.where(d < max_exact, d, large)


def _bias_lookup(bucket, rb_ref, h):
    acc = jnp.zeros(bucket.shape, F32)
    for b in range(REL_BUCKETS):
        acc = jnp.where(bucket == b, rb_ref[b, h], acc)
    return acc


def _bias_tab_body(rb_ref, out_ref):
    dlt = pl.program_id(0)
    i = lax.broadcasted_iota(I32, (LANES, LANES), 0)
    j = lax.broadcasted_iota(I32, (LANES, LANES), 1)
    bucket = _rel_bucket(dlt * LANES + i - j)
    for h in range(HEADS):
        out_ref[h, 0] = _bias_lookup(bucket, rb_ref, h)


def _bias_tab(rel_bias, n_tiles):
    return pl.pallas_call(
        _bias_tab_body,
        grid=(n_tiles,),
        in_specs=[pl.BlockSpec(memory_space=pltpu.SMEM)],
        out_specs=pl.BlockSpec((HEADS, 1, LANES, LANES), lambda t: (0, t, 0, 0)),
        out_shape=jax.ShapeDtypeStruct((HEADS, n_tiles, LANES, LANES), F32),
        name="bias_tab",
    )(rel_bias.astype(F32))


def _bias_sample_body(rb_ref, out_ref, *, past_len):
    blk = pl.program_id(0)
    q = lax.broadcasted_iota(I32, (SUBLANES, LANES), 0)
    s = blk * LANES + lax.broadcasted_iota(I32, (SUBLANES, LANES), 1)
    bucket = _rel_bucket(past_len + q - s)
    for h in range(HEADS):
        out_ref[h * SUBLANES:(h + 1) * SUBLANES, :] = _bias_lookup(bucket, rb_ref, h)


def _bias_sample(rel_bias, past_len, n_blocks):
    return pl.pallas_call(
        functools.partial(_bias_sample_body, past_len=past_len),
        grid=(n_blocks,),
        in_specs=[pl.BlockSpec(memory_space=pltpu.SMEM)],
        out_specs=pl.BlockSpec((HEADS * SUBLANES, LANES), lambda t: (0, t)),
        out_shape=jax.ShapeDtypeStruct((HEADS * SUBLANES, n_blocks * LANES), F32),
        name="bias_sample",
    )(rel_bias.astype(F32))


def _sort_key(score, admissible):
    score = jnp.where(score == 0.0, 0.0, score)
    bits = pltpu.bitcast(score, I32)
    key = jnp.where(bits < 0, bits ^ 0x7FFFFFFF, bits)
    return jnp.where(admissible, key, INT_MIN)


def _kth_largest_key(count_ge, rows, topk):
    c0 = jnp.broadcast_to(count_ge(jnp.zeros((rows, LANES), I32)), (rows, LANES))
    t0 = jnp.where(c0 >= topk, 0, INT_MIN).astype(I32)
    c0 = jnp.where(c0 >= topk, c0, float(2 ** 30))

    def bit_step(i, carry):
        t, c = carry
        cand = t + lax.shift_left(jnp.int32(1), 30 - i)
        cc = jnp.broadcast_to(count_ge(cand), (rows, LANES))
        ok = cc >= topk
        return jnp.where(ok, cand, t), jnp.where(ok, cc, c)

    return lax.fori_loop(0, 31, bit_step, (t0, c0))


def _tie_mask(key, thr, need, running, tri):
    eq = (key == thr) & (key != INT_MIN)
    pref = _mm(jnp.where(eq, 1.0, 0.0).astype(BF16), tri) + running
    sel = (key > thr) | (eq & (pref <= need))
    return sel, jnp.broadcast_to(pref[:, LANES - 1:LANES], pref.shape)


def _tri_incl():
    r = lax.broadcasted_iota(I32, (LANES, LANES), 0)
    c = lax.broadcasted_iota(I32, (LANES, LANES), 1)
    return jnp.where(r <= c, 1.0, 0.0).astype(BF16)


def _dsa_prompt_tiles(first, n_tiles, qi, w_raw, kt_ref, kidxt_ref, bias_ref, out_ref,
                      qh_scr, vh_scr, keys_scr, mbias_scr, o_scr, *, lo, topk):
    nblk = first + n_tiles
    tq = n_tiles * LANES
    wid = nblk * LANES
    t_pos = first * LANES + lax.broadcasted_iota(I32, (tq, wid), 0)
    s_pos = lax.broadcasted_iota(I32, (tq, wid), 1)
    w = w_raw * (HEADS ** -0.5)
    kxt = kidxt_ref[S_KIDX:S_KIDX + HD, 0:wid]
    acc = jnp.zeros((tq, wid), F32)
    for h in range(HEADS):
        acc = acc + jnp.maximum(_mm(qi[:, h * HD:(h + 1) * HD], kxt), 0.0) * w[:, h:h + 1]
    keys_scr[:, 0:wid] = _sort_key(acc, (s_pos <= t_pos) & (s_pos >= lo))

    def count_ge(cand):
        return jnp.sum(jnp.where(keys_scr[:, 0:wid] >= cand[:, 0:1], 1.0, 0.0), axis=1, keepdims=True)

    thr, cnt = _kth_largest_key(count_ge, tq, topk)
    keys = keys_scr[:, 0:wid]
    mbias_scr[:, 0:wid] = jnp.where(keys >= jnp.maximum(thr[:, 0:1], INT_MIN + 1), 0.0, NEG)

    @pl.when(jnp.max(jnp.where(cnt < float(2 ** 30), cnt, 0.0)) > topk)
    def _():
        need = topk - jnp.sum(jnp.where(keys_scr[:, 0:wid] > thr[:, 0:1], 1.0, 0.0), axis=1, keepdims=True)
        tri = _tri_incl()
        running = jnp.zeros((tq, LANES), F32)
        for jb in range(nblk):
            cols = slice(jb * LANES, (jb + 1) * LANES)
            sel, running = _tie_mask(keys_scr[:, cols], thr, need, running, tri)
            mbias_scr[:, cols] = jnp.where(sel, 0.0, NEG)

    def head_attn(h, carry):
        bias = jnp.concatenate(
            [jnp.concatenate([bias_ref[h, max(first + r - jb, 0)] for jb in range(nblk)], axis=1)
             for r in range(n_tiles)], axis=0)
        kt = kt_ref[pl.ds(pl.multiple_of(h * HD, HD), HD), 0:wid]
        lg = _mm(qh_scr[h], kt) + bias + mbias_scr[:, 0:wid]
        p = jnp.exp(lg - jnp.max(lg, axis=1, keepdims=True))
        inv = 1.0 / jnp.sum(p, axis=1, keepdims=True)
        o_scr[h] = _mm(p.astype(BF16), vh_scr[h, 0:wid, :]) * inv
        return carry

    lax.fori_loop(0, HEADS, head_attn, 0)
    for h in range(HEADS):
        out_ref[0, :, h * HD:(h + 1) * HD] = o_scr[h].astype(out_ref.dtype)


def _dsa_prompt_body(*refs, lo, topk, n_tiles, n_steps):
    qb_refs, qi_refs, small_refs = refs[:n_tiles], refs[n_tiles:2 * n_tiles], refs[2 * n_tiles:3 * n_tiles]
    (kt_ref, vbf_ref, kidxt_ref, bias_ref, out_ref, qh_scr, vh_scr, keys_scr, mbias_scr, o_scr) = refs[3 * n_tiles:]
    js = pl.program_id(1)

    @pl.when(js == 0)
    def _():
        for h in range(HEADS):
            vh_scr[h] = vbf_ref[0, :, h * HD:(h + 1) * HD]

    for r in range(n_tiles):
        for h in range(HEADS):
            qh_scr[h, r * LANES:(r + 1) * LANES, :] = qb_refs[r][0, :, h * HD:(h + 1) * HD]
    qi = jnp.concatenate([ref[0] for ref in qi_refs], axis=0)
    w_raw = jnp.concatenate([ref[0][:, S_W:S_W + HEADS] for ref in small_refs], axis=0)
    for step in range(n_steps):
        @pl.when(js == step)
        def _(step=step):
            _dsa_prompt_tiles(1 + step * n_tiles, n_tiles, qi, w_raw, kt_ref, kidxt_ref, bias_ref, out_ref,
                              qh_scr, vh_scr, keys_scr, mbias_scr, o_scr, lo=lo, topk=topk)


DSA_TILES_PER_STEP = 4


def _dsa_prompt(qb, qi, small, kt, vbf, smalltbf, bias_tab, bsz, t_pad, lo, topk):
    n_real = t_pad // LANES - 1
    n_tiles = max(d for d in (1, 2, DSA_TILES_PER_STEP) if n_real % d == 0)
    n_steps = n_real // n_tiles
    rows = n_tiles * LANES
    r3 = lambda a: a.reshape(bsz, t_pad, a.shape[-1])
    tiles = lambda w: [pl.BlockSpec((1, LANES, w), functools.partial(lambda b, js, r: (b, 1 + js * n_tiles + r, 0), r=r))
                       for r in range(n_tiles)]
    seq_cols = lambda r: pl.BlockSpec((r, t_pad), lambda b, js: (0, b))
    return pl.pallas_call(
        functools.partial(_dsa_prompt_body, lo=lo, topk=topk, n_tiles=n_tiles, n_steps=n_steps),
        grid=(bsz, n_steps),
        in_specs=tiles(B_DIM) + tiles(HEADS * HD) + tiles(LANES) + [
            seq_cols(B_DIM), pl.BlockSpec((1, t_pad, B_DIM), lambda b, js: (b, 0, 0)), seq_cols(LANES),
            _const_spec(bias_tab.shape)],
        out_specs=pl.BlockSpec((1, rows, B_DIM), lambda b, js: (b, js, 0)),
        out_shape=jax.ShapeDtypeStruct((bsz, t_pad - FRONT, B_DIM), BF16),
        scratch_shapes=[pltpu.VMEM((HEADS, rows, HD), BF16), pltpu.VMEM((HEADS, t_pad, HD), BF16),
                        pltpu.VMEM((rows, t_pad), I32), pltpu.VMEM((rows, t_pad), F32),
                        pltpu.VMEM((HEADS, rows, HD), F32)],
        compiler_params=pltpu.CompilerParams(dimension_semantics=("parallel", "arbitrary"),
                                             vmem_limit_bytes=VMEM_LIMIT),
        name="dsa_prompt",
    )(*([r3(qb)] * n_tiles), *([r3(qi)] * n_tiles), *([r3(small)] * n_tiles), kt, r3(vbf), smalltbf, bias_tab)


PAGES_PER_STEP = 8


def _dsa_sample_body(pt_ref, *refs, n_new, past_len, topk):
    npp = PAGES_PER_STEP
    kidx_refs, k_refs, v_refs = refs[:npp], refs[npp:2 * npp], refs[2 * npp:3 * npp]
    (qi_ref, qbd_ref, w_ref, knew_ref, vnew_ref, kidxnew_ref, bias_ref, out_ref,
     keys_scr, mb_scr, logit_scr, vt_scr) = refs[3 * npp:]
    del pt_ref
    s = pl.program_id(1)
    n_page_steps = past_len // (PAGE * npp)
    n_blocks = past_len // PAGE + 1
    qi = qi_ref[0]
    qbd = qbd_ref[0]
    w = w_ref[0] * (HEADS ** -0.5)
    q_row = lax.broadcasted_iota(I32, (n_new, LANES), 0)
    lane = lax.broadcasted_iota(I32, (n_new, LANES), 1)

    def do_block(off, kxt, kt, vt, admissible):
        rel = jnp.maximum(_mm(qi, kxt), 0.0) * w
        sc = rel[0:n_new]
        for h in range(1, HEADS):
            sc = sc + rel[h * n_new:(h + 1) * n_new]
        keys_scr[:, pl.ds(off, LANES)] = _sort_key(sc, admissible)
        logit_scr[:, pl.ds(off, LANES)] = _mm(qbd, kt) + bias_ref[:, pl.ds(off, LANES)]
        vt_scr[:, pl.ds(off, LANES)] = vt

    @pl.when(s < n_page_steps)
    def _():
        for i in range(npp):
            off = pl.multiple_of((s * npp + i) * PAGE, PAGE)
            do_block(off, kidx_refs[i][0].astype(BF16), k_refs[i][0].reshape(B_DIM, PAGE).astype(BF16),
                     v_refs[i][0].reshape(B_DIM, PAGE).astype(BF16), True)

    @pl.when(s == n_page_steps)
    def _():
        tr = lambda a: jnp.concatenate(
            [a.astype(F32), jnp.zeros((PAGE - n_new, a.shape[1]), F32)], axis=0).T.astype(BF16)
        do_block(past_len, tr(kidxnew_ref[0])[S_KIDX:S_KIDX + HD], tr(knew_ref[0]), tr(vnew_ref[0]), lane <= q_row)

        keys = keys_scr[...]
        thr, cnt = _kth_largest_key(
            lambda cand: jnp.sum(jnp.where(keys >= cand[:, 0:1], 1.0, 0.0), axis=1, keepdims=True), n_new, topk)
        mb_scr[...] = jnp.where(keys >= jnp.maximum(thr[:, 0:1], INT_MIN + 1), 0.0, NEG)

        @pl.when(jnp.max(jnp.where(cnt < float(2 ** 30), cnt, 0.0)) > topk)
        def _():
            need = topk - jnp.sum(jnp.where(keys > thr[:, 0:1], 1.0, 0.0), axis=1, keepdims=True)
            tri = _tri_incl()

            def mask_blk(jb, running):
                cols = pl.ds(pl.multiple_of(jb * LANES, LANES), LANES)
                sel, running = _tie_mask(keys_scr[:, cols], thr, need, running, tri)
                mb_scr[:, cols] = jnp.where(sel, 0.0, NEG)
                return running

            lax.fori_loop(0, n_blocks, mask_blk, jnp.zeros((n_new, LANES), F32))

        lg = logit_scr[...] + jnp.concatenate([mb_scr[...]] * HEADS, axis=0)
        m = jnp.max(lg, axis=1, keepdims=True)
        p = jnp.exp(lg - m)
        pb = (p * (1.0 / jnp.sum(p, axis=1, keepdims=True))).astype(BF16)
        out_ref[0] = _nt(vt_scr[...], pb)


def _dsa_sample(qb, qi, small, kbf, vbf, smallbf, cache_k, cache_v, cache_kidx, page_table, bias_s, n_new, topk):
    dbs, n_pages = page_table.shape
    past_len = n_pages * PAGE
    npp = PAGES_PER_STEP
    assert n_pages % npp == 0 and n_new == SUBLANES
    n_page_steps = n_pages // npp
    n_blocks = n_pages + 1
    n_pool = cache_k.shape[0]
    qi_rows = qi.reshape(dbs, n_new, HEADS, HD).transpose(0, 2, 1, 3).reshape(dbs, HEADS * n_new, HD)
    q4 = qb.reshape(dbs, n_new, HEADS, HD).transpose(0, 2, 1, 3)
    eye = jnp.eye(HEADS, dtype=qb.dtype)
    qbd = (q4[:, :, :, None, :] * eye[None, :, None, :, None]).reshape(dbs, HEADS * n_new, B_DIM)
    w_rows = small.reshape(dbs, n_new, LANES)[:, :, S_W:S_W + HEADS].transpose(0, 2, 1).reshape(dbs, HEADS * n_new, 1)
    w_rows = jnp.broadcast_to(w_rows, (dbs, HEADS * n_new, LANES))

    def page_spec(shape, i):
        def imap(b, s, pt):
            return (pt[b, jnp.minimum(s, n_page_steps - 1) * npp + i],) + (0,) * len(shape)
        return pl.BlockSpec((1,) + shape, imap)

    per_seq = lambda r, w: pl.BlockSpec((1, r, w), lambda b, s, pt: (b, 0, 0))
    in_specs = ([page_spec((HD, PAGE), i) for i in range(npp)] + [page_spec((HEADS, HD, PAGE), i) for i in range(npp)]
                + [page_spec((HEADS, HD, PAGE), i) for i in range(npp)]
                + [per_seq(HEADS * n_new, HD), per_seq(HEADS * n_new, B_DIM), per_seq(HEADS * n_new, LANES),
                   per_seq(n_new, B_DIM), per_seq(n_new, B_DIM), per_seq(n_new, LANES),
                   pl.BlockSpec(bias_s.shape, lambda b, s, pt: (0, 0), pipeline_mode=pl.Buffered(1))])
    grid_spec = pltpu.PrefetchScalarGridSpec(
        num_scalar_prefetch=1,
        grid=(dbs, n_page_steps + 1),
        in_specs=in_specs,
        out_specs=pl.BlockSpec((1, B_DIM, HEADS * n_new), lambda b, s, pt: (b, 0, 0)),
        scratch_shapes=[pltpu.VMEM((n_new, n_blocks * LANES), I32), pltpu.VMEM((n_new, n_blocks * LANES), F32),
                        pltpu.VMEM((HEADS * n_new, n_blocks * LANES), F32),
                        pltpu.VMEM((B_DIM, n_blocks * LANES), BF16)],
    )
    ck = cache_k.transpose(0, 2, 3, 1)
    cv = cache_v.transpose(0, 2, 3, 1)
    cki = cache_kidx.transpose(0, 2, 1)
    r3 = lambda a: a.reshape(dbs, n_new, a.shape[-1])
    out_t = pl.pallas_call(
        functools.partial(_dsa_sample_body, n_new=n_new, past_len=past_len, topk=topk),
        grid_spec=grid_spec,
        out_shape=jax.ShapeDtypeStruct((dbs, B_DIM, HEADS * n_new), F32),
        compiler_params=pltpu.CompilerParams(dimension_semantics=("parallel", "arbitrary"),
                                             vmem_limit_bytes=VMEM_LIMIT),
        name="dsa_sample",
    )(page_table, *([cki] * npp), *([ck] * npp), *([cv] * npp),
      qi_rows, qbd, w_rows, r3(kbf), r3(vbf), r3(smallbf), bias_s)
    o5 = out_t.reshape(dbs, HEADS, HD, HEADS, n_new)
    o = jnp.stack([o5[:, h, :, h, :] for h in range(HEADS)], axis=1)
    return o.transpose(0, 3, 1, 2).reshape(dbs, n_new, B_DIM).astype(BF16)


def _rms(x, g):
    return x * lax.rsqrt(jnp.mean(x * x, axis=-1, keepdims=True) + EPS) * g


def _mlp_body(x_ref, ma_ref, ob_ref, woa_ref, wob_ref, gffn_ref, wg_ref, wu_ref, wd_ref, gfin_ref, y_ref):
    h = x_ref[...] + (_mm(ma_ref[...], woa_ref[...]) + _mm(ob_ref[...], wob_ref[...]))
    hn = _rms(h, gffn_ref[...]).astype(BF16)
    act = _silu(_mm(hn, wg_ref[...])) * _mm(hn, wu_ref[...])
    y_ref[...] = _rms(h + _mm(act.astype(BF16), wd_ref[...]), gfin_ref[...])


def _mlp(x2d, mix_a, o_b, w_out, g_ffn, w_gate, w_up, w_down, g_final, tm):
    n = x2d.shape[0]
    assert n % tm == 0
    row = lambda w: pl.BlockSpec((tm, w), lambda i: (i, 0))
    woa, wob = w_out[:A_DIM].astype(BF16), w_out[A_DIM:].astype(BF16)
    return pl.pallas_call(
        _mlp_body,
        grid=(n // tm,),
        in_specs=[row(D_MODEL), row(A_DIM), row(B_DIM), _const_spec(woa.shape), _const_spec(wob.shape),
                  _const_spec((1, D_MODEL)), _const_spec(w_gate.shape), _const_spec(w_up.shape),
                  _const_spec(w_down.shape), _const_spec((1, D_MODEL))],
        out_specs=row(D_MODEL),
        out_shape=jax.ShapeDtypeStruct((n, D_MODEL), F32),
        compiler_params=pltpu.CompilerParams(dimension_semantics=("parallel",), vmem_limit_bytes=VMEM_LIMIT),
        name="mlp",
    )(x2d, mix_a, o_b, woa, wob, g_ffn.reshape(1, D_MODEL), w_gate.astype(BF16), w_up.astype(BF16),
      w_down.astype(BF16), g_final.reshape(1, D_MODEL))


def kernel(x_prompt, x_sample, cache_k, cache_v, cache_kidx, state_conv, state_ssm, page_table, meta_tokens, norm_mix, w_in, conv_w, a_log, dt_bias, o_norm, w_out, rel_bias, norm_ffn, w_gate, w_up, w_down, norm_final):
    assert w_in.shape[0] == 1, "single-layer stack"
    bsz, seq, _ = x_prompt.shape
    dbs, n_new, _ = x_sample.shape
    t_pad = FRONT + seq
    lo = FRONT - N_META
    past_len = page_table.shape[1] * PAGE
    topk_p = min(TOPK_MAX, seq // 4)
    topk_s = min(TOPK_MAX, (past_len + n_new) // 4)
    w_parts = _split_w_in(w_in[0])
    mlp_w = (w_out[0], norm_ffn[0], w_gate[0], w_up[0], w_down[0], norm_final)
    gdn_w = (conv_w[0], a_log[0], dt_bias[0], o_norm[0])

    meta = jnp.broadcast_to(meta_tokens.astype(F32)[None], (bsz, N_META, D_MODEL))
    xp = jnp.concatenate([jnp.zeros((bsz, lo, D_MODEL), F32), meta, x_prompt], axis=1).reshape(bsz * t_pad, D_MODEL)
    qkva, z, qb, k, v, _, vbf, qi, small, _, small_t, small_tbf, kt = _proj(xp, norm_mix[0], w_parts, 512)
    mix_a, ssm_p = _gdn_prompt(qkva, z, small, small_t, *gdn_w, bsz, t_pad, lo)
    bias_tab = _bias_tab(rel_bias, t_pad // LANES)
    o_b = _dsa_prompt(qb, qi, small, kt, vbf, small_tbf, bias_tab, bsz, t_pad, lo, topk_p)
    y_p = _mlp(x_prompt.reshape(bsz * seq, D_MODEL), mix_a.reshape(bsz * seq, A_DIM), o_b.reshape(bsz * seq, B_DIM),
               *mlp_w, 512).reshape(bsz, seq, D_MODEL)
    real = lambda a: a.reshape(bsz, t_pad, a.shape[-1])[:, lo:]
    k_p = real(k).reshape(1, bsz, N_META + seq, HEADS, HD)
    v_p = real(v).reshape(1, bsz, N_META + seq, HEADS, HD)
    kidx_p = real(small)[:, :, S_KIDX:S_KIDX + HD][None]
    conv_p = real(qkva)[:, -(CONV_WIDTH - 1):][None]

    xs = x_sample.reshape(dbs * n_new, D_MODEL)
    (qkva_s, z_s, qb_s, k_s, v_s, kbf_s, vbf_s, qi_s, small_s, smallbf_s, small_t_s, _, _) = _proj(
        xs, norm_mix[0], w_parts, 512)
    mix_a_s, ssm_s = _gdn_sample(qkva_s, z_s, small_s, small_t_s, state_conv[0], state_ssm[0], *gdn_w, n_new)
    bias_s = _bias_sample(rel_bias, past_len, past_len // PAGE + 1)
    o_b_s = _dsa_sample(qb_s, qi_s, small_s, kbf_s, vbf_s, smallbf_s, cache_k[0], cache_v[0], cache_kidx[0], page_table,
                        bias_s, n_new, topk_s)
    y_s = _mlp(xs, mix_a_s, o_b_s.reshape(dbs * n_new, B_DIM), *mlp_w, 512).reshape(dbs, n_new, D_MODEL)
    k_sn = k_s.reshape(1, dbs, n_new, HEADS, HD)
    v_sn = v_s.reshape(1, dbs, n_new, HEADS, HD)
    kidx_s = small_s.reshape(dbs, n_new, LANES)[:, :, S_KIDX:S_KIDX + HD][None]
    conv_s = jnp.concatenate([state_conv[0].astype(F32), qkva_s.reshape(dbs, n_new, 3 * A_DIM)],
                             axis=1)[:, -(CONV_WIDTH - 1):][None]
    return (y_p, y_s, k_p, v_p, kidx_p, conv_p, ssm_p[None], k_sn, v_sn, kidx_s, conv_s, ssm_s[None])
```

```python
import functools
import math

import jax
import jax.numpy as jnp
from jax import lax
from jax.experimental import pallas as pl
from jax.experimental.pallas import tpu as pltpu

F32 = jnp.float32
BF16 = jnp.bfloat16
I32 = jnp.int32
HI = lax.Precision.HIGHEST

D_MODEL = 1024
N_META = 16
HEADS = 8
HD = 64
A_DIM = HEADS * HD
B_DIM = HEADS * HD
CONV_WIDTH = 4
GDN_CHUNK = 64
TOPK_MAX = 256
PAGE = 128
REL_BUCKETS = 32
REL_MAX_DIST = 1024
D_FF = 2816
EPS = 1e-6
LANES = 128
SUBLANES = 8
FRONT = 128
NEG = -1e30
INT_MIN = -2 ** 31
VMEM_LIMIT = 56 * 1024 * 1024

S_KIDX, S_BETA, S_A, S_W = 0, 64, 72, 80


def _mm(a, b):
    return jnp.dot(a, b, preferred_element_type=F32)


def _mm_hi(a, b):
    return jnp.dot(a, b, preferred_element_type=F32, precision=HI)


def _split_bf16(x):
    hi = x.astype(BF16)
    return hi, (x - hi.astype(F32)).astype(BF16)


def _nt(a, b, precision=None):
    return lax.dot_general(a, b, (((1,), (1,)), ((), ())), preferred_element_type=F32, precision=precision)


def _const_spec(shape):
    nd = len(shape)
    return pl.BlockSpec(shape, lambda *_: (0,) * nd, pipeline_mode=pl.Buffered(1))


def _silu(x):
    return x * jax.nn.sigmoid(x)


def _proj_body(x_ref, g_ref, wa_ref, wz_ref, wb_ref, wqi_ref, ws_ref, wst_ref,
               qkva_ref, z_ref, qb_ref, k_ref, v_ref, kbf_ref, vbf_ref, qi_ref, small_ref, smallbf_ref, smallt_ref,
               smalltbf_ref, kt_ref):
    x = x_ref[...]
    ms = jnp.mean(x * x, axis=-1, keepdims=True)
    h = (x * lax.rsqrt(ms + EPS) * g_ref[...]).astype(BF16)
    qkva_ref[...] = _mm(h, wa_ref[...])
    z_ref[...] = _mm(h, wz_ref[...])
    qkvb = _mm(h, wb_ref[...])
    qb_ref[...] = (qkvb[:, :B_DIM] * 0.125).astype(BF16)
    k = qkvb[:, B_DIM:2 * B_DIM]
    v = qkvb[:, 2 * B_DIM:]
    k_ref[...] = k
    v_ref[...] = v
    kbf_ref[...] = k.astype(BF16)
    vbf_ref[...] = v.astype(BF16)
    qi_ref[...] = (_mm(h, wqi_ref[...]) * 0.125).astype(BF16)
    s = _mm(h, ws_ref[...])
    small_ref[...] = s
    smallbf_ref[...] = s.astype(BF16)
    t = _nt(wst_ref[...], h)
    smallt_ref[...] = t[:LANES]
    smalltbf_ref[...] = t[:LANES].astype(BF16)
    kt_ref[...] = t[LANES:].astype(BF16)


def _split_w_in(w_in):
    c = 0
    parts = []
    for n in (3 * A_DIM, A_DIM, HEADS, HEADS, 3 * B_DIM, HEADS * HD, HD, HEADS):
        parts.append(w_in[:, c:c + n])
        c += n
    w_qkva, w_z, w_beta, w_a, w_qkvb, w_qi, w_kidx, w_w = parts
    pad = jnp.zeros((w_in.shape[0], LANES - (HD + 3 * HEADS)), w_in.dtype)
    w_small = jnp.concatenate([w_kidx, w_beta, w_a, w_w, pad], axis=1)
    bf = lambda t: t.astype(BF16)
    w_t = jnp.concatenate([w_small, w_qkvb[:, B_DIM:2 * B_DIM]], axis=1).T
    return bf(w_qkva), bf(w_z), bf(w_qkvb), bf(w_qi), bf(w_small), bf(w_t)


def _proj(x2d, g, w_parts, tm):
    n = x2d.shape[0]
    assert n % tm == 0 and tm % LANES == 0
    wa, wz, wb, wqi, ws, wst = w_parts
    row = lambda w: pl.BlockSpec((tm, w), lambda i: (i, 0))
    outs = [
        (jax.ShapeDtypeStruct((n, 3 * A_DIM), F32), row(3 * A_DIM)),
        (jax.ShapeDtypeStruct((n, A_DIM), F32), row(A_DIM)),
        (jax.ShapeDtypeStruct((n, B_DIM), BF16), row(B_DIM)),
        (jax.ShapeDtypeStruct((n, B_DIM), F32), row(B_DIM)),
        (jax.ShapeDtypeStruct((n, B_DIM), F32), row(B_DIM)),
        (jax.ShapeDtypeStruct((n, B_DIM), BF16), row(B_DIM)),
        (jax.ShapeDtypeStruct((n, B_DIM), BF16), row(B_DIM)),
        (jax.ShapeDtypeStruct((n, HEADS * HD), BF16), row(HEADS * HD)),
        (jax.ShapeDtypeStruct((n, LANES), F32), row(LANES)),
        (jax.ShapeDtypeStruct((n, LANES), BF16), row(LANES)),
        (jax.ShapeDtypeStruct((LANES, n), F32), pl.BlockSpec((LANES, tm), lambda i: (0, i))),
        (jax.ShapeDtypeStruct((LANES, n), BF16), pl.BlockSpec((LANES, tm), lambda i: (0, i))),
        (jax.ShapeDtypeStruct((B_DIM, n), BF16), pl.BlockSpec((B_DIM, tm), lambda i: (0, i))),
    ]
    return pl.pallas_call(
        _proj_body,
        grid=(n // tm,),
        in_specs=[row(D_MODEL), _const_spec((1, D_MODEL)), _const_spec(wa.shape), _const_spec(wz.shape),
                  _const_spec(wb.shape), _const_spec(wqi.shape), _const_spec(ws.shape), _const_spec(wst.shape)],
        out_specs=[o[1] for o in outs],
        out_shape=[o[0] for o in outs],
        compiler_params=pltpu.CompilerParams(dimension_semantics=("parallel",), vmem_limit_bytes=VMEM_LIMIT),
        name="proj",
    )(x2d, g.reshape(1, D_MODEL), wa, wz, wb, wqi, ws, wst)


def _gdn_prep(qkv, tail, conv_w, small, small_t, a_log_row, dt_row, a_log_col, dt_col, valid_col, valid_row, chunk):
    rows = qkv.shape[0]
    n_chunks = rows // chunk
    ys = []
    for c in range(n_chunks):
        ext = jnp.concatenate([tail[c], qkv[c * chunk:(c + 1) * chunk]], axis=0)
        y = sum(ext[SUBLANES - (CONV_WIDTH - 1) + j: SUBLANES - (CONV_WIDTH - 1) + j + chunk] * conv_w[j:j + 1]
                for j in range(CONV_WIDTH))
        ys.append(_silu(y))
    y = ys[0] if n_chunks == 1 else jnp.concatenate(ys, axis=0)
    beta_c = jnp.where(valid_col, jax.nn.sigmoid(small), 0.0)
    beta_r = jnp.where(valid_row, jax.nn.sigmoid(small_t), 0.0)
    g_c = jnp.where(valid_col, -jnp.exp(a_log_row) * jax.nn.softplus(small + dt_row), 0.0)
    g_r = jnp.where(valid_row, -jnp.exp(a_log_col) * jax.nn.softplus(small_t + dt_col), 0.0)
    ri = lax.broadcasted_iota(I32, (rows, rows), 0)
    ci = lax.broadcasted_iota(I32, (rows, rows), 1)
    same = (ri // chunk) == (ci // chunk)
    low = jnp.where(same & (ci <= ri), 1.0, 0.0).astype(F32)
    b_c = _mm_hi(low, g_c)
    b_r = _nt(g_r, low, precision=HI)
    return y, beta_c, b_c, b_r


def _bdot(a, b, ca, cb):
    return lax.dot_general(a, b, (((ca,), (cb,)), ((0,), (0,))), preferred_element_type=F32)


def _bmm_solve(a, b):
    a_hi, a_lo = _split_bf16(a)
    b_hi, b_lo = _split_bf16(b)
    return _bdot(a_hi, b_hi, 2, 1) + (_bdot(a_hi, b_lo, 2, 1) + _bdot(a_lo, b_hi, 2, 1))


def _gdn_solve(q, k, v, beta, b_c, b_r, chunk):
    c = chunk
    qn = q * lax.rsqrt(jnp.sum(q * q, axis=-1, keepdims=True) + EPS) * (HD ** -0.5)
    kn = k * lax.rsqrt(jnp.sum(k * k, axis=-1, keepdims=True) + EPS)
    ri = lax.broadcasted_iota(I32, (1, c, c), 1)
    ci = lax.broadcasted_iota(I32, (1, c, c), 2)
    incl = ci <= ri
    decay = jnp.where(incl, jnp.exp(jnp.where(incl, b_c - b_r, 0.0)), 0.0)
    kb, qb16 = kn.astype(BF16), qn.astype(BF16)
    kk = _bdot(kb, kb, 2, 2)
    qk = _bdot(qb16, kb, 2, 2)
    bk = -jnp.where(ci < ri, beta * kk * decay, 0.0)
    eb = jnp.exp(b_c)
    y = jnp.concatenate([beta * v, (beta * eb) * kn], axis=2)
    n_levels = int(math.log2(c))
    for lvl in range(n_levels):
        if lvl + 1 < n_levels:
            prod = _bmm_solve(bk, jnp.concatenate([y, bk], axis=2))
            y, bk = y + prod[:, :, :2 * HD], prod[:, :, 2 * HD:]
        else:
            y = y + _bmm_solve(bk, y)
    b_last = b_c[:, c - 1:c, :]
    return (y[:, :, :HD], y[:, :, HD:].astype(BF16), (qk * decay).astype(BF16), qb16,
            (kn * jnp.exp(b_last - b_c)).astype(BF16), eb, jnp.exp(b_last))


def _gdn_apply(u, w, attn, qb16, kd, eb, eb_last, s0, o_norm, z):
    s0b = s0.astype(BF16)
    delta = u - _bdot(w, s0b, 2, 1)
    deltab = delta.astype(BF16)
    o = eb * _bdot(qb16, s0b, 2, 1) + _bdot(attn, deltab, 2, 1)
    s_new = eb_last * s0 + _bdot(kd, deltab, 1, 1)
    on = o * lax.rsqrt(jnp.mean(o * o, axis=-1, keepdims=True) + EPS) * o_norm
    return s_new, on * _silu(z)


def _gdn_prompt_body(*refs, rows, lo, nb):
    qkv_ref, z_ref, small_ref = refs[:3]
    smallt_refs = refs[3:3 + nb]
    convw_ref, alr_ref, dtr_ref, alc_ref, dtc_ref, onorm_ref, mix_ref, s_ref, tail_scr = refs[3 + nb:]
    t = pl.program_id(1)

    @pl.when(t == 0)
    def _():
        s_ref[...] = jnp.zeros_like(s_ref)
        tail_scr[...] = jnp.zeros_like(tail_scr)

    chunk = GDN_CHUNK
    n_chunks = rows // chunk
    pos_c = t * rows + lax.broadcasted_iota(I32, (rows, LANES), 0)
    pos_r = t * rows + lax.broadcasted_iota(I32, (LANES, rows), 1)
    preps = []
    for s in range(nb):
        qkv = qkv_ref[s]
        tails = [tail_scr[s]] + [qkv[c * chunk - SUBLANES:c * chunk] for c in range(1, n_chunks)]
        preps.append(_gdn_prep(qkv, tails, convw_ref[...], small_ref[s], smallt_refs[s][...],
                               alr_ref[...], dtr_ref[...], alc_ref[...], dtc_ref[...],
                               pos_c >= lo, pos_r >= lo, chunk) + (z_ref[s],))
        tail_scr[s] = qkv[rows - SUBLANES:rows]
    units = [(c, s, h) for c in range(n_chunks) for s in range(nb) for h in range(HEADS)]
    rs = lambda c: slice(c * chunk, (c + 1) * chunk)
    stack = lambda k, off: jnp.stack([preps[s][k][rs(c), off + h * HD:off + (h + 1) * HD] for c, s, h in units])
    col = lambda k, off: jnp.stack([preps[s][k][rs(c), off + h:off + h + 1] for c, s, h in units])
    sol = _gdn_solve(stack(0, 0), stack(0, A_DIM), stack(0, 2 * A_DIM), col(1, S_BETA), col(2, S_A),
                     jnp.stack([preps[s][3][S_A + h:S_A + h + 1, rs(c)] for c, s, h in units]), chunk)
    z3 = stack(4, 0)
    o_norm = onorm_ref[...]
    per_chunk = nb * HEADS
    for c in range(n_chunks):
        g = slice(c * per_chunk, (c + 1) * per_chunk)
        s_new, out = _gdn_apply(*[a[g] for a in sol], s_ref[...].reshape(per_chunk, HD, HD), o_norm, z3[g])
        s_ref[...] = s_new.reshape(nb, HEADS, HD, HD)
        for s in range(nb):
            for h in range(HEADS):
                mix_ref[s, rs(c), h * HD:(h + 1) * HD] = out[s * HEADS + h].astype(mix_ref.dtype)


GDN_SEQS_PER_STEP = 2


def _gate_params(a_log, dt_bias):
    row = lambda v: jnp.zeros((1, LANES), F32).at[0, S_A:S_A + HEADS].set(v.astype(F32))
    return row(a_log), row(dt_bias), row(a_log).T, row(dt_bias).T


def _gdn_prompt(qkva, z, small, small_t, conv_w, a_log, dt_bias, o_norm, bsz, t_pad, lo):
    rows = LANES
    nt = t_pad // rows
    nb = GDN_SEQS_PER_STEP if bsz % GDN_SEQS_PER_STEP == 0 else 1
    alr, dtr, alc, dtc = _gate_params(a_log, dt_bias)
    body = functools.partial(_gdn_prompt_body, rows=rows, lo=lo, nb=nb)
    seqs = lambda w: pl.BlockSpec((nb, rows, w), lambda b, t: (b, t, 0))
    mix, s = pl.pallas_call(
        body,
        grid=(bsz // nb, nt),
        in_specs=[seqs(3 * A_DIM), seqs(A_DIM), seqs(LANES)] + [
            pl.BlockSpec((LANES, rows), functools.partial(lambda b, t, s: (0, (b * nb + s) * nt + t), s=s))
            for s in range(nb)] + [
            _const_spec((CONV_WIDTH, 3 * A_DIM)), _const_spec((1, LANES)), _const_spec((1, LANES)),
            _const_spec((LANES, 1)), _const_spec((LANES, 1)), _const_spec((1, HD)),
        ],
        out_specs=[
            pl.BlockSpec((nb, rows, A_DIM), lambda b, t: (b, jnp.maximum(t - 1, 0), 0)),
            pl.BlockSpec((nb, HEADS, HD, HD), lambda b, t: (b, 0, 0, 0)),
        ],
        out_shape=[jax.ShapeDtypeStruct((bsz, t_pad - FRONT, A_DIM), BF16),
                   jax.ShapeDtypeStruct((bsz, HEADS, HD, HD), F32)],
        scratch_shapes=[pltpu.VMEM((nb, SUBLANES, 3 * A_DIM), F32)],
        compiler_params=pltpu.CompilerParams(dimension_semantics=("parallel", "arbitrary"),
                                             vmem_limit_bytes=VMEM_LIMIT),
        name="gdn_prompt",
    )(qkva.reshape(bsz, t_pad, 3 * A_DIM), z.reshape(bsz, t_pad, A_DIM), small.reshape(bsz, t_pad, LANES),
      *([small_t] * nb), conv_w, alr, dtr, alc, dtc, o_norm.reshape(1, HD))
    return mix, s


def _gdn_sample_body(qkv_ref, z_ref, small_ref, smallt_ref, conv0_ref, s0_ref, convw_ref, alr_ref, dtr_ref, alc_ref,
                     dtc_ref, onorm_ref, mix_ref, s_ref, *, chunk):
    rows = LANES
    n_seq = rows // chunk
    qkv = qkv_ref[...]
    tails = [conv0_ref[i] for i in range(n_seq)]
    y, beta_c, b_c, b_r = _gdn_prep(qkv, tails, convw_ref[...], small_ref[...], smallt_ref[...],
                                    alr_ref[...], dtr_ref[...], alc_ref[...], dtc_ref[...], True, True, chunk)
    z = z_ref[...]
    per_seq = lambda a: a.reshape(n_seq, chunk, a.shape[-1])
    stack = lambda a, off: jnp.concatenate([per_seq(a[:, off + h * HD:off + (h + 1) * HD]) for h in range(HEADS)])
    col = lambda a, off: jnp.concatenate([per_seq(a[:, off + h:off + h + 1]) for h in range(HEADS)])
    b_r3 = jnp.stack([b_r[S_A + h:S_A + h + 1, i * chunk:(i + 1) * chunk] for h in range(HEADS) for i in range(n_seq)])
    sol = _gdn_solve(stack(y, 0), stack(y, A_DIM), stack(y, 2 * A_DIM), col(beta_c, S_BETA), col(b_c, S_A), b_r3, chunk)
    s0 = jnp.concatenate([s0_ref[:, h] for h in range(HEADS)])
    s_new, out = _gdn_apply(*sol, s0, onorm_ref[...], stack(z, 0))
    for h in range(HEADS):
        g = slice(h * n_seq, (h + 1) * n_seq)
        s_ref[:, h] = s_new[g]
        mix_ref[:, h * HD:(h + 1) * HD] = out[g].reshape(rows, HD).astype(mix_ref.dtype)


def _gdn_sample(qkva, z, small, small_t, state_conv, state_ssm, conv_w, a_log, dt_bias, o_norm, n_new):
    n = qkva.shape[0]
    bsz = n // n_new
    rows = LANES
    n_seq = rows // n_new
    assert n_new == SUBLANES and n % rows == 0
    alr, dtr, alc, dtc = _gate_params(a_log, dt_bias)
    conv0 = jnp.pad(state_conv.astype(F32), ((0, 0), (SUBLANES - (CONV_WIDTH - 1), 0), (0, 0)))
    body = functools.partial(_gdn_sample_body, chunk=n_new)
    mix, s = pl.pallas_call(
        body,
        grid=(n // rows,),
        in_specs=[
            pl.BlockSpec((rows, 3 * A_DIM), lambda g: (g, 0)),
            pl.BlockSpec((rows, A_DIM), lambda g: (g, 0)),
            pl.BlockSpec((rows, LANES), lambda g: (g, 0)),
            pl.BlockSpec((LANES, rows), lambda g: (0, g)),
            pl.BlockSpec((n_seq, SUBLANES, 3 * A_DIM), lambda g: (g, 0, 0)),
            pl.BlockSpec((n_seq, HEADS, HD, HD), lambda g: (g, 0, 0, 0)),
            _const_spec((CONV_WIDTH, 3 * A_DIM)), _const_spec((1, LANES)), _const_spec((1, LANES)),
            _const_spec((LANES, 1)), _const_spec((LANES, 1)), _const_spec((1, HD)),
        ],
        out_specs=[
            pl.BlockSpec((rows, A_DIM), lambda g: (g, 0)),
            pl.BlockSpec((n_seq, HEADS, HD, HD), lambda g: (g, 0, 0, 0)),
        ],
        out_shape=[jax.ShapeDtypeStruct((n, A_DIM), BF16), jax.ShapeDtypeStruct((bsz, HEADS, HD, HD), F32)],
        compiler_params=pltpu.CompilerParams(dimension_semantics=("parallel",), vmem_limit_bytes=VMEM_LIMIT),
        name="gdn_sample",
    )(qkva, z, small, small_t, conv0, state_ssm.astype(F32), conv_w, alr, dtr, alc, dtc, o_norm.reshape(1, HD))
    return mix, s


def _rel_bucket(d):
    d = jnp.maximum(d, 0)
    max_exact = REL_BUCKETS // 2
    df = jnp.maximum(d, 1).astype(F32)
    large = max_exact + (jnp.log(df / max_exact) / math.log(REL_MAX_DIST / max_exact)
                         * (REL_BUCKETS - max_exact)).astype(I32)
    large = jnp.minimum(large, REL_BUCKETS - 1)
    return jnp.where(d < max_exact, d, large)


def _bias_lookup(bucket, rb_ref, h):
    acc = jnp.zeros(bucket.shape, F32)
    for b in range(REL_BUCKETS):
        acc = jnp.where(bucket == b, rb_ref[b, h], acc)
    return acc


def _bias_tab_body(rb_ref, out_ref):
    dlt = pl.program_id(0)
    i = lax.broadcasted_iota(I32, (LANES, LANES), 0)
    j = lax.broadcasted_iota(I32, (LANES, LANES), 1)
    bucket = _rel_bucket(dlt * LANES + i - j)
    for h in range(HEADS):
        out_ref[h, 0] = _bias_lookup(bucket, rb_ref, h)


def _bias_tab(rel_bias, n_tiles):
    return pl.pallas_call(
        _bias_tab_body,
        grid=(n_tiles,),
        in_specs=[pl.BlockSpec(memory_space=pltpu.SMEM)],
        out_specs=pl.BlockSpec((HEADS, 1, LANES, LANES), lambda t: (0, t, 0, 0)),
        out_shape=jax.ShapeDtypeStruct((HEADS, n_tiles, LANES, LANES), F32),
        name="bias_tab",
    )(rel_bias.astype(F32))


def _bias_sample_body(rb_ref, out_ref, *, past_len):
    blk = pl.program_id(0)
    q = lax.broadcasted_iota(I32, (SUBLANES, LANES), 0)
    s = blk * LANES + lax.broadcasted_iota(I32, (SUBLANES, LANES), 1)
    bucket = _rel_bucket(past_len + q - s)
    for h in range(HEADS):
        out_ref[h * SUBLANES:(h + 1) * SUBLANES, :] = _bias_lookup(bucket, rb_ref, h)


def _bias_sample(rel_bias, past_len, n_blocks):
    return pl.pallas_call(
        functools.partial(_bias_sample_body, past_len=past_len),
        grid=(n_blocks,),
        in_specs=[pl.BlockSpec(memory_space=pltpu.SMEM)],
        out_specs=pl.BlockSpec((HEADS * SUBLANES, LANES), lambda t: (0, t)),
        out_shape=jax.ShapeDtypeStruct((HEADS * SUBLANES, n_blocks * LANES), F32),
        name="bias_sample",
    )(rel_bias.astype(F32))


def _sort_key(score, admissible):
    score = jnp.where(score == 0.0, 0.0, score)
    bits = pltpu.bitcast(score, I32)
    key = jnp.where(bits < 0, bits ^ 0x7FFFFFFF, bits)
    return jnp.where(admissible, key, INT_MIN)


def _kth_largest_key(count_ge, rows, topk):
    c0 = jnp.broadcast_to(count_ge(jnp.zeros((rows, LANES), I32)), (rows, LANES))
    t0 = jnp.where(c0 >= topk, 0, INT_MIN).astype(I32)
    c0 = jnp.where(c0 >= topk, c0, float(2 ** 30))

    def bit_step(i, carry):
        t, c = carry
        cand = t + lax.shift_left(jnp.int32(1), 30 - i)
        cc = jnp.broadcast_to(count_ge(cand), (rows, LANES))
        ok = cc >= topk
        return jnp.where(ok, cand, t), jnp.where(ok, cc, c)

    return lax.fori_loop(0, 31, bit_step, (t0, c0))


def _tie_mask(key, thr, need, running, tri):
    eq = (key == thr) & (key != INT_MIN)
    pref = _mm(jnp.where(eq, 1.0, 0.0).astype(BF16), tri) + running
    sel = (key > thr) | (eq & (pref <= need))
    return sel, jnp.broadcast_to(pref[:, LANES - 1:LANES], pref.shape)


def _tri_incl():
    r = lax.broadcasted_iota(I32, (LANES, LANES), 0)
    c = lax.broadcasted_iota(I32, (LANES, LANES), 1)
    return jnp.where(r <= c, 1.0, 0.0).astype(BF16)


def _dsa_prompt_tiles(first, n_tiles, qi, w_raw, kt_ref, kidxt_ref, bias_ref, out_ref,
                      qh_scr, vh_scr, keys_scr, mbias_scr, o_scr, *, lo, topk):
    nblk = first + n_tiles
    tq = n_tiles * LANES
    wid = nblk * LANES
    w = w_raw * (HEADS ** -0.5)
    tiles = [(slice(r * LANES, (r + 1) * LANES), (first + r + 1) * LANES) for r in range(n_tiles)]
    for r, (rows, wr) in enumerate(tiles):
        t_pos = (first + r) * LANES + lax.broadcasted_iota(I32, (LANES, wr), 0)
        s_pos = lax.broadcasted_iota(I32, (LANES, wr), 1)
        kxt = kidxt_ref[S_KIDX:S_KIDX + HD, 0:wr]
        acc = jnp.zeros((LANES, wr), F32)
        for h in range(HEADS):
            acc = acc + jnp.maximum(_mm(qi[rows, h * HD:(h + 1) * HD], kxt), 0.0) * w[rows, h:h + 1]
        keys_scr[rows, 0:wr] = _sort_key(acc, (s_pos <= t_pos) & (s_pos >= lo))
        if wr < wid:
            keys_scr[rows, wr:wid] = jnp.full((LANES, wid - wr), INT_MIN, I32)

    def count_ge(cand):
        return jnp.sum(jnp.where(keys_scr[:, 0:wid] >= cand[:, 0:1], 1.0, 0.0), axis=1, keepdims=True)

    thr, cnt = _kth_largest_key(count_ge, tq, topk)
    keys = keys_scr[:, 0:wid]
    mbias_scr[:, 0:wid] = jnp.where(keys >= jnp.maximum(thr[:, 0:1], INT_MIN + 1), 0.0, NEG)

    @pl.when(jnp.max(jnp.where(cnt < float(2 ** 30), cnt, 0.0)) > topk)
    def _():
        need = topk - jnp.sum(jnp.where(keys_scr[:, 0:wid] > thr[:, 0:1], 1.0, 0.0), axis=1, keepdims=True)
        tri = _tri_incl()
        running = jnp.zeros((tq, LANES), F32)
        for jb in range(nblk):
            cols = slice(jb * LANES, (jb + 1) * LANES)
            sel, running = _tie_mask(keys_scr[:, cols], thr, need, running, tri)
            mbias_scr[:, cols] = jnp.where(sel, 0.0, NEG)

    def head_attn(h, carry):
        bias = jnp.concatenate(
            [jnp.concatenate([bias_ref[h, max(first + r - jb, 0)] for jb in range(nblk)], axis=1)
             for r in range(n_tiles)], axis=0)
        kt = kt_ref[pl.ds(pl.multiple_of(h * HD, HD), HD), 0:wid]
        lg = _mm(qh_scr[h], kt) + bias + mbias_scr[:, 0:wid]
        p = jnp.exp(lg - jnp.max(lg, axis=1, keepdims=True))
        inv = 1.0 / jnp.sum(p, axis=1, keepdims=True)
        o_scr[h] = _mm(p.astype(BF16), vh_scr[h, 0:wid, :]) * inv
        return carry

    lax.fori_loop(0, HEADS, head_attn, 0, unroll=2)
    for h in range(HEADS):
        out_ref[0, :, h * HD:(h + 1) * HD] = o_scr[h].astype(out_ref.dtype)


def _dsa_prompt_body(*refs, lo, topk, n_tiles, n_steps):
    qb_refs, qi_refs, small_refs = refs[:n_tiles], refs[n_tiles:2 * n_tiles], refs[2 * n_tiles:3 * n_tiles]
    (kt_ref, vbf_ref, kidxt_ref, bias_ref, out_ref, qh_scr, vh_scr, keys_scr, mbias_scr, o_scr) = refs[3 * n_tiles:]
    js = pl.program_id(1)

    @pl.when(js == 0)
    def _():
        for h in range(HEADS):
            vh_scr[h] = vbf_ref[0, :, h * HD:(h + 1) * HD]

    for r in range(n_tiles):
        for h in range(HEADS):
            qh_scr[h, r * LANES:(r + 1) * LANES, :] = qb_refs[r][0, :, h * HD:(h + 1) * HD]
    qi = jnp.concatenate([ref[0] for ref in qi_refs], axis=0)
    w_raw = jnp.concatenate([ref[0][:, S_W:S_W + HEADS] for ref in small_refs], axis=0)
    for step in range(n_steps):
        @pl.when(js == step)
        def _(step=step):
            _dsa_prompt_tiles(1 + step * n_tiles, n_tiles, qi, w_raw, kt_ref, kidxt_ref, bias_ref, out_ref,
                              qh_scr, vh_scr, keys_scr, mbias_scr, o_scr, lo=lo, topk=topk)


DSA_TILES_PER_STEP = 4


def _dsa_prompt(qb, qi, small, kt, vbf, smalltbf, bias_tab, bsz, t_pad, lo, topk):
    n_real = t_pad // LANES - 1
    n_tiles = max(d for d in (1, 2, DSA_TILES_PER_STEP) if n_real % d == 0)
    n_steps = n_real // n_tiles
    rows = n_tiles * LANES
    r3 = lambda a: a.reshape(bsz, t_pad, a.shape[-1])
    tiles = lambda w: [pl.BlockSpec((1, LANES, w), functools.partial(lambda b, js, r: (b, 1 + js * n_tiles + r, 0), r=r))
                       for r in range(n_tiles)]
    seq_cols = lambda r: pl.BlockSpec((r, t_pad), lambda b, js: (0, b))
    return pl.pallas_call(
        functools.partial(_dsa_prompt_body, lo=lo, topk=topk, n_tiles=n_tiles, n_steps=n_steps),
        grid=(bsz, n_steps),
        in_specs=tiles(B_DIM) + tiles(HEADS * HD) + tiles(LANES) + [
            seq_cols(B_DIM), pl.BlockSpec((1, t_pad, B_DIM), lambda b, js: (b, 0, 0)), seq_cols(LANES),
            _const_spec(bias_tab.shape)],
        out_specs=pl.BlockSpec((1, rows, B_DIM), lambda b, js: (b, js, 0)),
        out_shape=jax.ShapeDtypeStruct((bsz, t_pad - FRONT, B_DIM), BF16),
        scratch_shapes=[pltpu.VMEM((HEADS, rows, HD), BF16), pltpu.VMEM((HEADS, t_pad, HD), BF16),
                        pltpu.VMEM((rows, t_pad), I32), pltpu.VMEM((rows, t_pad), F32),
                        pltpu.VMEM((HEADS, rows, HD), F32)],
        compiler_params=pltpu.CompilerParams(dimension_semantics=("parallel", "arbitrary"),
                                             vmem_limit_bytes=VMEM_LIMIT),
        name="dsa_prompt",
    )(*([r3(qb)] * n_tiles), *([r3(qi)] * n_tiles), *([r3(small)] * n_tiles), kt, r3(vbf), smalltbf, bias_tab)


PAGES_PER_STEP = 16


def _dsa_sample_body(pt_ref, *refs, n_new, past_len, topk):
    npp = PAGES_PER_STEP
    kidx_refs, k_refs, v_refs = refs[:npp], refs[npp:2 * npp], refs[2 * npp:3 * npp]
    (qi_ref, qbd_ref, w_ref, knew_ref, vnew_ref, kidxnew_ref, bias_ref, out_ref,
     keys_scr, mb_scr, logit_scr, vt_scr) = refs[3 * npp:]
    del pt_ref
    s = pl.program_id(1)
    n_page_steps = past_len // (PAGE * npp)
    n_blocks = past_len // PAGE + 1
    qi = qi_ref[0]
    qbd = qbd_ref[0]
    w = w_ref[0] * (HEADS ** -0.5)
    q_row = lax.broadcasted_iota(I32, (n_new, LANES), 0)
    lane = lax.broadcasted_iota(I32, (n_new, LANES), 1)

    def do_block(off, kxt, kt, vt, admissible):
        rel = jnp.maximum(_mm(qi, kxt), 0.0) * w
        sc = rel[0:n_new]
        for h in range(1, HEADS):
            sc = sc + rel[h * n_new:(h + 1) * n_new]
        keys_scr[:, pl.ds(off, LANES)] = _sort_key(sc, admissible)
        logit_scr[:, pl.ds(off, LANES)] = _mm(qbd, kt) + bias_ref[:, pl.ds(off, LANES)]
        vt_scr[:, pl.ds(off, LANES)] = vt

    @pl.when(s < n_page_steps)
    def _():
        for i in range(npp):
            off = pl.multiple_of((s * npp + i) * PAGE, PAGE)
            do_block(off, kidx_refs[i][0].astype(BF16), k_refs[i][0].reshape(B_DIM, PAGE).astype(BF16),
                     v_refs[i][0].reshape(B_DIM, PAGE).astype(BF16), True)

    @pl.when(s == n_page_steps)
    def _():
        tr = lambda a: jnp.concatenate(
            [a.astype(F32), jnp.zeros((PAGE - n_new, a.shape[1]), F32)], axis=0).T.astype(BF16)
        do_block(past_len, tr(kidxnew_ref[0])[S_KIDX:S_KIDX + HD], tr(knew_ref[0]), tr(vnew_ref[0]), lane <= q_row)

        keys = keys_scr[...]
        thr, cnt = _kth_largest_key(
            lambda cand: jnp.sum(jnp.where(keys >= cand[:, 0:1], 1.0, 0.0), axis=1, keepdims=True), n_new, topk)
        mb_scr[...] = jnp.where(keys >= jnp.maximum(thr[:, 0:1], INT_MIN + 1), 0.0, NEG)

        @pl.when(jnp.max(jnp.where(cnt < float(2 ** 30), cnt, 0.0)) > topk)
        def _():
            need = topk - jnp.sum(jnp.where(keys > thr[:, 0:1], 1.0, 0.0), axis=1, keepdims=True)
            tri = _tri_incl()

            def mask_blk(jb, running):
                cols = pl.ds(pl.multiple_of(jb * LANES, LANES), LANES)
                sel, running = _tie_mask(keys_scr[:, cols], thr, need, running, tri)
                mb_scr[:, cols] = jnp.where(sel, 0.0, NEG)
                return running

            lax.fori_loop(0, n_blocks, mask_blk, jnp.zeros((n_new, LANES), F32))

        lg = logit_scr[...] + jnp.concatenate([mb_scr[...]] * HEADS, axis=0)
        m = jnp.max(lg, axis=1, keepdims=True)
        p = jnp.exp(lg - m)
        pb = (p * (1.0 / jnp.sum(p, axis=1, keepdims=True))).astype(BF16)
        out_ref[0] = _nt(vt_scr[...], pb)


def _dsa_sample(qb, qi, small, kbf, vbf, smallbf, cache_k, cache_v, cache_kidx, page_table, bias_s, n_new, topk):
    dbs, n_pages = page_table.shape
    past_len = n_pages * PAGE
    npp = PAGES_PER_STEP
    assert n_pages % npp == 0 and n_new == SUBLANES
    n_page_steps = n_pages // npp
    n_blocks = n_pages + 1
    n_pool = cache_k.shape[0]
    qi_rows = qi.reshape(dbs, n_new, HEADS, HD).transpose(0, 2, 1, 3).reshape(dbs, HEADS * n_new, HD)
    q4 = qb.reshape(dbs, n_new, HEADS, HD).transpose(0, 2, 1, 3)
    eye = jnp.eye(HEADS, dtype=qb.dtype)
    qbd = (q4[:, :, :, None, :] * eye[None, :, None, :, None]).reshape(dbs, HEADS * n_new, B_DIM)
    w_rows = small.reshape(dbs, n_new, LANES)[:, :, S_W:S_W + HEADS].transpose(0, 2, 1).reshape(dbs, HEADS * n_new, 1)
    w_rows = jnp.broadcast_to(w_rows, (dbs, HEADS * n_new, LANES))

    def page_spec(shape, i):
        def imap(b, s, pt):
            return (pt[b, jnp.minimum(s, n_page_steps - 1) * npp + i],) + (0,) * len(shape)
        return pl.BlockSpec((1,) + shape, imap)

    per_seq = lambda r, w: pl.BlockSpec((1, r, w), lambda b, s, pt: (b, 0, 0))
    in_specs = ([page_spec((HD, PAGE), i) for i in range(npp)] + [page_spec((HEADS, HD, PAGE), i) for i in range(npp)]
                + [page_spec((HEADS, HD, PAGE), i) for i in range(npp)]
                + [per_seq(HEADS * n_new, HD), per_seq(HEADS * n_new, B_DIM), per_seq(HEADS * n_new, LANES),
                   per_seq(n_new, B_DIM), per_seq(n_new, B_DIM), per_seq(n_new, LANES),
                   pl.BlockSpec(bias_s.shape, lambda b, s, pt: (0, 0), pipeline_mode=pl.Buffered(1))])
    grid_spec = pltpu.PrefetchScalarGridSpec(
        num_scalar_prefetch=1,
        grid=(dbs, n_page_steps + 1),
        in_specs=in_specs,
        out_specs=pl.BlockSpec((1, B_DIM, HEADS * n_new), lambda b, s, pt: (b, 0, 0)),
        scratch_shapes=[pltpu.VMEM((n_new, n_blocks * LANES), I32), pltpu.VMEM((n_new, n_blocks * LANES), F32),
                        pltpu.VMEM((HEADS * n_new, n_blocks * LANES), F32),
                        pltpu.VMEM((B_DIM, n_blocks * LANES), BF16)],
    )
    ck = cache_k.transpose(0, 2, 3, 1)
    cv = cache_v.transpose(0, 2, 3, 1)
    cki = cache_kidx.transpose(0, 2, 1)
    r3 = lambda a: a.reshape(dbs, n_new, a.shape[-1])
    out_t = pl.pallas_call(
        functools.partial(_dsa_sample_body, n_new=n_new, past_len=past_len, topk=topk),
        grid_spec=grid_spec,
        out_shape=jax.ShapeDtypeStruct((dbs, B_DIM, HEADS * n_new), F32),
        compiler_params=pltpu.CompilerParams(dimension_semantics=("parallel", "arbitrary"),
                                             vmem_limit_bytes=VMEM_LIMIT),
        name="dsa_sample",
    )(page_table, *([cki] * npp), *([ck] * npp), *([cv] * npp),
      qi_rows, qbd, w_rows, r3(kbf), r3(vbf), r3(smallbf), bias_s)
    o5 = out_t.reshape(dbs, HEADS, HD, HEADS, n_new)
    o = jnp.stack([o5[:, h, :, h, :] for h in range(HEADS)], axis=1)
    return o.transpose(0, 3, 1, 2).reshape(dbs, n_new, B_DIM).astype(BF16)


def _rms(x, g):
    return x * lax.rsqrt(jnp.mean(x * x, axis=-1, keepdims=True) + EPS) * g


def _mlp_body(x_ref, ma_ref, ob_ref, woa_ref, wob_ref, gffn_ref, wg_ref, wu_ref, wd_ref, gfin_ref, y_ref):
    h = x_ref[...] + (_mm(ma_ref[...], woa_ref[...]) + _mm(ob_ref[...], wob_ref[...]))
    hn = _rms(h, gffn_ref[...]).astype(BF16)
    act = _silu(_mm(hn, wg_ref[...])) * _mm(hn, wu_ref[...])
    y_ref[...] = _rms(h + _mm(act.astype(BF16), wd_ref[...]), gfin_ref[...])


def _mlp(x2d, mix_a, o_b, w_out, g_ffn, w_gate, w_up, w_down, g_final, tm):
    n = x2d.shape[0]
    assert n % tm == 0
    row = lambda w: pl.BlockSpec((tm, w), lambda i: (i, 0))
    woa, wob = w_out[:A_DIM].astype(BF16), w_out[A_DIM:].astype(BF16)
    return pl.pallas_call(
        _mlp_body,
        grid=(n // tm,),
        in_specs=[row(D_MODEL), row(A_DIM), row(B_DIM), _const_spec(woa.shape), _const_spec(wob.shape),
                  _const_spec((1, D_MODEL)), _const_spec(w_gate.shape), _const_spec(w_up.shape),
                  _const_spec(w_down.shape), _const_spec((1, D_MODEL))],
        out_specs=row(D_MODEL),
        out_shape=jax.ShapeDtypeStruct((n, D_MODEL), F32),
        compiler_params=pltpu.CompilerParams(dimension_semantics=("parallel",), vmem_limit_bytes=VMEM_LIMIT),
        name="mlp",
    )(x2d, mix_a, o_b, woa, wob, g_ffn.reshape(1, D_MODEL), w_gate.astype(BF16), w_up.astype(BF16),
      w_down.astype(BF16), g_final.reshape(1, D_MODEL))


def kernel(x_prompt, x_sample, cache_k, cache_v, cache_kidx, state_conv, state_ssm, page_table, meta_tokens, norm_mix, w_in, conv_w, a_log, dt_bias, o_norm, w_out, rel_bias, norm_ffn, w_gate, w_up, w_down, norm_final):
    assert w_in.shape[0] == 1, "single-layer stack"
    bsz, seq, _ = x_prompt.shape
    dbs, n_new, _ = x_sample.shape
    t_pad = FRONT + seq
    lo = FRONT - N_META
    past_len = page_table.shape[1] * PAGE
    topk_p = min(TOPK_MAX, seq // 4)
    topk_s = min(TOPK_MAX, (past_len + n_new) // 4)
    w_parts = _split_w_in(w_in[0])
    mlp_w = (w_out[0], norm_ffn[0], w_gate[0], w_up[0], w_down[0], norm_final)
    gdn_w = (conv_w[0], a_log[0], dt_bias[0], o_norm[0])

    meta = jnp.broadcast_to(meta_tokens.astype(F32)[None], (bsz, N_META, D_MODEL))
    xp = jnp.concatenate([jnp.zeros((bsz, lo, D_MODEL), F32), meta, x_prompt], axis=1).reshape(bsz * t_pad, D_MODEL)
    qkva, z, qb, k, v, _, vbf, qi, small, _, small_t, small_tbf, kt = _proj(xp, norm_mix[0], w_parts, 512)
    mix_a, ssm_p = _gdn_prompt(qkva, z, small, small_t, *gdn_w, bsz, t_pad, lo)
    bias_tab = _bias_tab(rel_bias, t_pad // LANES)
    o_b = _dsa_prompt(qb, qi, small, kt, vbf, small_tbf, bias_tab, bsz, t_pad, lo, topk_p)
    y_p = _mlp(x_prompt.reshape(bsz * seq, D_MODEL), mix_a.reshape(bsz * seq, A_DIM), o_b.reshape(bsz * seq, B_DIM),
               *mlp_w, 512).reshape(bsz, seq, D_MODEL)
    real = lambda a: a.reshape(bsz, t_pad, a.shape[-1])[:, lo:]
    k_p = real(k).reshape(1, bsz, N_META + seq, HEADS, HD)
    v_p = real(v).reshape(1, bsz, N_META + seq, HEADS, HD)
    kidx_p = real(small)[:, :, S_KIDX:S_KIDX + HD][None]
    conv_p = real(qkva)[:, -(CONV_WIDTH - 1):][None]

    xs = x_sample.reshape(dbs * n_new, D_MODEL)
    (qkva_s, z_s, qb_s, k_s, v_s, kbf_s, vbf_s, qi_s, small_s, smallbf_s, small_t_s, _, _) = _proj(
        xs, norm_mix[0], w_parts, 512)
    mix_a_s, ssm_s = _gdn_sample(qkva_s, z_s, small_s, small_t_s, state_conv[0], state_ssm[0], *gdn_w, n_new)
    bias_s = _bias_sample(rel_bias, past_len, past_len // PAGE + 1)
    o_b_s = _dsa_sample(qb_s, qi_s, small_s, kbf_s, vbf_s, smallbf_s, cache_k[0], cache_v[0], cache_kidx[0], page_table,
                        bias_s, n_new, topk_s)
    y_s = _mlp(xs, mix_a_s, o_b_s.reshape(dbs * n_new, B_DIM), *mlp_w, 512).reshape(dbs, n_new, D_MODEL)
    k_sn = k_s.reshape(1, dbs, n_new, HEADS, HD)
    v_sn = v_s.reshape(1, dbs, n_new, HEADS, HD)
    kidx_s = small_s.reshape(dbs, n_new, LANES)[:, :, S_KIDX:S_KIDX + HD][None]
    conv_s = jnp.concatenate([state_conv[0].astype(F32), qkva_s.reshape(dbs, n_new, 3 * A_DIM)],
                             axis=1)[:, -(CONV_WIDTH - 1):][None]
    return (y_p, y_s, k_p, v_p, kidx_p, conv_p, ssm_p[None], k_sn, v_sn, kidx_s, conv_s, ssm_s[None])
```

```python
import functools
import math

import jax
import jax.numpy as jnp
from jax import lax
from jax.experimental import pallas as pl
from jax.experimental.pallas import tpu as pltpu

F32 = jnp.float32
BF16 = jnp.bfloat16
I32 = jnp.int32
HI = lax.Precision.HIGHEST

D_MODEL = 1024
N_META = 16
HEADS = 8
HD = 64
A_DIM = HEADS * HD
B_DIM = HEADS * HD
CONV_WIDTH = 4
GDN_CHUNK = 64
TOPK_MAX = 256
PAGE = 128
REL_BUCKETS = 32
REL_MAX_DIST = 1024
D_FF = 2816
EPS = 1e-6
LANES = 128
SUBLANES = 8
FRONT = 128
NEG = -1e30
INT_MIN = -2 ** 31
VMEM_LIMIT = 56 * 1024 * 1024

S_KIDX, S_BETA, S_A, S_W = 0, 64, 72, 80


def _mm(a, b):
    return jnp.dot(a, b, preferred_element_type=F32)


def _mm_hi(a, b):
    return jnp.dot(a, b, preferred_element_type=F32, precision=HI)


def _split_bf16(x):
    hi = x.astype(BF16)
    return hi, (x - hi.astype(F32)).astype(BF16)


def _nt(a, b, precision=None):
    return lax.dot_general(a, b, (((1,), (1,)), ((), ())), preferred_element_type=F32, precision=precision)


def _const_spec(shape):
    nd = len(shape)
    return pl.BlockSpec(shape, lambda *_: (0,) * nd, pipeline_mode=pl.Buffered(1))


def _silu(x):
    return x * jax.nn.sigmoid(x)


def _proj_body(x_ref, g_ref, wa_ref, wz_ref, wb_ref, wqi_ref, ws_ref, wst_ref,
               qkva_ref, z_ref, qb_ref, k_ref, v_ref, kbf_ref, vbf_ref, qi_ref, small_ref, smallbf_ref, smallt_ref,
               smalltbf_ref, kt_ref):
    x = x_ref[...]
    ms = jnp.mean(x * x, axis=-1, keepdims=True)
    h = (x * lax.rsqrt(ms + EPS) * g_ref[...]).astype(BF16)
    qkva_ref[...] = _mm(h, wa_ref[...])
    z_ref[...] = _mm(h, wz_ref[...])
    qkvb = _mm(h, wb_ref[...])
    qb_ref[...] = (qkvb[:, :B_DIM] * 0.125).astype(BF16)
    k = qkvb[:, B_DIM:2 * B_DIM]
    v = qkvb[:, 2 * B_DIM:]
    k_ref[...] = k
    v_ref[...] = v
    kbf_ref[...] = k.astype(BF16)
    vbf_ref[...] = v.astype(BF16)
    qi_ref[...] = (_mm(h, wqi_ref[...]) * 0.125).astype(BF16)
    s = _mm(h, ws_ref[...])
    small_ref[...] = s
    smallbf_ref[...] = s.astype(BF16)
    t = _nt(wst_ref[...], h)
    smallt_ref[...] = t[:LANES]
    smalltbf_ref[...] = t[:LANES].astype(BF16)
    kt_ref[...] = t[LANES:].astype(BF16)


def _split_w_in(w_in):
    c = 0
    parts = []
    for n in (3 * A_DIM, A_DIM, HEADS, HEADS, 3 * B_DIM, HEADS * HD, HD, HEADS):
        parts.append(w_in[:, c:c + n])
        c += n
    w_qkva, w_z, w_beta, w_a, w_qkvb, w_qi, w_kidx, w_w = parts
    pad = jnp.zeros((w_in.shape[0], LANES - (HD + 3 * HEADS)), w_in.dtype)
    w_small = jnp.concatenate([w_kidx, w_beta, w_a, w_w, pad], axis=1)
    bf = lambda t: t.astype(BF16)
    w_t = jnp.concatenate([w_small, w_qkvb[:, B_DIM:2 * B_DIM]], axis=1).T
    return bf(w_qkva), bf(w_z), bf(w_qkvb), bf(w_qi), bf(w_small), bf(w_t)


def _proj(x2d, g, w_parts, tm):
    n = x2d.shape[0]
    assert n % tm == 0 and tm % LANES == 0
    wa, wz, wb, wqi, ws, wst = w_parts
    row = lambda w: pl.BlockSpec((tm, w), lambda i: (i, 0))
    outs = [
        (jax.ShapeDtypeStruct((n, 3 * A_DIM), F32), row(3 * A_DIM)),
        (jax.ShapeDtypeStruct((n, A_DIM), F32), row(A_DIM)),
        (jax.ShapeDtypeStruct((n, B_DIM), BF16), row(B_DIM)),
        (jax.ShapeDtypeStruct((n, B_DIM), F32), row(B_DIM)),
        (jax.ShapeDtypeStruct((n, B_DIM), F32), row(B_DIM)),
        (jax.ShapeDtypeStruct((n, B_DIM), BF16), row(B_DIM)),
        (jax.ShapeDtypeStruct((n, B_DIM), BF16), row(B_DIM)),
        (jax.ShapeDtypeStruct((n, HEADS * HD), BF16), row(HEADS * HD)),
        (jax.ShapeDtypeStruct((n, LANES), F32), row(LANES)),
        (jax.ShapeDtypeStruct((n, LANES), BF16), row(LANES)),
        (jax.ShapeDtypeStruct((LANES, n), F32), pl.BlockSpec((LANES, tm), lambda i: (0, i))),
        (jax.ShapeDtypeStruct((LANES, n), BF16), pl.BlockSpec((LANES, tm), lambda i: (0, i))),
        (jax.ShapeDtypeStruct((B_DIM, n), BF16), pl.BlockSpec((B_DIM, tm), lambda i: (0, i))),
    ]
    return pl.pallas_call(
        _proj_body,
        grid=(n // tm,),
        in_specs=[row(D_MODEL), _const_spec((1, D_MODEL)), _const_spec(wa.shape), _const_spec(wz.shape),
                  _const_spec(wb.shape), _const_spec(wqi.shape), _const_spec(ws.shape), _const_spec(wst.shape)],
        out_specs=[o[1] for o in outs],
        out_shape=[o[0] for o in outs],
        compiler_params=pltpu.CompilerParams(dimension_semantics=("parallel",), vmem_limit_bytes=VMEM_LIMIT),
        name="proj",
    )(x2d, g.reshape(1, D_MODEL), wa, wz, wb, wqi, ws, wst)


def _gdn_prep(qkv, tail, conv_w, small, small_t, a_log_row, dt_row, a_log_col, dt_col, valid_col, valid_row, chunk):
    rows = qkv.shape[0]
    n_chunks = rows // chunk
    ys = []
    for c in range(n_chunks):
        ext = jnp.concatenate([tail[c], qkv[c * chunk:(c + 1) * chunk]], axis=0)
        y = sum(ext[SUBLANES - (CONV_WIDTH - 1) + j: SUBLANES - (CONV_WIDTH - 1) + j + chunk] * conv_w[j:j + 1]
                for j in range(CONV_WIDTH))
        ys.append(_silu(y))
    y = ys[0] if n_chunks == 1 else jnp.concatenate(ys, axis=0)
    beta_c = jnp.where(valid_col, jax.nn.sigmoid(small), 0.0)
    beta_r = jnp.where(valid_row, jax.nn.sigmoid(small_t), 0.0)
    g_c = jnp.where(valid_col, -jnp.exp(a_log_row) * jax.nn.softplus(small + dt_row), 0.0)
    g_r = jnp.where(valid_row, -jnp.exp(a_log_col) * jax.nn.softplus(small_t + dt_col), 0.0)
    ri = lax.broadcasted_iota(I32, (rows, rows), 0)
    ci = lax.broadcasted_iota(I32, (rows, rows), 1)
    same = (ri // chunk) == (ci // chunk)
    low = jnp.where(same & (ci <= ri), 1.0, 0.0).astype(F32)
    b_c = _mm_hi(low, g_c)
    b_r = _nt(g_r, low, precision=HI)
    return y, beta_c, b_c, b_r


def _bdot(a, b, ca, cb):
    return lax.dot_general(a, b, (((ca,), (cb,)), ((0,), (0,))), preferred_element_type=F32)


def _bmm_solve(a, b):
    a_hi, a_lo = _split_bf16(a)
    b_hi, b_lo = _split_bf16(b)
    return _bdot(a_hi, b_hi, 2, 1) + (_bdot(a_hi, b_lo, 2, 1) + _bdot(a_lo, b_hi, 2, 1))


def _gdn_solve(q, k, v, beta, b_c, b_r, chunk):
    c = chunk
    qn = q * lax.rsqrt(jnp.sum(q * q, axis=-1, keepdims=True) + EPS) * (HD ** -0.5)
    kn = k * lax.rsqrt(jnp.sum(k * k, axis=-1, keepdims=True) + EPS)
    ri = lax.broadcasted_iota(I32, (1, c, c), 1)
    ci = lax.broadcasted_iota(I32, (1, c, c), 2)
    incl = ci <= ri
    decay = jnp.where(incl, jnp.exp(jnp.where(incl, b_c - b_r, 0.0)), 0.0)
    kb, qb16 = kn.astype(BF16), qn.astype(BF16)
    kk = _bdot(kb, kb, 2, 2)
    qk = _bdot(qb16, kb, 2, 2)
    bk = -jnp.where(ci < ri, beta * kk * decay, 0.0)
    eb = jnp.exp(b_c)
    y = jnp.concatenate([beta * v, (beta * eb) * kn], axis=2)
    n_levels = int(math.log2(c))
    for lvl in range(n_levels):
        if lvl + 1 < n_levels:
            prod = _bmm_solve(bk, jnp.concatenate([y, bk], axis=2))
            y, bk = y + prod[:, :, :2 * HD], prod[:, :, 2 * HD:]
        else:
            y = y + _bmm_solve(bk, y)
    b_last = b_c[:, c - 1:c, :]
    return (y[:, :, :HD], y[:, :, HD:].astype(BF16), (qk * decay).astype(BF16), qb16,
            (kn * jnp.exp(b_last - b_c)).astype(BF16), eb, jnp.exp(b_last))


def _gdn_apply(u, w, attn, qb16, kd, eb, eb_last, s0, o_norm, z):
    s0b = s0.astype(BF16)
    delta = u - _bdot(w, s0b, 2, 1)
    deltab = delta.astype(BF16)
    o = eb * _bdot(qb16, s0b, 2, 1) + _bdot(attn, deltab, 2, 1)
    s_new = eb_last * s0 + _bdot(kd, deltab, 1, 1)
    on = o * lax.rsqrt(jnp.mean(o * o, axis=-1, keepdims=True) + EPS) * o_norm
    return s_new, on * _silu(z)


def _gdn_prompt_body(*refs, rows, lo, nb):
    qkv_ref, z_ref, small_ref = refs[:3]
    smallt_refs = refs[3:3 + nb]
    convw_ref, alr_ref, dtr_ref, alc_ref, dtc_ref, onorm_ref, mix_ref, s_ref, tail_scr = refs[3 + nb:]
    t = pl.program_id(1)

    @pl.when(t == 0)
    def _():
        s_ref[...] = jnp.zeros_like(s_ref)
        tail_scr[...] = jnp.zeros_like(tail_scr)

    chunk = GDN_CHUNK
    n_chunks = rows // chunk
    pos_c = t * rows + lax.broadcasted_iota(I32, (rows, LANES), 0)
    pos_r = t * rows + lax.broadcasted_iota(I32, (LANES, rows), 1)
    preps = []
    for s in range(nb):
        qkv = qkv_ref[s]
        tails = [tail_scr[s]] + [qkv[c * chunk - SUBLANES:c * chunk] for c in range(1, n_chunks)]
        preps.append(_gdn_prep(qkv, tails, convw_ref[...], small_ref[s], smallt_refs[s][...],
                               alr_ref[...], dtr_ref[...], alc_ref[...], dtc_ref[...],
                               pos_c >= lo, pos_r >= lo, chunk) + (z_ref[s],))
        tail_scr[s] = qkv[rows - SUBLANES:rows]
    units = [(c, s, h) for c in range(n_chunks) for s in range(nb) for h in range(HEADS)]
    rs = lambda c: slice(c * chunk, (c + 1) * chunk)
    stack = lambda k, off: jnp.stack([preps[s][k][rs(c), off + h * HD:off + (h + 1) * HD] for c, s, h in units])
    col = lambda k, off: jnp.stack([preps[s][k][rs(c), off + h:off + h + 1] for c, s, h in units])
    sol = _gdn_solve(stack(0, 0), stack(0, A_DIM), stack(0, 2 * A_DIM), col(1, S_BETA), col(2, S_A),
                     jnp.stack([preps[s][3][S_A + h:S_A + h + 1, rs(c)] for c, s, h in units]), chunk)
    z3 = stack(4, 0)
    o_norm = onorm_ref[...]
    per_chunk = nb * HEADS
    for c in range(n_chunks):
        g = slice(c * per_chunk, (c + 1) * per_chunk)
        s_new, out = _gdn_apply(*[a[g] for a in sol], s_ref[...].reshape(per_chunk, HD, HD), o_norm, z3[g])
        s_ref[...] = s_new.reshape(nb, HEADS, HD, HD)
        for s in range(nb):
            for h in range(HEADS):
                mix_ref[s, rs(c), h * HD:(h + 1) * HD] = out[s * HEADS + h].astype(mix_ref.dtype)


GDN_SEQS_PER_STEP = 2


def _gate_params(a_log, dt_bias):
    row = lambda v: jnp.zeros((1, LANES), F32).at[0, S_A:S_A + HEADS].set(v.astype(F32))
    return row(a_log), row(dt_bias), row(a_log).T, row(dt_bias).T


def _gdn_prompt(qkva, z, small, small_t, conv_w, a_log, dt_bias, o_norm, bsz, t_pad, lo):
    rows = LANES
    nt = t_pad // rows
    nb = GDN_SEQS_PER_STEP if bsz % GDN_SEQS_PER_STEP == 0 else 1
    alr, dtr, alc, dtc = _gate_params(a_log, dt_bias)
    body = functools.partial(_gdn_prompt_body, rows=rows, lo=lo, nb=nb)
    seqs = lambda w: pl.BlockSpec((nb, rows, w), lambda b, t: (b, t, 0))
    mix, s = pl.pallas_call(
        body,
        grid=(bsz // nb, nt),
        in_specs=[seqs(3 * A_DIM), seqs(A_DIM), seqs(LANES)] + [
            pl.BlockSpec((LANES, rows), functools.partial(lambda b, t, s: (0, (b * nb + s) * nt + t), s=s))
            for s in range(nb)] + [
            _const_spec((CONV_WIDTH, 3 * A_DIM)), _const_spec((1, LANES)), _const_spec((1, LANES)),
            _const_spec((LANES, 1)), _const_spec((LANES, 1)), _const_spec((1, HD)),
        ],
        out_specs=[
            pl.BlockSpec((nb, rows, A_DIM), lambda b, t: (b, jnp.maximum(t - 1, 0), 0)),
            pl.BlockSpec((nb, HEADS, HD, HD), lambda b, t: (b, 0, 0, 0)),
        ],
        out_shape=[jax.ShapeDtypeStruct((bsz, t_pad - FRONT, A_DIM), BF16),
                   jax.ShapeDtypeStruct((bsz, HEADS, HD, HD), F32)],
        scratch_shapes=[pltpu.VMEM((nb, SUBLANES, 3 * A_DIM), F32)],
        compiler_params=pltpu.CompilerParams(dimension_semantics=("parallel", "arbitrary"),
                                             vmem_limit_bytes=VMEM_LIMIT),
        name="gdn_prompt",
    )(qkva.reshape(bsz, t_pad, 3 * A_DIM), z.reshape(bsz, t_pad, A_DIM), small.reshape(bsz, t_pad, LANES),
      *([small_t] * nb), conv_w, alr, dtr, alc, dtc, o_norm.reshape(1, HD))
    return mix, s


def _gdn_sample_body(qkv_ref, z_ref, small_ref, smallt_ref, conv0_ref, s0_ref, convw_ref, alr_ref, dtr_ref, alc_ref,
                     dtc_ref, onorm_ref, mix_ref, s_ref, *, chunk):
    rows = LANES
    n_seq = rows // chunk
    qkv = qkv_ref[...]
    tails = [conv0_ref[i] for i in range(n_seq)]
    y, beta_c, b_c, b_r = _gdn_prep(qkv, tails, convw_ref[...], small_ref[...], smallt_ref[...],
                                    alr_ref[...], dtr_ref[...], alc_ref[...], dtc_ref[...], True, True, chunk)
    z = z_ref[...]
    per_seq = lambda a: a.reshape(n_seq, chunk, a.shape[-1])
    stack = lambda a, off: jnp.concatenate([per_seq(a[:, off + h * HD:off + (h + 1) * HD]) for h in range(HEADS)])
    col = lambda a, off: jnp.concatenate([per_seq(a[:, off + h:off + h + 1]) for h in range(HEADS)])
    b_r3 = jnp.stack([b_r[S_A + h:S_A + h + 1, i * chunk:(i + 1) * chunk] for h in range(HEADS) for i in range(n_seq)])
    sol = _gdn_solve(stack(y, 0), stack(y, A_DIM), stack(y, 2 * A_DIM), col(beta_c, S_BETA), col(b_c, S_A), b_r3, chunk)
    s0 = jnp.concatenate([s0_ref[:, h] for h in range(HEADS)])
    s_new, out = _gdn_apply(*sol, s0, onorm_ref[...], stack(z, 0))
    for h in range(HEADS):
        g = slice(h * n_seq, (h + 1) * n_seq)
        s_ref[:, h] = s_new[g]
        mix_ref[:, h * HD:(h + 1) * HD] = out[g].reshape(rows, HD).astype(mix_ref.dtype)


def _gdn_sample(qkva, z, small, small_t, state_conv, state_ssm, conv_w, a_log, dt_bias, o_norm, n_new):
    n = qkva.shape[0]
    bsz = n // n_new
    rows = LANES
    n_seq = rows // n_new
    assert n_new == SUBLANES and n % rows == 0
    alr, dtr, alc, dtc = _gate_params(a_log, dt_bias)
    conv0 = jnp.pad(state_conv.astype(F32), ((0, 0), (SUBLANES - (CONV_WIDTH - 1), 0), (0, 0)))
    body = functools.partial(_gdn_sample_body, chunk=n_new)
    mix, s = pl.pallas_call(
        body,
        grid=(n // rows,),
        in_specs=[
            pl.BlockSpec((rows, 3 * A_DIM), lambda g: (g, 0)),
            pl.BlockSpec((rows, A_DIM), lambda g: (g, 0)),
            pl.BlockSpec((rows, LANES), lambda g: (g, 0)),
            pl.BlockSpec((LANES, rows), lambda g: (0, g)),
            pl.BlockSpec((n_seq, SUBLANES, 3 * A_DIM), lambda g: (g, 0, 0)),
            pl.BlockSpec((n_seq, HEADS, HD, HD), lambda g: (g, 0, 0, 0)),
            _const_spec((CONV_WIDTH, 3 * A_DIM)), _const_spec((1, LANES)), _const_spec((1, LANES)),
            _const_spec((LANES, 1)), _const_spec((LANES, 1)), _const_spec((1, HD)),
        ],
        out_specs=[
            pl.BlockSpec((rows, A_DIM), lambda g: (g, 0)),
            pl.BlockSpec((n_seq, HEADS, HD, HD), lambda g: (g, 0, 0, 0)),
        ],
        out_shape=[jax.ShapeDtypeStruct((n, A_DIM), BF16), jax.ShapeDtypeStruct((bsz, HEADS, HD, HD), F32)],
        compiler_params=pltpu.CompilerParams(dimension_semantics=("parallel",), vmem_limit_bytes=VMEM_LIMIT),
        name="gdn_sample",
    )(qkva, z, small, small_t, conv0, state_ssm.astype(F32), conv_w, alr, dtr, alc, dtc, o_norm.reshape(1, HD))
    return mix, s


def _rel_bucket(d):
    d = jnp.maximum(d, 0)
    max_exact = REL_BUCKETS // 2
    df = jnp.maximum(d, 1).astype(F32)
    large = max_exact + (jnp.log(df / max_exact) / math.log(REL_MAX_DIST / max_exact)
                         * (REL_BUCKETS - max_exact)).astype(I32)
    large = jnp.minimum(large, REL_BUCKETS - 1)
    return jnp.where(d < max_exact, d, large)


def _bias_lookup(bucket, rb_ref, h):
    acc = jnp.zeros(bucket.shape, F32)
    for b in range(REL_BUCKETS):
        acc = jnp.where(bucket == b, rb_ref[b, h], acc)
    return acc


def _bias_tab_body(rb_ref, out_ref):
    dlt = pl.program_id(0)
    i = lax.broadcasted_iota(I32, (LANES, LANES), 0)
    j = lax.broadcasted_iota(I32, (LANES, LANES), 1)
    bucket = _rel_bucket(dlt * LANES + i - j)
    for h in range(HEADS):
        out_ref[h, 0] = _bias_lookup(bucket, rb_ref, h)


def _bias_tab(rel_bias, n_tiles):
    return pl.pallas_call(
        _bias_tab_body,
        grid=(n_tiles,),
        in_specs=[pl.BlockSpec(memory_space=pltpu.SMEM)],
        out_specs=pl.BlockSpec((HEADS, 1, LANES, LANES), lambda t: (0, t, 0, 0)),
        out_shape=jax.ShapeDtypeStruct((HEADS, n_tiles, LANES, LANES), F32),
        name="bias_tab",
    )(rel_bias.astype(F32))


def _bias_sample_body(rb_ref, out_ref, *, past_len):
    blk = pl.program_id(0)
    q = lax.broadcasted_iota(I32, (SUBLANES, LANES), 0)
    s = blk * LANES + lax.broadcasted_iota(I32, (SUBLANES, LANES), 1)
    bucket = _rel_bucket(past_len + q - s)
    for h in range(HEADS):
        out_ref[h * SUBLANES:(h + 1) * SUBLANES, :] = _bias_lookup(bucket, rb_ref, h)


def _bias_sample(rel_bias, past_len, n_blocks):
    return pl.pallas_call(
        functools.partial(_bias_sample_body, past_len=past_len),
        grid=(n_blocks,),
        in_specs=[pl.BlockSpec(memory_space=pltpu.SMEM)],
        out_specs=pl.BlockSpec((HEADS * SUBLANES, LANES), lambda t: (0, t)),
        out_shape=jax.ShapeDtypeStruct((HEADS * SUBLANES, n_blocks * LANES), F32),
        name="bias_sample",
    )(rel_bias.astype(F32))


def _sort_key(score, admissible):
    score = jnp.where(score == 0.0, 0.0, score)
    bits = pltpu.bitcast(score, I32)
    key = jnp.where(bits < 0, bits ^ 0x7FFFFFFF, bits)
    return jnp.where(admissible, key, INT_MIN)


def _kth_largest_key(count_ge, rows, topk):
    c0 = jnp.broadcast_to(count_ge(jnp.zeros((rows, LANES), I32)), (rows, LANES))
    t0 = jnp.where(c0 >= topk, 0, INT_MIN).astype(I32)
    c0 = jnp.where(c0 >= topk, c0, float(2 ** 30))

    def bit_step(i, carry):
        t, c = carry
        cand = t + lax.shift_left(jnp.int32(1), 30 - i)
        cc = jnp.broadcast_to(count_ge(cand), (rows, LANES))
        ok = cc >= topk
        return jnp.where(ok, cand, t), jnp.where(ok, cc, c)

    return lax.fori_loop(0, 31, bit_step, (t0, c0))


def _tie_mask(key, thr, need, running, tri):
    eq = (key == thr) & (key != INT_MIN)
    pref = _mm(jnp.where(eq, 1.0, 0.0).astype(BF16), tri) + running
    sel = (key > thr) | (eq & (pref <= need))
    return sel, jnp.broadcast_to(pref[:, LANES - 1:LANES], pref.shape)


def _tri_incl():
    r = lax.broadcasted_iota(I32, (LANES, LANES), 0)
    c = lax.broadcasted_iota(I32, (LANES, LANES), 1)
    return jnp.where(r <= c, 1.0, 0.0).astype(BF16)


def _dsa_prompt_tiles(first, n_tiles, qi, w_raw, kt_ref, kidxt_ref, bias_ref, out_ref,
                      qh_scr, vh_scr, keys_scr, mbias_scr, o_scr, *, lo, topk):
    nblk = first + n_tiles
    tq = n_tiles * LANES
    wid = nblk * LANES
    w = w_raw * (HEADS ** -0.5)
    tiles = [(slice(r * LANES, (r + 1) * LANES), (first + r + 1) * LANES) for r in range(n_tiles)]
    for r, (rows, wr) in enumerate(tiles):
        t_pos = (first + r) * LANES + lax.broadcasted_iota(I32, (LANES, wr), 0)
        s_pos = lax.broadcasted_iota(I32, (LANES, wr), 1)
        kxt = kidxt_ref[S_KIDX:S_KIDX + HD, 0:wr]
        acc = jnp.zeros((LANES, wr), F32)
        for h in range(HEADS):
            acc = acc + jnp.maximum(_mm(qi[rows, h * HD:(h + 1) * HD], kxt), 0.0) * w[rows, h:h + 1]
        keys_scr[rows, 0:wr] = _sort_key(acc, (s_pos <= t_pos) & (s_pos >= lo))
        if wr < wid:
            keys_scr[rows, wr:wid] = jnp.full((LANES, wid - wr), INT_MIN, I32)

    def count_ge(cand):
        return jnp.sum(jnp.where(keys_scr[:, 0:wid] >= cand[:, 0:1], 1.0, 0.0), axis=1, keepdims=True)

    thr, cnt = _kth_largest_key(count_ge, tq, topk)
    keys = keys_scr[:, 0:wid]
    mbias_scr[:, 0:wid] = jnp.where(keys >= jnp.maximum(thr[:, 0:1], INT_MIN + 1), 0.0, NEG)

    @pl.when(jnp.max(jnp.where(cnt < float(2 ** 30), cnt, 0.0)) > topk)
    def _():
        need = topk - jnp.sum(jnp.where(keys_scr[:, 0:wid] > thr[:, 0:1], 1.0, 0.0), axis=1, keepdims=True)
        tri = _tri_incl()
        running = jnp.zeros((tq, LANES), F32)
        for jb in range(nblk):
            cols = slice(jb * LANES, (jb + 1) * LANES)
            sel, running = _tie_mask(keys_scr[:, cols], thr, need, running, tri)
            mbias_scr[:, cols] = jnp.where(sel, 0.0, NEG)

    def head_attn(h, carry):
        bias = jnp.concatenate(
            [jnp.concatenate([bias_ref[h, max(first + r - jb, 0)] for jb in range(nblk)], axis=1)
             for r in range(n_tiles)], axis=0)
        kt = kt_ref[pl.ds(pl.multiple_of(h * HD, HD), HD), 0:wid]
        lg = _mm(qh_scr[h], kt) + bias + mbias_scr[:, 0:wid]
        p = jnp.exp(lg - jnp.max(lg, axis=1, keepdims=True))
        inv = 1.0 / jnp.sum(p, axis=1, keepdims=True)
        o_scr[h] = _mm(p.astype(BF16), vh_scr[h, 0:wid, :]) * inv
        return carry

    lax.fori_loop(0, HEADS, head_attn, 0)
    for h in range(HEADS):
        out_ref[0, :, h * HD:(h + 1) * HD] = o_scr[h].astype(out_ref.dtype)


def _dsa_prompt_body(*refs, lo, topk, n_tiles, n_steps):
    qb_refs, qi_refs, small_refs = refs[:n_tiles], refs[n_tiles:2 * n_tiles], refs[2 * n_tiles:3 * n_tiles]
    (kt_ref, vbf_ref, kidxt_ref, bias_ref, out_ref, qh_scr, vh_scr, keys_scr, mbias_scr, o_scr) = refs[3 * n_tiles:]
    js = pl.program_id(1)

    @pl.when(js == 0)
    def _():
        for h in range(HEADS):
            vh_scr[h] = vbf_ref[0, :, h * HD:(h + 1) * HD]

    for r in range(n_tiles):
        for h in range(HEADS):
            qh_scr[h, r * LANES:(r + 1) * LANES, :] = qb_refs[r][0, :, h * HD:(h + 1) * HD]
    qi = jnp.concatenate([ref[0] for ref in qi_refs], axis=0)
    w_raw = jnp.concatenate([ref[0][:, S_W:S_W + HEADS] for ref in small_refs], axis=0)
    for step in range(n_steps):
        @pl.when(js == step)
        def _(step=step):
            _dsa_prompt_tiles(1 + step * n_tiles, n_tiles, qi, w_raw, kt_ref, kidxt_ref, bias_ref, out_ref,
                              qh_scr, vh_scr, keys_scr, mbias_scr, o_scr, lo=lo, topk=topk)


DSA_TILES_PER_STEP = 4


def _dsa_prompt(qb, qi, small, kt, vbf, smalltbf, bias_tab, bsz, t_pad, lo, topk):
    n_real = t_pad // LANES - 1
    n_tiles = max(d for d in (1, 2, DSA_TILES_PER_STEP) if n_real % d == 0)
    n_steps = n_real // n_tiles
    rows = n_tiles * LANES
    r3 = lambda a: a.reshape(bsz, t_pad, a.shape[-1])
    tiles = lambda w: [pl.BlockSpec((1, LANES, w), functools.partial(lambda b, js, r: (b, 1 + js * n_tiles + r, 0), r=r))
                       for r in range(n_tiles)]
    seq_cols = lambda r: pl.BlockSpec((r, t_pad), lambda b, js: (0, b))
    return pl.pallas_call(
        functools.partial(_dsa_prompt_body, lo=lo, topk=topk, n_tiles=n_tiles, n_steps=n_steps),
        grid=(bsz, n_steps),
        in_specs=tiles(B_DIM) + tiles(HEADS * HD) + tiles(LANES) + [
            seq_cols(B_DIM), pl.BlockSpec((1, t_pad, B_DIM), lambda b, js: (b, 0, 0)), seq_cols(LANES),
            _const_spec(bias_tab.shape)],
        out_specs=pl.BlockSpec((1, rows, B_DIM), lambda b, js: (b, js, 0)),
        out_shape=jax.ShapeDtypeStruct((bsz, t_pad - FRONT, B_DIM), BF16),
        scratch_shapes=[pltpu.VMEM((HEADS, rows, HD), BF16), pltpu.VMEM((HEADS, t_pad, HD), BF16),
                        pltpu.VMEM((rows, t_pad), I32), pltpu.VMEM((rows, t_pad), F32),
                        pltpu.VMEM((HEADS, rows, HD), F32)],
        compiler_params=pltpu.CompilerParams(dimension_semantics=("parallel", "arbitrary"),
                                             vmem_limit_bytes=VMEM_LIMIT),
        name="dsa_prompt",
    )(*([r3(qb)] * n_tiles), *([r3(qi)] * n_tiles), *([r3(small)] * n_tiles), kt, r3(vbf), smalltbf, bias_tab)


PAGES_PER_STEP = 16


def _dsa_sample_body(pt_ref, *refs, n_new, past_len, topk):
    npp = PAGES_PER_STEP
    kidx_refs, k_refs, v_refs = refs[:npp], refs[npp:2 * npp], refs[2 * npp:3 * npp]
    (qi_ref, qbd_ref, w_ref, knew_ref, vnew_ref, kidxnew_ref, bias_ref, out_ref,
     keys_scr, mb_scr, logit_scr, vt_scr) = refs[3 * npp:]
    del pt_ref
    s = pl.program_id(1)
    n_page_steps = past_len // (PAGE * npp)
    n_blocks = past_len // PAGE + 1
    qi = qi_ref[0]
    qbd = qbd_ref[0]
    w = w_ref[0] * (HEADS ** -0.5)
    q_row = lax.broadcasted_iota(I32, (n_new, LANES), 0)
    lane = lax.broadcasted_iota(I32, (n_new, LANES), 1)

    def do_block(off, kxt, kt, vt, admissible):
        rel = jnp.maximum(_mm(qi, kxt), 0.0) * w
        sc = rel[0:n_new]
        for h in range(1, HEADS):
            sc = sc + rel[h * n_new:(h + 1) * n_new]
        keys_scr[:, pl.ds(off, LANES)] = _sort_key(sc, admissible)
        logit_scr[:, pl.ds(off, LANES)] = _mm(qbd, kt) + bias_ref[:, pl.ds(off, LANES)]
        vt_scr[:, pl.ds(off, LANES)] = vt

    @pl.when(s < n_page_steps)
    def _():
        for i in range(npp):
            off = pl.multiple_of((s * npp + i) * PAGE, PAGE)
            do_block(off, kidx_refs[i][0].astype(BF16), k_refs[i][0].reshape(B_DIM, PAGE).astype(BF16),
                     v_refs[i][0].reshape(B_DIM, PAGE).astype(BF16), True)

    @pl.when(s == n_page_steps)
    def _():
        tr = lambda a: jnp.concatenate(
            [a.astype(F32), jnp.zeros((PAGE - n_new, a.shape[1]), F32)], axis=0).T.astype(BF16)
        do_block(past_len, tr(kidxnew_ref[0])[S_KIDX:S_KIDX + HD], tr(knew_ref[0]), tr(vnew_ref[0]), lane <= q_row)

        keys = keys_scr[...]
        thr, cnt = _kth_largest_key(
            lambda cand: jnp.sum(jnp.where(keys >= cand[:, 0:1], 1.0, 0.0), axis=1, keepdims=True), n_new, topk)
        mb_scr[...] = jnp.where(keys >= jnp.maximum(thr[:, 0:1], INT_MIN + 1), 0.0, NEG)

        @pl.when(jnp.max(jnp.where(cnt < float(2 ** 30), cnt, 0.0)) > topk)
        def _():
            need = topk - jnp.sum(jnp.where(keys > thr[:, 0:1], 1.0, 0.0), axis=1, keepdims=True)
            tri = _tri_incl()

            def mask_blk(jb, running):
                cols = pl.ds(pl.multiple_of(jb * LANES, LANES), LANES)
                sel, running = _tie_mask(keys_scr[:, cols], thr, need, running, tri)
                mb_scr[:, cols] = jnp.where(sel, 0.0, NEG)
                return running

            lax.fori_loop(0, n_blocks, mask_blk, jnp.zeros((n_new, LANES), F32))

        lg = logit_scr[...] + jnp.concatenate([mb_scr[...]] * HEADS, axis=0)
        m = jnp.max(lg, axis=1, keepdims=True)
        p = jnp.exp(lg - m)
        pb = (p * (1.0 / jnp.sum(p, axis=1, keepdims=True))).astype(BF16)
        out_ref[0] = _nt(vt_scr[...], pb)


def _dsa_sample(qb, qi, small, kbf, vbf, smallbf, cache_k, cache_v, cache_kidx, page_table, bias_s, n_new, topk):
    dbs, n_pages = page_table.shape
    past_len = n_pages * PAGE
    npp = PAGES_PER_STEP
    assert n_pages % npp == 0 and n_new == SUBLANES
    n_page_steps = n_pages // npp
    n_blocks = n_pages + 1
    n_pool = cache_k.shape[0]
    qi_rows = qi.reshape(dbs, n_new, HEADS, HD).transpose(0, 2, 1, 3).reshape(dbs, HEADS * n_new, HD)
    q4 = qb.reshape(dbs, n_new, HEADS, HD).transpose(0, 2, 1, 3)
    eye = jnp.eye(HEADS, dtype=qb.dtype)
    qbd = (q4[:, :, :, None, :] * eye[None, :, None, :, None]).reshape(dbs, HEADS * n_new, B_DIM)
    w_rows = small.reshape(dbs, n_new, LANES)[:, :, S_W:S_W + HEADS].transpose(0, 2, 1).reshape(dbs, HEADS * n_new, 1)
    w_rows = jnp.broadcast_to(w_rows, (dbs, HEADS * n_new, LANES))

    def page_spec(shape, i):
        def imap(b, s, pt):
            return (pt[b, jnp.minimum(s, n_page_steps - 1) * npp + i],) + (0,) * len(shape)
        return pl.BlockSpec((1,) + shape, imap)

    per_seq = lambda r, w: pl.BlockSpec((1, r, w), lambda b, s, pt: (b, 0, 0))
    in_specs = ([page_spec((HD, PAGE), i) for i in range(npp)] + [page_spec((HEADS, HD, PAGE), i) for i in range(npp)]
                + [page_spec((HEADS, HD, PAGE), i) for i in range(npp)]
                + [per_seq(HEADS * n_new, HD), per_seq(HEADS * n_new, B_DIM), per_seq(HEADS * n_new, LANES),
                   per_seq(n_new, B_DIM), per_seq(n_new, B_DIM), per_seq(n_new, LANES),
                   pl.BlockSpec(bias_s.shape, lambda b, s, pt: (0, 0), pipeline_mode=pl.Buffered(1))])
    grid_spec = pltpu.PrefetchScalarGridSpec(
        num_scalar_prefetch=1,
        grid=(dbs, n_page_steps + 1),
        in_specs=in_specs,
        out_specs=pl.BlockSpec((1, B_DIM, HEADS * n_new), lambda b, s, pt: (b, 0, 0)),
        scratch_shapes=[pltpu.VMEM((n_new, n_blocks * LANES), I32), pltpu.VMEM((n_new, n_blocks * LANES), F32),
                        pltpu.VMEM((HEADS * n_new, n_blocks * LANES), F32),
                        pltpu.VMEM((B_DIM, n_blocks * LANES), BF16)],
    )
    ck = cache_k.transpose(0, 2, 3, 1)
    cv = cache_v.transpose(0, 2, 3, 1)
    cki = cache_kidx.transpose(0, 2, 1)
    r3 = lambda a: a.reshape(dbs, n_new, a.shape[-1])
    out_t = pl.pallas_call(
        functools.partial(_dsa_sample_body, n_new=n_new, past_len=past_len, topk=topk),
        grid_spec=grid_spec,
        out_shape=jax.ShapeDtypeStruct((dbs, B_DIM, HEADS * n_new), F32),
        compiler_params=pltpu.CompilerParams(dimension_semantics=("parallel", "arbitrary"),
                                             vmem_limit_bytes=VMEM_LIMIT),
        name="dsa_sample",
    )(page_table, *([cki] * npp), *([ck] * npp), *([cv] * npp),
      qi_rows, qbd, w_rows, r3(kbf), r3(vbf), r3(smallbf), bias_s)
    o5 = out_t.reshape(dbs, HEADS, HD, HEADS, n_new)
    o = jnp.stack([o5[:, h, :, h, :] for h in range(HEADS)], axis=1)
    return o.transpose(0, 3, 1, 2).reshape(dbs, n_new, B_DIM).astype(BF16)


def _rms(x, g):
    return x * lax.rsqrt(jnp.mean(x * x, axis=-1, keepdims=True) + EPS) * g


def _mlp_body(x_ref, ma_ref, ob_ref, woa_ref, wob_ref, gffn_ref, wg_ref, wu_ref, wd_ref, gfin_ref, y_ref):
    h = x_ref[...] + (_mm(ma_ref[...], woa_ref[...]) + _mm(ob_ref[...], wob_ref[...]))
    hn = _rms(h, gffn_ref[...]).astype(BF16)
    act = _silu(_mm(hn, wg_ref[...])) * _mm(hn, wu_ref[...])
    y_ref[...] = _rms(h + _mm(act.astype(BF16), wd_ref[...]), gfin_ref[...])


def _mlp(x2d, mix_a, o_b, w_out, g_ffn, w_gate, w_up, w_down, g_final, tm):
    n = x2d.shape[0]
    assert n % tm == 0
    row = lambda w: pl.BlockSpec((tm, w), lambda i: (i, 0))
    woa, wob = w_out[:A_DIM].astype(BF16), w_out[A_DIM:].astype(BF16)
    return pl.pallas_call(
        _mlp_body,
        grid=(n // tm,),
        in_specs=[row(D_MODEL), row(A_DIM), row(B_DIM), _const_spec(woa.shape), _const_spec(wob.shape),
                  _const_spec((1, D_MODEL)), _const_spec(w_gate.shape), _const_spec(w_up.shape),
                  _const_spec(w_down.shape), _const_spec((1, D_MODEL))],
        out_specs=row(D_MODEL),
        out_shape=jax.ShapeDtypeStruct((n, D_MODEL), F32),
        compiler_params=pltpu.CompilerParams(dimension_semantics=("parallel",), vmem_limit_bytes=VMEM_LIMIT),
        name="mlp",
    )(x2d, mix_a, o_b, woa, wob, g_ffn.reshape(1, D_MODEL), w_gate.astype(BF16), w_up.astype(BF16),
      w_down.astype(BF16), g_final.reshape(1, D_MODEL))


def kernel(x_prompt, x_sample, cache_k, cache_v, cache_kidx, state_conv, state_ssm, page_table, meta_tokens, norm_mix, w_in, conv_w, a_log, dt_bias, o_norm, w_out, rel_bias, norm_ffn, w_gate, w_up, w_down, norm_final):
    assert w_in.shape[0] == 1, "single-layer stack"
    bsz, seq, _ = x_prompt.shape
    dbs, n_new, _ = x_sample.shape
    t_pad = FRONT + seq
    lo = FRONT - N_META
    past_len = page_table.shape[1] * PAGE
    topk_p = min(TOPK_MAX, seq // 4)
    topk_s = min(TOPK_MAX, (past_len + n_new) // 4)
    w_parts = _split_w_in(w_in[0])
    mlp_w = (w_out[0], norm_ffn[0], w_gate[0], w_up[0], w_down[0], norm_final)
    gdn_w = (conv_w[0], a_log[0], dt_bias[0], o_norm[0])

    meta = jnp.broadcast_to(meta_tokens.astype(F32)[None], (bsz, N_META, D_MODEL))
    xp = jnp.concatenate([jnp.zeros((bsz, lo, D_MODEL), F32), meta, x_prompt], axis=1).reshape(bsz * t_pad, D_MODEL)
    qkva, z, qb, k, v, _, vbf, qi, small, _, small_t, small_tbf, kt = _proj(xp, norm_mix[0], w_parts, 512)
    mix_a, ssm_p = _gdn_prompt(qkva, z, small, small_t, *gdn_w, bsz, t_pad, lo)
    bias_tab = _bias_tab(rel_bias, t_pad // LANES)
    o_b = _dsa_prompt(qb, qi, small, kt, vbf, small_tbf, bias_tab, bsz, t_pad, lo, topk_p)
    y_p = _mlp(x_prompt.reshape(bsz * seq, D_MODEL), mix_a.reshape(bsz * seq, A_DIM), o_b.reshape(bsz * seq, B_DIM),
               *mlp_w, 512).reshape(bsz, seq, D_MODEL)
    real = lambda a: a.reshape(bsz, t_pad, a.shape[-1])[:, lo:]
    k_p = real(k).reshape(1, bsz, N_META + seq, HEADS, HD)
    v_p = real(v).reshape(1, bsz, N_META + seq, HEADS, HD)
    kidx_p = real(small)[:, :, S_KIDX:S_KIDX + HD][None]
    conv_p = real(qkva)[:, -(CONV_WIDTH - 1):][None]

    xs = x_sample.reshape(dbs * n_new, D_MODEL)
    (qkva_s, z_s, qb_s, k_s, v_s, kbf_s, vbf_s, qi_s, small_s, smallbf_s, small_t_s, _, _) = _proj(
        xs, norm_mix[0], w_parts, 512)
    mix_a_s, ssm_s = _gdn_sample(qkva_s, z_s, small_s, small_t_s, state_conv[0], state_ssm[0], *gdn_w, n_new)
    bias_s = _bias_sample(rel_bias, past_len, past_len // PAGE + 1)
    o_b_s = _dsa_sample(qb_s, qi_s, small_s, kbf_s, vbf_s, smallbf_s, cache_k[0], cache_v[0], cache_kidx[0], page_table,
                        bias_s, n_new, topk_s)
    y_s = _mlp(xs, mix_a_s, o_b_s.reshape(dbs * n_new, B_DIM), *mlp_w, 512).reshape(dbs, n_new, D_MODEL)
    k_sn = k_s.reshape(1, dbs, n_new, HEADS, HD)
    v_sn = v_s.reshape(1, dbs, n_new, HEADS, HD)
    kidx_s = small_s.reshape(dbs, n_new, LANES)[:, :, S_KIDX:S_KIDX + HD][None]
    conv_s = jnp.concatenate([state_conv[0].astype(F32), qkva_s.reshape(dbs, n_new, 3 * A_DIM)],
                             axis=1)[:, -(CONV_WIDTH - 1):][None]
    return (y_p, y_s, k_p, v_p, kidx_p, conv_p, ssm_p[None], k_sn, v_sn, kidx_s, conv_s, ssm_s[None])
```

```python
import functools
import math

import jax
import jax.numpy as jnp
from jax import lax
from jax.experimental import pallas as pl
from jax.experimental.pallas import tpu as pltpu

F32 = jnp.float32
BF16 = jnp.bfloat16
I32 = jnp.int32
HI = lax.Precision.HIGHEST

D_MODEL = 1024
N_META = 16
HEADS = 8
HD = 64
A_DIM = HEADS * HD
B_DIM = HEADS * HD
CONV_WIDTH = 4
GDN_CHUNK = 64
TOPK_MAX = 256
PAGE = 128
REL_BUCKETS = 32
REL_MAX_DIST = 1024
EPS = 1e-6
LANES = 128
SUBLANES = 8
FRONT = 128
NEG = -1e30
INT_MIN = -2 ** 31
V7X_VMEM_BYTES = 64 * 1024 * 1024
VMEM_LIMIT = V7X_VMEM_BYTES * 7 // 8

S_KIDX, S_BETA, S_A, S_W = 0, 64, 72, 80


def _mm(a, b):
    return jnp.dot(a, b, preferred_element_type=F32)


def _mm_hi(a, b):
    return jnp.dot(a, b, preferred_element_type=F32, precision=HI)


def _split_bf16(x):
    hi = x.astype(BF16)
    return hi, (x - hi.astype(F32)).astype(BF16)


def _nt(a, b, precision=None):
    return lax.dot_general(a, b, (((1,), (1,)), ((), ())), preferred_element_type=F32, precision=precision)


def _const_spec(shape):
    nd = len(shape)
    return pl.BlockSpec(shape, lambda *_: (0,) * nd, pipeline_mode=pl.Buffered(1))


def _silu(x):
    return x * jax.nn.sigmoid(x)


def _proj_body(x_ref, g_ref, wa_ref, wz_ref, wb_ref, wqi_ref, ws_ref, wst_ref,
               qkva_ref, z_ref, qb_ref, k_ref, v_ref, kbf_ref, vbf_ref, qi_ref, small_ref, smallbf_ref, smallt_ref,
               smalltbf_ref, kt_ref):
    x = x_ref[...]
    ms = jnp.mean(x * x, axis=-1, keepdims=True)
    h = (x * lax.rsqrt(ms + EPS) * g_ref[...]).astype(BF16)
    qkva_ref[...] = _mm(h, wa_ref[...])
    z_ref[...] = _mm(h, wz_ref[...])
    qkvb = _mm(h, wb_ref[...])
    qb_ref[...] = (qkvb[:, :B_DIM] * 0.125).astype(BF16)
    k = qkvb[:, B_DIM:2 * B_DIM]
    v = qkvb[:, 2 * B_DIM:]
    k_ref[...] = k
    v_ref[...] = v
    kbf_ref[...] = k.astype(BF16)
    vbf_ref[...] = v.astype(BF16)
    qi_ref[...] = (_mm(h, wqi_ref[...]) * 0.125).astype(BF16)
    s = _mm(h, ws_ref[...])
    small_ref[...] = s
    smallbf_ref[...] = s.astype(BF16)
    t = _nt(wst_ref[...], h)
    smallt_ref[...] = t[:LANES]
    smalltbf_ref[...] = t[:LANES].astype(BF16)
    kt_ref[...] = t[LANES:].astype(BF16)


def _split_w_in(w_in):
    c = 0
    parts = []
    for n in (3 * A_DIM, A_DIM, HEADS, HEADS, 3 * B_DIM, HEADS * HD, HD, HEADS):
        parts.append(w_in[:, c:c + n])
        c += n
    w_qkva, w_z, w_beta, w_a, w_qkvb, w_qi, w_kidx, w_w = parts
    pad = jnp.zeros((w_in.shape[0], LANES - (HD + 3 * HEADS)), w_in.dtype)
    w_small = jnp.concatenate([w_kidx, w_beta, w_a, w_w, pad], axis=1)
    bf = lambda t: t.astype(BF16)
    w_t = jnp.concatenate([w_small, w_qkvb[:, B_DIM:2 * B_DIM]], axis=1).T
    return bf(w_qkva), bf(w_z), bf(w_qkvb), bf(w_qi), bf(w_small), bf(w_t)


def _proj(x2d, g, w_parts, tm):
    n = x2d.shape[0]
    assert n % tm == 0 and tm % LANES == 0
    wa, wz, wb, wqi, ws, wst = w_parts
    row = lambda w: pl.BlockSpec((tm, w), lambda i: (i, 0))
    outs = [
        (jax.ShapeDtypeStruct((n, 3 * A_DIM), F32), row(3 * A_DIM)),
        (jax.ShapeDtypeStruct((n, A_DIM), F32), row(A_DIM)),
        (jax.ShapeDtypeStruct((n, B_DIM), BF16), row(B_DIM)),
        (jax.ShapeDtypeStruct((n, B_DIM), F32), row(B_DIM)),
        (jax.ShapeDtypeStruct((n, B_DIM), F32), row(B_DIM)),
        (jax.ShapeDtypeStruct((n, B_DIM), BF16), row(B_DIM)),
        (jax.ShapeDtypeStruct((n, B_DIM), BF16), row(B_DIM)),
        (jax.ShapeDtypeStruct((n, HEADS * HD), BF16), row(HEADS * HD)),
        (jax.ShapeDtypeStruct((n, LANES), F32), row(LANES)),
        (jax.ShapeDtypeStruct((n, LANES), BF16), row(LANES)),
        (jax.ShapeDtypeStruct((LANES, n), F32), pl.BlockSpec((LANES, tm), lambda i: (0, i))),
        (jax.ShapeDtypeStruct((LANES, n), BF16), pl.BlockSpec((LANES, tm), lambda i: (0, i))),
        (jax.ShapeDtypeStruct((B_DIM, n), BF16), pl.BlockSpec((B_DIM, tm), lambda i: (0, i))),
    ]
    return pl.pallas_call(
        _proj_body,
        grid=(n // tm,),
        in_specs=[row(D_MODEL), _const_spec((1, D_MODEL)), _const_spec(wa.shape), _const_spec(wz.shape),
                  _const_spec(wb.shape), _const_spec(wqi.shape), _const_spec(ws.shape), _const_spec(wst.shape)],
        out_specs=[o[1] for o in outs],
        out_shape=[o[0] for o in outs],
        compiler_params=pltpu.CompilerParams(dimension_semantics=("parallel",), vmem_limit_bytes=VMEM_LIMIT),
        name="proj",
    )(x2d, g.reshape(1, D_MODEL), wa, wz, wb, wqi, ws, wst)


def _gdn_prep(qkv, tail, conv_w, small, small_t, a_log_row, dt_row, a_log_col, dt_col, valid_col, valid_row, chunk):
    rows = qkv.shape[0]
    n_chunks = rows // chunk
    ys = []
    for c in range(n_chunks):
        ext = jnp.concatenate([tail[c], qkv[c * chunk:(c + 1) * chunk]], axis=0)
        y = sum(ext[SUBLANES - (CONV_WIDTH - 1) + j: SUBLANES - (CONV_WIDTH - 1) + j + chunk] * conv_w[j:j + 1]
                for j in range(CONV_WIDTH))
        ys.append(_silu(y))
    y = ys[0] if n_chunks == 1 else jnp.concatenate(ys, axis=0)
    beta_c = jnp.where(valid_col, jax.nn.sigmoid(small), 0.0)
    beta_r = jnp.where(valid_row, jax.nn.sigmoid(small_t), 0.0)
    g_c = jnp.where(valid_col, -jnp.exp(a_log_row) * jax.nn.softplus(small + dt_row), 0.0)
    g_r = jnp.where(valid_row, -jnp.exp(a_log_col) * jax.nn.softplus(small_t + dt_col), 0.0)
    ri = lax.broadcasted_iota(I32, (rows, rows), 0)
    ci = lax.broadcasted_iota(I32, (rows, rows), 1)
    same = (ri // chunk) == (ci // chunk)
    low = jnp.where(same & (ci <= ri), 1.0, 0.0).astype(F32)
    b_c = _mm_hi(low, g_c)
    b_r = _nt(g_r, low, precision=HI)
    return y, beta_c, b_c, b_r


def _bdot(a, b, ca, cb):
    return lax.dot_general(a, b, (((ca,), (cb,)), ((0,), (0,))), preferred_element_type=F32)


def _bmm_solve(a, b):
    a_hi, a_lo = _split_bf16(a)
    b_hi, b_lo = _split_bf16(b)
    return _bdot(a_hi, b_hi, 2, 1) + (_bdot(a_hi, b_lo, 2, 1) + _bdot(a_lo, b_hi, 2, 1))


def _gdn_solve(q, k, v, beta, b_c, b_r, chunk):
    c = chunk
    qn = q * lax.rsqrt(jnp.sum(q * q, axis=-1, keepdims=True) + EPS) * (HD ** -0.5)
    kn = k * lax.rsqrt(jnp.sum(k * k, axis=-1, keepdims=True) + EPS)
    ri = lax.broadcasted_iota(I32, (1, c, c), 1)
    ci = lax.broadcasted_iota(I32, (1, c, c), 2)
    incl = ci <= ri
    decay = jnp.where(incl, jnp.exp(jnp.where(incl, b_c - b_r, 0.0)), 0.0)
    kb, qb16 = kn.astype(BF16), qn.astype(BF16)
    kk = _bdot(kb, kb, 2, 2)
    qk = _bdot(qb16, kb, 2, 2)
    bk = -jnp.where(ci < ri, beta * kk * decay, 0.0)
    eb = jnp.exp(b_c)
    y = jnp.concatenate([beta * v, (beta * eb) * kn], axis=2)
    n_levels = int(math.log2(c))
    for lvl in range(n_levels):
        if lvl + 1 < n_levels:
            prod = _bmm_solve(bk, jnp.concatenate([y, bk], axis=2))
            y, bk = y + prod[:, :, :2 * HD], prod[:, :, 2 * HD:]
        else:
            y = y + _bmm_solve(bk, y)
    b_last = b_c[:, c - 1:c, :]
    return (y[:, :, :HD], y[:, :, HD:].astype(BF16), (qk * decay).astype(BF16), qb16,
            (kn * jnp.exp(b_last - b_c)).astype(BF16), eb, jnp.exp(b_last))


def _gdn_apply(u, w, attn, qb16, kd, eb, eb_last, s0, o_norm, z):
    s0b = s0.astype(BF16)
    delta = u - _bdot(w, s0b, 2, 1)
    deltab = delta.astype(BF16)
    o = eb * _bdot(qb16, s0b, 2, 1) + _bdot(attn, deltab, 2, 1)
    s_new = eb_last * s0 + _bdot(kd, deltab, 1, 1)
    on = o * lax.rsqrt(jnp.mean(o * o, axis=-1, keepdims=True) + EPS) * o_norm
    return s_new, on * _silu(z)


def _gdn_prompt_body(*refs, rows, lo, nb):
    qkv_ref, z_ref, small_ref = refs[:3]
    smallt_refs = refs[3:3 + nb]
    convw_ref, alr_ref, dtr_ref, alc_ref, dtc_ref, onorm_ref, mix_ref, s_ref, tail_scr = refs[3 + nb:]
    t = pl.program_id(1)

    @pl.when(t == 0)
    def _():
        s_ref[...] = jnp.zeros_like(s_ref)
        tail_scr[...] = jnp.zeros_like(tail_scr)

    chunk = GDN_CHUNK
    n_chunks = rows // chunk
    pos_c = t * rows + lax.broadcasted_iota(I32, (rows, LANES), 0)
    pos_r = t * rows + lax.broadcasted_iota(I32, (LANES, rows), 1)
    preps = []
    for s in range(nb):
        qkv = qkv_ref[s]
        tails = [tail_scr[s]] + [qkv[c * chunk - SUBLANES:c * chunk] for c in range(1, n_chunks)]
        preps.append(_gdn_prep(qkv, tails, convw_ref[...], small_ref[s], smallt_refs[s][...],
                               alr_ref[...], dtr_ref[...], alc_ref[...], dtc_ref[...],
                               pos_c >= lo, pos_r >= lo, chunk) + (z_ref[s],))
        tail_scr[s] = qkv[rows - SUBLANES:rows]
    units = [(c, s, h) for c in range(n_chunks) for s in range(nb) for h in range(HEADS)]
    rs = lambda c: slice(c * chunk, (c + 1) * chunk)
    stack = lambda k, off: jnp.stack([preps[s][k][rs(c), off + h * HD:off + (h + 1) * HD] for c, s, h in units])
    col = lambda k, off: jnp.stack([preps[s][k][rs(c), off + h:off + h + 1] for c, s, h in units])
    sol = _gdn_solve(stack(0, 0), stack(0, A_DIM), stack(0, 2 * A_DIM), col(1, S_BETA), col(2, S_A),
                     jnp.stack([preps[s][3][S_A + h:S_A + h + 1, rs(c)] for c, s, h in units]), chunk)
    z3 = stack(4, 0)
    o_norm = onorm_ref[...]
    per_chunk = nb * HEADS
    for c in range(n_chunks):
        g = slice(c * per_chunk, (c + 1) * per_chunk)
        s_new, out = _gdn_apply(*[a[g] for a in sol], s_ref[...].reshape(per_chunk, HD, HD), o_norm, z3[g])
        s_ref[...] = s_new.reshape(nb, HEADS, HD, HD)
        for s in range(nb):
            for h in range(HEADS):
                mix_ref[s, rs(c), h * HD:(h + 1) * HD] = out[s * HEADS + h].astype(mix_ref.dtype)


GDN_SEQS_PER_STEP = 2


def _gate_params(a_log, dt_bias):
    row = lambda v: jnp.zeros((1, LANES), F32).at[0, S_A:S_A + HEADS].set(v.astype(F32))
    return row(a_log), row(dt_bias), row(a_log).T, row(dt_bias).T


def _gdn_prompt(qkva, z, small, small_t, conv_w, a_log, dt_bias, o_norm, bsz, t_pad, lo):
    rows = LANES
    nt = t_pad // rows
    nb = GDN_SEQS_PER_STEP if bsz % GDN_SEQS_PER_STEP == 0 else 1
    alr, dtr, alc, dtc = _gate_params(a_log, dt_bias)
    body = functools.partial(_gdn_prompt_body, rows=rows, lo=lo, nb=nb)
    seqs = lambda w: pl.BlockSpec((nb, rows, w), lambda b, t: (b, t, 0))
    mix, s = pl.pallas_call(
        body,
        grid=(bsz // nb, nt),
        in_specs=[seqs(3 * A_DIM), seqs(A_DIM), seqs(LANES)] + [
            pl.BlockSpec((LANES, rows), functools.partial(lambda b, t, s: (0, (b * nb + s) * nt + t), s=s))
            for s in range(nb)] + [
            _const_spec((CONV_WIDTH, 3 * A_DIM)), _const_spec((1, LANES)), _const_spec((1, LANES)),
            _const_spec((LANES, 1)), _const_spec((LANES, 1)), _const_spec((1, HD)),
        ],
        out_specs=[
            pl.BlockSpec((nb, rows, A_DIM), lambda b, t: (b, jnp.maximum(t - 1, 0), 0)),
            pl.BlockSpec((nb, HEADS, HD, HD), lambda b, t: (b, 0, 0, 0)),
        ],
        out_shape=[jax.ShapeDtypeStruct((bsz, t_pad - FRONT, A_DIM), BF16),
                   jax.ShapeDtypeStruct((bsz, HEADS, HD, HD), F32)],
        scratch_shapes=[pltpu.VMEM((nb, SUBLANES, 3 * A_DIM), F32)],
        compiler_params=pltpu.CompilerParams(dimension_semantics=("parallel", "arbitrary"),
                                             vmem_limit_bytes=VMEM_LIMIT),
        name="gdn_prompt",
    )(qkva.reshape(bsz, t_pad, 3 * A_DIM), z.reshape(bsz, t_pad, A_DIM), small.reshape(bsz, t_pad, LANES),
      *([small_t] * nb), conv_w, alr, dtr, alc, dtc, o_norm.reshape(1, HD))
    return mix, s


def _gdn_sample_body(qkv_ref, z_ref, small_ref, smallt_ref, conv0_ref, s0_ref, convw_ref, alr_ref, dtr_ref, alc_ref,
                     dtc_ref, onorm_ref, mix_ref, s_ref, *, chunk):
    rows = LANES
    n_seq = rows // chunk
    qkv = qkv_ref[...]
    tails = [conv0_ref[i] for i in range(n_seq)]
    y, beta_c, b_c, b_r = _gdn_prep(qkv, tails, convw_ref[...], small_ref[...], smallt_ref[...],
                                    alr_ref[...], dtr_ref[...], alc_ref[...], dtc_ref[...], True, True, chunk)
    z = z_ref[...]
    per_seq = lambda a: a.reshape(n_seq, chunk, a.shape[-1])
    stack = lambda a, off: jnp.concatenate([per_seq(a[:, off + h * HD:off + (h + 1) * HD]) for h in range(HEADS)])
    col = lambda a, off: jnp.concatenate([per_seq(a[:, off + h:off + h + 1]) for h in range(HEADS)])
    b_r3 = jnp.stack([b_r[S_A + h:S_A + h + 1, i * chunk:(i + 1) * chunk] for h in range(HEADS) for i in range(n_seq)])
    sol = _gdn_solve(stack(y, 0), stack(y, A_DIM), stack(y, 2 * A_DIM), col(beta_c, S_BETA), col(b_c, S_A), b_r3, chunk)
    s0 = jnp.concatenate([s0_ref[:, h] for h in range(HEADS)])
    s_new, out = _gdn_apply(*sol, s0, onorm_ref[...], stack(z, 0))
    for h in range(HEADS):
        g = slice(h * n_seq, (h + 1) * n_seq)
        s_ref[:, h] = s_new[g]
        mix_ref[:, h * HD:(h + 1) * HD] = out[g].reshape(rows, HD).astype(mix_ref.dtype)


def _gdn_sample(qkva, z, small, small_t, state_conv, state_ssm, conv_w, a_log, dt_bias, o_norm, n_new):
    n = qkva.shape[0]
    bsz = n // n_new
    rows = LANES
    n_seq = rows // n_new
    assert n_new == SUBLANES and n % rows == 0
    alr, dtr, alc, dtc = _gate_params(a_log, dt_bias)
    conv0 = jnp.pad(state_conv.astype(F32), ((0, 0), (SUBLANES - (CONV_WIDTH - 1), 0), (0, 0)))
    body = functools.partial(_gdn_sample_body, chunk=n_new)
    mix, s = pl.pallas_call(
        body,
        grid=(n // rows,),
        in_specs=[
            pl.BlockSpec((rows, 3 * A_DIM), lambda g: (g, 0)),
            pl.BlockSpec((rows, A_DIM), lambda g: (g, 0)),
            pl.BlockSpec((rows, LANES), lambda g: (g, 0)),
            pl.BlockSpec((LANES, rows), lambda g: (0, g)),
            pl.BlockSpec((n_seq, SUBLANES, 3 * A_DIM), lambda g: (g, 0, 0)),
            pl.BlockSpec((n_seq, HEADS, HD, HD), lambda g: (g, 0, 0, 0)),
            _const_spec((CONV_WIDTH, 3 * A_DIM)), _const_spec((1, LANES)), _const_spec((1, LANES)),
            _const_spec((LANES, 1)), _const_spec((LANES, 1)), _const_spec((1, HD)),
        ],
        out_specs=[
            pl.BlockSpec((rows, A_DIM), lambda g: (g, 0)),
            pl.BlockSpec((n_seq, HEADS, HD, HD), lambda g: (g, 0, 0, 0)),
        ],
        out_shape=[jax.ShapeDtypeStruct((n, A_DIM), BF16), jax.ShapeDtypeStruct((bsz, HEADS, HD, HD), F32)],
        compiler_params=pltpu.CompilerParams(dimension_semantics=("parallel",), vmem_limit_bytes=VMEM_LIMIT),
        name="gdn_sample",
    )(qkva, z, small, small_t, conv0, state_ssm.astype(F32), conv_w, alr, dtr, alc, dtc, o_norm.reshape(1, HD))
    return mix, s


def _rel_bucket(d):
    d = jnp.maximum(d, 0)
    max_exact = REL_BUCKETS // 2
    df = jnp.maximum(d, 1).astype(F32)
    large = max_exact + (jnp.log(df / max_exact) / math.log(REL_MAX_DIST / max_exact)
                         * (REL_BUCKETS - max_exact)).astype(I32)
    large = jnp.minimum(large, REL_BUCKETS - 1)
    return jnp.where(d < max_exact, d, large)


def _bias_lookup(bucket, rb_ref, h):
    acc = jnp.zeros(bucket.shape, F32)
    for b in range(REL_BUCKETS):
        acc = jnp.where(bucket == b, rb_ref[b, h], acc)
    return acc


def _bias_tab_body(rb_ref, out_ref):
    dlt = pl.program_id(0)
    i = lax.broadcasted_iota(I32, (LANES, LANES), 0)
    j = lax.broadcasted_iota(I32, (LANES, LANES), 1)
    bucket = _rel_bucket(dlt * LANES + i - j)
    for h in range(HEADS):
        out_ref[h, 0] = _bias_lookup(bucket, rb_ref, h)


def _bias_tab(rel_bias, n_tiles):
    return pl.pallas_call(
        _bias_tab_body,
        grid=(n_tiles,),
        in_specs=[pl.BlockSpec(memory_space=pltpu.SMEM)],
        out_specs=pl.BlockSpec((HEADS, 1, LANES, LANES), lambda t: (0, t, 0, 0)),
        out_shape=jax.ShapeDtypeStruct((HEADS, n_tiles, LANES, LANES), F32),
        name="bias_tab",
    )(rel_bias.astype(F32))


def _bias_sample_body(rb_ref, out_ref, *, past_len):
    blk = pl.program_id(0)
    q = lax.broadcasted_iota(I32, (SUBLANES, LANES), 0)
    s = blk * LANES + lax.broadcasted_iota(I32, (SUBLANES, LANES), 1)
    bucket = _rel_bucket(past_len + q - s)
    for h in range(HEADS):
        out_ref[h * SUBLANES:(h + 1) * SUBLANES, :] = _bias_lookup(bucket, rb_ref, h)


def _bias_sample(rel_bias, past_len, n_blocks):
    return pl.pallas_call(
        functools.partial(_bias_sample_body, past_len=past_len),
        grid=(n_blocks,),
        in_specs=[pl.BlockSpec(memory_space=pltpu.SMEM)],
        out_specs=pl.BlockSpec((HEADS * SUBLANES, LANES), lambda t: (0, t)),
        out_shape=jax.ShapeDtypeStruct((HEADS * SUBLANES, n_blocks * LANES), F32),
        name="bias_sample",
    )(rel_bias.astype(F32))


def _sort_key(score, admissible):
    score = jnp.where(score == 0.0, 0.0, score)
    bits = pltpu.bitcast(score, I32)
    key = jnp.where(bits < 0, bits ^ 0x7FFFFFFF, bits)
    return jnp.where(admissible, key, INT_MIN)


def _kth_largest_key(count_ge, rows, topk, two_bits=False):
    count = lambda cand: jnp.broadcast_to(count_ge(cand), (rows, LANES))
    c0 = count(jnp.zeros((rows, LANES), I32))
    t0 = jnp.where(c0 >= topk, 0, INT_MIN).astype(I32)
    c0 = jnp.where(c0 >= topk, c0, float(2 ** 30))

    def accept(carry, cand, cc):
        t, c = carry
        ok = cc >= topk
        return jnp.where(ok, cand, t), jnp.where(ok, cc, c)

    def bit_step(i, carry):
        cand = carry[0] + lax.shift_left(jnp.int32(1), 30 - i)
        return accept(carry, cand, count(cand))

    def two_bit_step(i, carry):
        unit = lax.shift_left(jnp.int32(1), 29 - 2 * i)
        cands = [carry[0] + m * unit for m in (1, 2, 3)]
        counts = [count(cand) for cand in cands]
        for cand, cc in zip(cands, counts):
            carry = accept(carry, cand, cc)
        return carry

    if not two_bits:
        return lax.fori_loop(0, 31, bit_step, (t0, c0))
    carry = lax.fori_loop(0, 15, two_bit_step, (t0, c0))
    return bit_step(30, carry)


def _tie_mask(key, thr, need, running, tri):
    eq = (key == thr) & (key != INT_MIN)
    pref = _mm(jnp.where(eq, 1.0, 0.0).astype(BF16), tri) + running
    sel = (key > thr) | (eq & (pref <= need))
    return sel, jnp.broadcast_to(pref[:, LANES - 1:LANES], pref.shape)


def _tri_incl():
    r = lax.broadcasted_iota(I32, (LANES, LANES), 0)
    c = lax.broadcasted_iota(I32, (LANES, LANES), 1)
    return jnp.where(r <= c, 1.0, 0.0).astype(BF16)


def _dsa_prompt_tiles(first, n_tiles, qi, w_raw, kt_ref, kidxt_ref, bias_ref, out_ref,
                      qh_scr, vh_scr, keys_scr, mbias_scr, o_scr, *, lo, topk):
    nblk = first + n_tiles
    tq = n_tiles * LANES
    wid = nblk * LANES
    w = w_raw * (HEADS ** -0.5)
    tiles = [(slice(r * LANES, (r + 1) * LANES), (first + r + 1) * LANES) for r in range(n_tiles)]
    for r, (rows, wr) in enumerate(tiles):
        t_pos = (first + r) * LANES + lax.broadcasted_iota(I32, (LANES, wr), 0)
        s_pos = lax.broadcasted_iota(I32, (LANES, wr), 1)
        kxt = kidxt_ref[S_KIDX:S_KIDX + HD, 0:wr]
        acc = jnp.zeros((LANES, wr), F32)
        for h in range(HEADS):
            acc = acc + jnp.maximum(_mm(qi[rows, h * HD:(h + 1) * HD], kxt), 0.0) * w[rows, h:h + 1]
        keys_scr[rows, 0:wr] = _sort_key(acc, (s_pos <= t_pos) & (s_pos >= lo))
        if wr < wid:
            keys_scr[rows, wr:wid] = jnp.full((LANES, wid - wr), INT_MIN, I32)

    def count_ge(cand):
        return jnp.sum(jnp.where(keys_scr[:, 0:wid] >= cand[:, 0:1], 1.0, 0.0), axis=1, keepdims=True)

    thr, cnt = _kth_largest_key(count_ge, tq, topk)
    keys = keys_scr[:, 0:wid]
    mbias_scr[:, 0:wid] = jnp.where(keys >= jnp.maximum(thr[:, 0:1], INT_MIN + 1), 0.0, NEG)

    @pl.when(jnp.max(jnp.where(cnt < float(2 ** 30), cnt, 0.0)) > topk)
    def _():
        need = topk - jnp.sum(jnp.where(keys_scr[:, 0:wid] > thr[:, 0:1], 1.0, 0.0), axis=1, keepdims=True)
        tri = _tri_incl()
        running = jnp.zeros((tq, LANES), F32)
        for jb in range(nblk):
            cols = slice(jb * LANES, (jb + 1) * LANES)
            sel, running = _tie_mask(keys_scr[:, cols], thr, need, running, tri)
            mbias_scr[:, cols] = jnp.where(sel, 0.0, NEG)

    def head_attn(h, carry):
        bias = jnp.concatenate(
            [jnp.concatenate([bias_ref[h, max(first + r - jb, 0)] for jb in range(nblk)], axis=1)
             for r in range(n_tiles)], axis=0)
        kt = kt_ref[pl.ds(pl.multiple_of(h * HD, HD), HD), 0:wid]
        lg = _mm(qh_scr[h], kt) + bias + mbias_scr[:, 0:wid]
        p = jnp.exp(lg - jnp.max(lg, axis=1, keepdims=True))
        inv = 1.0 / jnp.sum(p, axis=1, keepdims=True)
        o_scr[h] = _mm(p.astype(BF16), vh_scr[h, 0:wid, :]) * inv
        return carry

    lax.fori_loop(0, HEADS, head_attn, 0)
    for h in range(HEADS):
        out_ref[0, :, h * HD:(h + 1) * HD] = o_scr[h].astype(out_ref.dtype)


def _dsa_prompt_body(*refs, lo, topk, n_tiles, n_steps):
    qb_refs, qi_refs, small_refs = refs[:n_tiles], refs[n_tiles:2 * n_tiles], refs[2 * n_tiles:3 * n_tiles]
    (kt_ref, vbf_ref, kidxt_ref, bias_ref, out_ref, qh_scr, vh_scr, keys_scr, mbias_scr, o_scr) = refs[3 * n_tiles:]
    js = pl.program_id(1)

    @pl.when(js == 0)
    def _():
        for h in range(HEADS):
            vh_scr[h] = vbf_ref[0, :, h * HD:(h + 1) * HD]

    for r in range(n_tiles):
        for h in range(HEADS):
            qh_scr[h, r * LANES:(r + 1) * LANES, :] = qb_refs[r][0, :, h * HD:(h + 1) * HD]
    qi = jnp.concatenate([ref[0] for ref in qi_refs], axis=0)
    w_raw = jnp.concatenate([ref[0][:, S_W:S_W + HEADS] for ref in small_refs], axis=0)
    for step in range(n_steps):
        @pl.when(js == step)
        def _(step=step):
            _dsa_prompt_tiles(1 + step * n_tiles, n_tiles, qi, w_raw, kt_ref, kidxt_ref, bias_ref, out_ref,
                              qh_scr, vh_scr, keys_scr, mbias_scr, o_scr, lo=lo, topk=topk)


DSA_TILES_PER_STEP = 4


def _dsa_prompt(qb, qi, small, kt, vbf, smalltbf, bias_tab, bsz, t_pad, lo, topk):
    n_real = t_pad // LANES - 1
    n_tiles = max(d for d in (1, 2, DSA_TILES_PER_STEP) if n_real % d == 0)
    n_steps = n_real // n_tiles
    rows = n_tiles * LANES
    r3 = lambda a: a.reshape(bsz, t_pad, a.shape[-1])
    tiles = lambda w: [pl.BlockSpec((1, LANES, w), functools.partial(lambda b, js, r: (b, 1 + js * n_tiles + r, 0), r=r))
                       for r in range(n_tiles)]
    seq_cols = lambda r: pl.BlockSpec((r, t_pad), lambda b, js: (0, b))
    return pl.pallas_call(
        functools.partial(_dsa_prompt_body, lo=lo, topk=topk, n_tiles=n_tiles, n_steps=n_steps),
        grid=(bsz, n_steps),
        in_specs=tiles(B_DIM) + tiles(HEADS * HD) + tiles(LANES) + [
            seq_cols(B_DIM), pl.BlockSpec((1, t_pad, B_DIM), lambda b, js: (b, 0, 0)), seq_cols(LANES),
            _const_spec(bias_tab.shape)],
        out_specs=pl.BlockSpec((1, rows, B_DIM), lambda b, js: (b, js, 0)),
        out_shape=jax.ShapeDtypeStruct((bsz, t_pad - FRONT, B_DIM), BF16),
        scratch_shapes=[pltpu.VMEM((HEADS, rows, HD), BF16), pltpu.VMEM((HEADS, t_pad, HD), BF16),
                        pltpu.VMEM((rows, t_pad), I32), pltpu.VMEM((rows, t_pad), F32),
                        pltpu.VMEM((HEADS, rows, HD), F32)],
        compiler_params=pltpu.CompilerParams(dimension_semantics=("parallel", "arbitrary"),
                                             vmem_limit_bytes=VMEM_LIMIT),
        name="dsa_prompt",
    )(*([r3(qb)] * n_tiles), *([r3(qi)] * n_tiles), *([r3(small)] * n_tiles), kt, r3(vbf), smalltbf, bias_tab)


PAGES_PER_STEP = 16


def _dsa_sample_body(pt_ref, *refs, n_new, past_len, topk):
    npp = PAGES_PER_STEP
    kidx_refs, k_refs, v_refs = refs[:npp], refs[npp:2 * npp], refs[2 * npp:3 * npp]
    (qi_ref, qbd_ref, w_ref, knew_ref, vnew_ref, kidxnew_ref, bias_ref, out_ref,
     keys_scr, mb_scr, logit_scr, vt_scr) = refs[3 * npp:]
    del pt_ref
    s = pl.program_id(1)
    n_page_steps = past_len // (PAGE * npp)
    n_blocks = past_len // PAGE + 1
    qi = qi_ref[0]
    qbd = qbd_ref[0]
    w = w_ref[0] * (HEADS ** -0.5)
    q_row = lax.broadcasted_iota(I32, (n_new, LANES), 0)
    lane = lax.broadcasted_iota(I32, (n_new, LANES), 1)

    def do_block(off, kxt, kt, vt, admissible):
        rel = jnp.maximum(_mm(qi, kxt), 0.0) * w
        sc = rel[0:n_new]
        for h in range(1, HEADS):
            sc = sc + rel[h * n_new:(h + 1) * n_new]
        keys_scr[:, pl.ds(off, LANES)] = _sort_key(sc, admissible)
        logit_scr[:, pl.ds(off, LANES)] = _mm(qbd, kt) + bias_ref[:, pl.ds(off, LANES)]
        vt_scr[:, pl.ds(off, LANES)] = vt

    @pl.when(s < n_page_steps)
    def _():
        for i in range(npp):
            off = pl.multiple_of((s * npp + i) * PAGE, PAGE)
            do_block(off, kidx_refs[i][0].astype(BF16), k_refs[i][0].reshape(B_DIM, PAGE).astype(BF16),
                     v_refs[i][0].reshape(B_DIM, PAGE).astype(BF16), True)

    @pl.when(s == n_page_steps)
    def _():
        tr = lambda a: jnp.concatenate(
            [a.astype(F32), jnp.zeros((PAGE - n_new, a.shape[1]), F32)], axis=0).T.astype(BF16)
        do_block(past_len, tr(kidxnew_ref[0])[S_KIDX:S_KIDX + HD], tr(knew_ref[0]), tr(vnew_ref[0]), lane <= q_row)

        keys = keys_scr[...]
        thr, cnt = _kth_largest_key(
            lambda cand: jnp.sum(jnp.where(keys >= cand[:, 0:1], 1.0, 0.0), axis=1, keepdims=True), n_new, topk,
            two_bits=True)
        mb_scr[...] = jnp.where(keys >= jnp.maximum(thr[:, 0:1], INT_MIN + 1), 0.0, NEG)

        @pl.when(jnp.max(jnp.where(cnt < float(2 ** 30), cnt, 0.0)) > topk)
        def _():
            need = topk - jnp.sum(jnp.where(keys > thr[:, 0:1], 1.0, 0.0), axis=1, keepdims=True)
            tri = _tri_incl()

            def mask_blk(jb, running):
                cols = pl.ds(pl.multiple_of(jb * LANES, LANES), LANES)
                sel, running = _tie_mask(keys_scr[:, cols], thr, need, running, tri)
                mb_scr[:, cols] = jnp.where(sel, 0.0, NEG)
                return running

            lax.fori_loop(0, n_blocks, mask_blk, jnp.zeros((n_new, LANES), F32))

        lg = logit_scr[...] + jnp.concatenate([mb_scr[...]] * HEADS, axis=0)
        m = jnp.max(lg, axis=1, keepdims=True)
        p = jnp.exp(lg - m)
        pb = (p * (1.0 / jnp.sum(p, axis=1, keepdims=True))).astype(BF16)
        out_ref[0] = _nt(vt_scr[...], pb)


def _dsa_sample(qb, qi, small, kbf, vbf, smallbf, cache_k, cache_v, cache_kidx, page_table, bias_s, n_new, topk):
    dbs, n_pages = page_table.shape
    past_len = n_pages * PAGE
    npp = PAGES_PER_STEP
    assert n_pages % npp == 0 and n_new == SUBLANES
    n_page_steps = n_pages // npp
    n_blocks = n_pages + 1
    n_pool = cache_k.shape[0]
    qi_rows = qi.reshape(dbs, n_new, HEADS, HD).transpose(0, 2, 1, 3).reshape(dbs, HEADS * n_new, HD)
    q4 = qb.reshape(dbs, n_new, HEADS, HD).transpose(0, 2, 1, 3)
    eye = jnp.eye(HEADS, dtype=qb.dtype)
    qbd = (q4[:, :, :, None, :] * eye[None, :, None, :, None]).reshape(dbs, HEADS * n_new, B_DIM)
    w_rows = small.reshape(dbs, n_new, LANES)[:, :, S_W:S_W + HEADS].transpose(0, 2, 1).reshape(dbs, HEADS * n_new, 1)
    w_rows = jnp.broadcast_to(w_rows, (dbs, HEADS * n_new, LANES))

    def page_spec(shape, i):
        def imap(b, s, pt):
            return (pt[b, jnp.minimum(s, n_page_steps - 1) * npp + i],) + (0,) * len(shape)
        return pl.BlockSpec((1,) + shape, imap)

    per_seq = lambda r, w: pl.BlockSpec((1, r, w), lambda b, s, pt: (b, 0, 0))
    in_specs = ([page_spec((HD, PAGE), i) for i in range(npp)] + [page_spec((HEADS, HD, PAGE), i) for i in range(npp)]
                + [page_spec((HEADS, HD, PAGE), i) for i in range(npp)]
                + [per_seq(HEADS * n_new, HD), per_seq(HEADS * n_new, B_DIM), per_seq(HEADS * n_new, LANES),
                   per_seq(n_new, B_DIM), per_seq(n_new, B_DIM), per_seq(n_new, LANES),
                   pl.BlockSpec(bias_s.shape, lambda b, s, pt: (0, 0), pipeline_mode=pl.Buffered(1))])
    grid_spec = pltpu.PrefetchScalarGridSpec(
        num_scalar_prefetch=1,
        grid=(dbs, n_page_steps + 1),
        in_specs=in_specs,
        out_specs=pl.BlockSpec((1, B_DIM, HEADS * n_new), lambda b, s, pt: (b, 0, 0)),
        scratch_shapes=[pltpu.VMEM((n_new, n_blocks * LANES), I32), pltpu.VMEM((n_new, n_blocks * LANES), F32),
                        pltpu.VMEM((HEADS * n_new, n_blocks * LANES), F32),
                        pltpu.VMEM((B_DIM, n_blocks * LANES), BF16)],
    )
    ck = cache_k.transpose(0, 2, 3, 1)
    cv = cache_v.transpose(0, 2, 3, 1)
    cki = cache_kidx.transpose(0, 2, 1)
    r3 = lambda a: a.reshape(dbs, n_new, a.shape[-1])
    out_t = pl.pallas_call(
        functools.partial(_dsa_sample_body, n_new=n_new, past_len=past_len, topk=topk),
        grid_spec=grid_spec,
        out_shape=jax.ShapeDtypeStruct((dbs, B_DIM, HEADS * n_new), F32),
        compiler_params=pltpu.CompilerParams(dimension_semantics=("parallel", "arbitrary"),
                                             vmem_limit_bytes=VMEM_LIMIT),
        name="dsa_sample",
    )(page_table, *([cki] * npp), *([ck] * npp), *([cv] * npp),
      qi_rows, qbd, w_rows, r3(kbf), r3(vbf), r3(smallbf), bias_s)
    o5 = out_t.reshape(dbs, HEADS, HD, HEADS, n_new)
    o = jnp.stack([o5[:, h, :, h, :] for h in range(HEADS)], axis=1)
    return o.transpose(0, 3, 1, 2).reshape(dbs, n_new, B_DIM).astype(BF16)


def _rms(x, g):
    return x * lax.rsqrt(jnp.mean(x * x, axis=-1, keepdims=True) + EPS) * g


def _mlp_body(x_ref, ma_ref, ob_ref, woa_ref, wob_ref, gffn_ref, wg_ref, wu_ref, wd_ref, gfin_ref, y_ref):
    h = x_ref[...] + (_mm(ma_ref[...], woa_ref[...]) + _mm(ob_ref[...], wob_ref[...]))
    hn = _rms(h, gffn_ref[...]).astype(BF16)
    act = _silu(_mm(hn, wg_ref[...])) * _mm(hn, wu_ref[...])
    y_ref[...] = _rms(h + _mm(act.astype(BF16), wd_ref[...]), gfin_ref[...])


def _mlp(x2d, mix_a, o_b, w_out, g_ffn, w_gate, w_up, w_down, g_final, tm):
    n = x2d.shape[0]
    assert n % tm == 0
    row = lambda w: pl.BlockSpec((tm, w), lambda i: (i, 0))
    woa, wob = w_out[:A_DIM].astype(BF16), w_out[A_DIM:].astype(BF16)
    return pl.pallas_call(
        _mlp_body,
        grid=(n // tm,),
        in_specs=[row(D_MODEL), row(A_DIM), row(B_DIM), _const_spec(woa.shape), _const_spec(wob.shape),
                  _const_spec((1, D_MODEL)), _const_spec(w_gate.shape), _const_spec(w_up.shape),
                  _const_spec(w_down.shape), _const_spec((1, D_MODEL))],
        out_specs=row(D_MODEL),
        out_shape=jax.ShapeDtypeStruct((n, D_MODEL), F32),
        compiler_params=pltpu.CompilerParams(dimension_semantics=("parallel",), vmem_limit_bytes=VMEM_LIMIT),
        name="mlp",
    )(x2d, mix_a, o_b, woa, wob, g_ffn.reshape(1, D_MODEL), w_gate.astype(BF16), w_up.astype(BF16),
      w_down.astype(BF16), g_final.reshape(1, D_MODEL))


def kernel(x_prompt, x_sample, cache_k, cache_v, cache_kidx, state_conv, state_ssm, page_table, meta_tokens, norm_mix, w_in, conv_w, a_log, dt_bias, o_norm, w_out, rel_bias, norm_ffn, w_gate, w_up, w_down, norm_final):
    assert w_in.shape[0] == 1, "single-layer stack"
    bsz, seq, _ = x_prompt.shape
    dbs, n_new, _ = x_sample.shape
    t_pad = FRONT + seq
    lo = FRONT - N_META
    past_len = page_table.shape[1] * PAGE
    topk_p = min(TOPK_MAX, seq // 4)
    topk_s = min(TOPK_MAX, (past_len + n_new) // 4)
    w_parts = _split_w_in(w_in[0])
    mlp_w = (w_out[0], norm_ffn[0], w_gate[0], w_up[0], w_down[0], norm_final)
    gdn_w = (conv_w[0], a_log[0], dt_bias[0], o_norm[0])

    meta = jnp.broadcast_to(meta_tokens.astype(F32)[None], (bsz, N_META, D_MODEL))
    xp = jnp.concatenate([jnp.zeros((bsz, lo, D_MODEL), F32), meta, x_prompt], axis=1).reshape(bsz * t_pad, D_MODEL)
    qkva, z, qb, k, v, _, vbf, qi, small, _, small_t, small_tbf, kt = _proj(xp, norm_mix[0], w_parts, 512)
    mix_a, ssm_p = _gdn_prompt(qkva, z, small, small_t, *gdn_w, bsz, t_pad, lo)
    bias_tab = _bias_tab(rel_bias, t_pad // LANES)
    o_b = _dsa_prompt(qb, qi, small, kt, vbf, small_tbf, bias_tab, bsz, t_pad, lo, topk_p)
    y_p = _mlp(x_prompt.reshape(bsz * seq, D_MODEL), mix_a.reshape(bsz * seq, A_DIM), o_b.reshape(bsz * seq, B_DIM),
               *mlp_w, 512).reshape(bsz, seq, D_MODEL)
    real = lambda a: a.reshape(bsz, t_pad, a.shape[-1])[:, lo:]
    k_p = real(k).reshape(1, bsz, N_META + seq, HEADS, HD)
    v_p = real(v).reshape(1, bsz, N_META + seq, HEADS, HD)
    kidx_p = real(small)[:, :, S_KIDX:S_KIDX + HD][None]
    conv_p = real(qkva)[:, -(CONV_WIDTH - 1):][None]

    xs = x_sample.reshape(dbs * n_new, D_MODEL)
    (qkva_s, z_s, qb_s, k_s, v_s, kbf_s, vbf_s, qi_s, small_s, smallbf_s, small_t_s, _, _) = _proj(
        xs, norm_mix[0], w_parts, 512)
    mix_a_s, ssm_s = _gdn_sample(qkva_s, z_s, small_s, small_t_s, state_conv[0], state_ssm[0], *gdn_w, n_new)
    bias_s = _bias_sample(rel_bias, past_len, past_len // PAGE + 1)
    o_b_s = _dsa_sample(qb_s, qi_s, small_s, kbf_s, vbf_s, smallbf_s, cache_k[0], cache_v[0], cache_kidx[0], page_table,
                        bias_s, n_new, topk_s)
    y_s = _mlp(xs, mix_a_s, o_b_s.reshape(dbs * n_new, B_DIM), *mlp_w, 512).reshape(dbs, n_new, D_MODEL)
    k_sn = k_s.reshape(1, dbs, n_new, HEADS, HD)
    v_sn = v_s.reshape(1, dbs, n_new, HEADS, HD)
    kidx_s = small_s.reshape(dbs, n_new, LANES)[:, :, S_KIDX:S_KIDX + HD][None]
    conv_s = jnp.concatenate([state_conv[0].astype(F32), qkva_s.reshape(dbs, n_new, 3 * A_DIM)],
                             axis=1)[:, -(CONV_WIDTH - 1):][None]
    return (y_p, y_s, k_p, v_p, kidx_p, conv_p, ssm_p[None], k_sn, v_sn, kidx_s, conv_s, ssm_s[None])
```

```python
import functools
import math

import jax
import jax.numpy as jnp
from jax import lax
from jax.experimental import pallas as pl
from jax.experimental.pallas import tpu as pltpu

F32 = jnp.float32
BF16 = jnp.bfloat16
I32 = jnp.int32
HI = lax.Precision.HIGHEST

D_MODEL = 1024
N_META = 16
HEADS = 8
HD = 64
A_DIM = HEADS * HD
B_DIM = HEADS * HD
CONV_WIDTH = 4
GDN_CHUNK = 64
TOPK_MAX = 256
PAGE = 128
REL_BUCKETS = 32
REL_MAX_DIST = 1024
EPS = 1e-6
LANES = 128
SUBLANES = 8
FRONT = 128
NEG = -1e30
INT_MIN = -2 ** 31
V7X_VMEM_BYTES = 64 * 1024 * 1024
VMEM_LIMIT = V7X_VMEM_BYTES * 7 // 8

S_KIDX, S_BETA, S_A, S_W = 0, 64, 72, 80


def _mm(a, b):
    return jnp.dot(a, b, preferred_element_type=F32)


def _mm_hi(a, b):
    return jnp.dot(a, b, preferred_element_type=F32, precision=HI)


def _split_bf16(x):
    hi = x.astype(BF16)
    return hi, (x - hi.astype(F32)).astype(BF16)


def _nt(a, b, precision=None):
    return lax.dot_general(a, b, (((1,), (1,)), ((), ())), preferred_element_type=F32, precision=precision)


def _const_spec(shape):
    nd = len(shape)
    return pl.BlockSpec(shape, lambda *_: (0,) * nd, pipeline_mode=pl.Buffered(1))


def _silu(x):
    return x * jax.nn.sigmoid(x)


def _proj_body(x_ref, g_ref, wa_ref, wz_ref, wb_ref, wqi_ref, ws_ref, wst_ref,
               qkva_ref, z_ref, qb_ref, k_ref, v_ref, kbf_ref, vbf_ref, qi_ref, small_ref, smallbf_ref, smallt_ref,
               smalltbf_ref, kt_ref):
    x = x_ref[...]
    ms = jnp.mean(x * x, axis=-1, keepdims=True)
    h = (x * lax.rsqrt(ms + EPS) * g_ref[...]).astype(BF16)
    qkva_ref[...] = _mm(h, wa_ref[...])
    z_ref[...] = _mm(h, wz_ref[...])
    qkvb = _mm(h, wb_ref[...])
    qb_ref[...] = (qkvb[:, :B_DIM] * 0.125).astype(BF16)
    k = qkvb[:, B_DIM:2 * B_DIM]
    v = qkvb[:, 2 * B_DIM:]
    k_ref[...] = k
    v_ref[...] = v
    kbf_ref[...] = k.astype(BF16)
    vbf_ref[...] = v.astype(BF16)
    qi_ref[...] = (_mm(h, wqi_ref[...]) * 0.125).astype(BF16)
    s = _mm(h, ws_ref[...])
    small_ref[...] = s
    smallbf_ref[...] = s.astype(BF16)
    t = _nt(wst_ref[...], h)
    smallt_ref[...] = t[:LANES]
    smalltbf_ref[...] = t[:LANES].astype(BF16)
    kt_ref[...] = t[LANES:].astype(BF16)


def _split_w_in(w_in):
    c = 0
    parts = []
    for n in (3 * A_DIM, A_DIM, HEADS, HEADS, 3 * B_DIM, HEADS * HD, HD, HEADS):
        parts.append(w_in[:, c:c + n])
        c += n
    w_qkva, w_z, w_beta, w_a, w_qkvb, w_qi, w_kidx, w_w = parts
    pad = jnp.zeros((w_in.shape[0], LANES - (HD + 3 * HEADS)), w_in.dtype)
    w_small = jnp.concatenate([w_kidx, w_beta, w_a, w_w, pad], axis=1)
    bf = lambda t: t.astype(BF16)
    w_t = jnp.concatenate([w_small, w_qkvb[:, B_DIM:2 * B_DIM]], axis=1).T
    return bf(w_qkva), bf(w_z), bf(w_qkvb), bf(w_qi), bf(w_small), bf(w_t)


def _proj(x2d, g, w_parts, tm):
    n = x2d.shape[0]
    assert n % tm == 0 and tm % LANES == 0
    wa, wz, wb, wqi, ws, wst = w_parts
    row = lambda w: pl.BlockSpec((tm, w), lambda i: (i, 0))
    outs = [
        (jax.ShapeDtypeStruct((n, 3 * A_DIM), F32), row(3 * A_DIM)),
        (jax.ShapeDtypeStruct((n, A_DIM), F32), row(A_DIM)),
        (jax.ShapeDtypeStruct((n, B_DIM), BF16), row(B_DIM)),
        (jax.ShapeDtypeStruct((n, B_DIM), F32), row(B_DIM)),
        (jax.ShapeDtypeStruct((n, B_DIM), F32), row(B_DIM)),
        (jax.ShapeDtypeStruct((n, B_DIM), BF16), row(B_DIM)),
        (jax.ShapeDtypeStruct((n, B_DIM), BF16), row(B_DIM)),
        (jax.ShapeDtypeStruct((n, HEADS * HD), BF16), row(HEADS * HD)),
        (jax.ShapeDtypeStruct((n, LANES), F32), row(LANES)),
        (jax.ShapeDtypeStruct((n, LANES), BF16), row(LANES)),
        (jax.ShapeDtypeStruct((LANES, n), F32), pl.BlockSpec((LANES, tm), lambda i: (0, i))),
        (jax.ShapeDtypeStruct((LANES, n), BF16), pl.BlockSpec((LANES, tm), lambda i: (0, i))),
        (jax.ShapeDtypeStruct((B_DIM, n), BF16), pl.BlockSpec((B_DIM, tm), lambda i: (0, i))),
    ]
    return pl.pallas_call(
        _proj_body,
        grid=(n // tm,),
        in_specs=[row(D_MODEL), _const_spec((1, D_MODEL)), _const_spec(wa.shape), _const_spec(wz.shape),
                  _const_spec(wb.shape), _const_spec(wqi.shape), _const_spec(ws.shape), _const_spec(wst.shape)],
        out_specs=[o[1] for o in outs],
        out_shape=[o[0] for o in outs],
        compiler_params=pltpu.CompilerParams(dimension_semantics=("parallel",), vmem_limit_bytes=VMEM_LIMIT),
        name="proj",
    )(x2d, g.reshape(1, D_MODEL), wa, wz, wb, wqi, ws, wst)


def _gdn_prep(qkv, tail, conv_w, small, small_t, a_log_row, dt_row, a_log_col, dt_col, valid_col, valid_row, chunk):
    rows = qkv.shape[0]
    n_chunks = rows // chunk
    ys = []
    for c in range(n_chunks):
        ext = jnp.concatenate([tail[c], qkv[c * chunk:(c + 1) * chunk]], axis=0)
        y = sum(ext[SUBLANES - (CONV_WIDTH - 1) + j: SUBLANES - (CONV_WIDTH - 1) + j + chunk] * conv_w[j:j + 1]
                for j in range(CONV_WIDTH))
        ys.append(_silu(y))
    y = ys[0] if n_chunks == 1 else jnp.concatenate(ys, axis=0)
    beta_c = jnp.where(valid_col, jax.nn.sigmoid(small), 0.0)
    beta_r = jnp.where(valid_row, jax.nn.sigmoid(small_t), 0.0)
    g_c = jnp.where(valid_col, -jnp.exp(a_log_row) * jax.nn.softplus(small + dt_row), 0.0)
    g_r = jnp.where(valid_row, -jnp.exp(a_log_col) * jax.nn.softplus(small_t + dt_col), 0.0)
    ri = lax.broadcasted_iota(I32, (rows, rows), 0)
    ci = lax.broadcasted_iota(I32, (rows, rows), 1)
    same = (ri // chunk) == (ci // chunk)
    low = jnp.where(same & (ci <= ri), 1.0, 0.0).astype(F32)
    b_c = _mm_hi(low, g_c)
    b_r = _nt(g_r, low, precision=HI)
    return y, beta_c, b_c, b_r


def _bdot(a, b, ca, cb):
    return lax.dot_general(a, b, (((ca,), (cb,)), ((0,), (0,))), preferred_element_type=F32)


def _bmm_solve(a, b):
    a_hi, a_lo = _split_bf16(a)
    b_hi, b_lo = _split_bf16(b)
    return _bdot(a_hi, b_hi, 2, 1) + (_bdot(a_hi, b_lo, 2, 1) + _bdot(a_lo, b_hi, 2, 1))


def _gdn_solve(q, k, v, beta, b_c, b_r, chunk):
    c = chunk
    qn = q * lax.rsqrt(jnp.sum(q * q, axis=-1, keepdims=True) + EPS) * (HD ** -0.5)
    kn = k * lax.rsqrt(jnp.sum(k * k, axis=-1, keepdims=True) + EPS)
    ri = lax.broadcasted_iota(I32, (1, c, c), 1)
    ci = lax.broadcasted_iota(I32, (1, c, c), 2)
    incl = ci <= ri
    decay = jnp.where(incl, jnp.exp(jnp.where(incl, b_c - b_r, 0.0)), 0.0)
    kb, qb16 = kn.astype(BF16), qn.astype(BF16)
    kk = _bdot(kb, kb, 2, 2)
    qk = _bdot(qb16, kb, 2, 2)
    bk = -jnp.where(ci < ri, beta * kk * decay, 0.0)
    eb = jnp.exp(b_c)
    y = jnp.concatenate([beta * v, (beta * eb) * kn], axis=2)
    n_levels = int(math.log2(c))
    for lvl in range(n_levels):
        if lvl + 1 < n_levels:
            prod = _bmm_solve(bk, jnp.concatenate([y, bk], axis=2))
            y, bk = y + prod[:, :, :2 * HD], prod[:, :, 2 * HD:]
        else:
            y = y + _bmm_solve(bk, y)
    b_last = b_c[:, c - 1:c, :]
    return (y[:, :, :HD], y[:, :, HD:].astype(BF16), (qk * decay).astype(BF16), qb16,
            (kn * jnp.exp(b_last - b_c)).astype(BF16), eb, jnp.exp(b_last))


def _gdn_apply(u, w, attn, qb16, kd, eb, eb_last, s0, o_norm, z):
    s0b = s0.astype(BF16)
    delta = u - _bdot(w, s0b, 2, 1)
    deltab = delta.astype(BF16)
    o = eb * _bdot(qb16, s0b, 2, 1) + _bdot(attn, deltab, 2, 1)
    s_new = eb_last * s0 + _bdot(kd, deltab, 1, 1)
    on = o * lax.rsqrt(jnp.mean(o * o, axis=-1, keepdims=True) + EPS) * o_norm
    return s_new, on * _silu(z)


def _gdn_prompt_body(*refs, rows, lo, nb):
    qkv_ref, z_ref, small_ref = refs[:3]
    smallt_refs = refs[3:3 + nb]
    convw_ref, alr_ref, dtr_ref, alc_ref, dtc_ref, onorm_ref, mix_ref, s_ref, tail_scr = refs[3 + nb:]
    t = pl.program_id(1)

    @pl.when(t == 0)
    def _():
        s_ref[...] = jnp.zeros_like(s_ref)
        tail_scr[...] = jnp.zeros_like(tail_scr)

    chunk = GDN_CHUNK
    n_chunks = rows // chunk
    pos_c = t * rows + lax.broadcasted_iota(I32, (rows, LANES), 0)
    pos_r = t * rows + lax.broadcasted_iota(I32, (LANES, rows), 1)
    preps = []
    for s in range(nb):
        qkv = qkv_ref[s]
        tails = [tail_scr[s]] + [qkv[c * chunk - SUBLANES:c * chunk] for c in range(1, n_chunks)]
        preps.append(_gdn_prep(qkv, tails, convw_ref[...], small_ref[s], smallt_refs[s][...],
                               alr_ref[...], dtr_ref[...], alc_ref[...], dtc_ref[...],
                               pos_c >= lo, pos_r >= lo, chunk) + (z_ref[s],))
        tail_scr[s] = qkv[rows - SUBLANES:rows]
    units = [(c, s, h) for c in range(n_chunks) for s in range(nb) for h in range(HEADS)]
    rs = lambda c: slice(c * chunk, (c + 1) * chunk)
    stack = lambda k, off: jnp.stack([preps[s][k][rs(c), off + h * HD:off + (h + 1) * HD] for c, s, h in units])
    col = lambda k, off: jnp.stack([preps[s][k][rs(c), off + h:off + h + 1] for c, s, h in units])
    sol = _gdn_solve(stack(0, 0), stack(0, A_DIM), stack(0, 2 * A_DIM), col(1, S_BETA), col(2, S_A),
                     jnp.stack([preps[s][3][S_A + h:S_A + h + 1, rs(c)] for c, s, h in units]), chunk)
    z3 = stack(4, 0)
    o_norm = onorm_ref[...]
    per_chunk = nb * HEADS
    for c in range(n_chunks):
        g = slice(c * per_chunk, (c + 1) * per_chunk)
        s_new, out = _gdn_apply(*[a[g] for a in sol], s_ref[...].reshape(per_chunk, HD, HD), o_norm, z3[g])
        s_ref[...] = s_new.reshape(nb, HEADS, HD, HD)
        for s in range(nb):
            for h in range(HEADS):
                mix_ref[s, rs(c), h * HD:(h + 1) * HD] = out[s * HEADS + h].astype(mix_ref.dtype)


GDN_SEQS_PER_STEP = 2


def _gate_params(a_log, dt_bias):
    row = lambda v: jnp.zeros((1, LANES), F32).at[0, S_A:S_A + HEADS].set(v.astype(F32))
    return row(a_log), row(dt_bias), row(a_log).T, row(dt_bias).T


def _gdn_prompt(qkva, z, small, small_t, conv_w, a_log, dt_bias, o_norm, bsz, t_pad, lo):
    rows = LANES
    nt = t_pad // rows
    nb = GDN_SEQS_PER_STEP if bsz % GDN_SEQS_PER_STEP == 0 else 1
    alr, dtr, alc, dtc = _gate_params(a_log, dt_bias)
    body = functools.partial(_gdn_prompt_body, rows=rows, lo=lo, nb=nb)
    seqs = lambda w: pl.BlockSpec((nb, rows, w), lambda b, t: (b, t, 0))
    mix, s = pl.pallas_call(
        body,
        grid=(bsz // nb, nt),
        in_specs=[seqs(3 * A_DIM), seqs(A_DIM), seqs(LANES)] + [
            pl.BlockSpec((LANES, rows), functools.partial(lambda b, t, s: (0, (b * nb + s) * nt + t), s=s))
            for s in range(nb)] + [
            _const_spec((CONV_WIDTH, 3 * A_DIM)), _const_spec((1, LANES)), _const_spec((1, LANES)),
            _const_spec((LANES, 1)), _const_spec((LANES, 1)), _const_spec((1, HD)),
        ],
        out_specs=[
            pl.BlockSpec((nb, rows, A_DIM), lambda b, t: (b, jnp.maximum(t - 1, 0), 0)),
            pl.BlockSpec((nb, HEADS, HD, HD), lambda b, t: (b, 0, 0, 0)),
        ],
        out_shape=[jax.ShapeDtypeStruct((bsz, t_pad - FRONT, A_DIM), BF16),
                   jax.ShapeDtypeStruct((bsz, HEADS, HD, HD), F32)],
        scratch_shapes=[pltpu.VMEM((nb, SUBLANES, 3 * A_DIM), F32)],
        compiler_params=pltpu.CompilerParams(dimension_semantics=("parallel", "arbitrary"),
                                             vmem_limit_bytes=VMEM_LIMIT),
        name="gdn_prompt",
    )(qkva.reshape(bsz, t_pad, 3 * A_DIM), z.reshape(bsz, t_pad, A_DIM), small.reshape(bsz, t_pad, LANES),
      *([small_t] * nb), conv_w, alr, dtr, alc, dtc, o_norm.reshape(1, HD))
    return mix, s


def _gdn_sample_body(qkv_ref, z_ref, small_ref, smallt_ref, conv0_ref, s0_ref, convw_ref, alr_ref, dtr_ref, alc_ref,
                     dtc_ref, onorm_ref, mix_ref, s_ref, *, chunk):
    rows = LANES
    n_seq = rows // chunk
    qkv = qkv_ref[...]
    tails = [conv0_ref[i] for i in range(n_seq)]
    y, beta_c, b_c, b_r = _gdn_prep(qkv, tails, convw_ref[...], small_ref[...], smallt_ref[...],
                                    alr_ref[...], dtr_ref[...], alc_ref[...], dtc_ref[...], True, True, chunk)
    z = z_ref[...]
    per_seq = lambda a: a.reshape(n_seq, chunk, a.shape[-1])
    stack = lambda a, off: jnp.concatenate([per_seq(a[:, off + h * HD:off + (h + 1) * HD]) for h in range(HEADS)])
    col = lambda a, off: jnp.concatenate([per_seq(a[:, off + h:off + h + 1]) for h in range(HEADS)])
    b_r3 = jnp.stack([b_r[S_A + h:S_A + h + 1, i * chunk:(i + 1) * chunk] for h in range(HEADS) for i in range(n_seq)])
    sol = _gdn_solve(stack(y, 0), stack(y, A_DIM), stack(y, 2 * A_DIM), col(beta_c, S_BETA), col(b_c, S_A), b_r3, chunk)
    s0 = jnp.concatenate([s0_ref[:, h] for h in range(HEADS)])
    s_new, out = _gdn_apply(*sol, s0, onorm_ref[...], stack(z, 0))
    for h in range(HEADS):
        g = slice(h * n_seq, (h + 1) * n_seq)
        s_ref[:, h] = s_new[g]
        mix_ref[:, h * HD:(h + 1) * HD] = out[g].reshape(rows, HD).astype(mix_ref.dtype)


def _gdn_sample(qkva, z, small, small_t, state_conv, state_ssm, conv_w, a_log, dt_bias, o_norm, n_new):
    n = qkva.shape[0]
    bsz = n // n_new
    rows = LANES
    n_seq = rows // n_new
    assert n_new == SUBLANES and n % rows == 0
    alr, dtr, alc, dtc = _gate_params(a_log, dt_bias)
    conv0 = jnp.pad(state_conv.astype(F32), ((0, 0), (SUBLANES - (CONV_WIDTH - 1), 0), (0, 0)))
    body = functools.partial(_gdn_sample_body, chunk=n_new)
    mix, s = pl.pallas_call(
        body,
        grid=(n // rows,),
        in_specs=[
            pl.BlockSpec((rows, 3 * A_DIM), lambda g: (g, 0)),
            pl.BlockSpec((rows, A_DIM), lambda g: (g, 0)),
            pl.BlockSpec((rows, LANES), lambda g: (g, 0)),
            pl.BlockSpec((LANES, rows), lambda g: (0, g)),
            pl.BlockSpec((n_seq, SUBLANES, 3 * A_DIM), lambda g: (g, 0, 0)),
            pl.BlockSpec((n_seq, HEADS, HD, HD), lambda g: (g, 0, 0, 0)),
            _const_spec((CONV_WIDTH, 3 * A_DIM)), _const_spec((1, LANES)), _const_spec((1, LANES)),
            _const_spec((LANES, 1)), _const_spec((LANES, 1)), _const_spec((1, HD)),
        ],
        out_specs=[
            pl.BlockSpec((rows, A_DIM), lambda g: (g, 0)),
            pl.BlockSpec((n_seq, HEADS, HD, HD), lambda g: (g, 0, 0, 0)),
        ],
        out_shape=[jax.ShapeDtypeStruct((n, A_DIM), BF16), jax.ShapeDtypeStruct((bsz, HEADS, HD, HD), F32)],
        compiler_params=pltpu.CompilerParams(dimension_semantics=("parallel",), vmem_limit_bytes=VMEM_LIMIT),
        name="gdn_sample",
    )(qkva, z, small, small_t, conv0, state_ssm.astype(F32), conv_w, alr, dtr, alc, dtc, o_norm.reshape(1, HD))
    return mix, s


def _rel_bucket(d):
    d = jnp.maximum(d, 0)
    max_exact = REL_BUCKETS // 2
    df = jnp.maximum(d, 1).astype(F32)
    large = max_exact + (jnp.log(df / max_exact) / math.log(REL_MAX_DIST / max_exact)
                         * (REL_BUCKETS - max_exact)).astype(I32)
    large = jnp.minimum(large, REL_BUCKETS - 1)
    return jnp.where(d < max_exact, d, large)


def _bias_lookup(bucket, rb_ref, h):
    acc = jnp.zeros(bucket.shape, F32)
    for b in range(REL_BUCKETS):
        acc = jnp.where(bucket == b, rb_ref[b, h], acc)
    return acc


def _bias_tab_body(rb_ref, out_ref):
    dlt = pl.program_id(0)
    i = lax.broadcasted_iota(I32, (LANES, LANES), 0)
    j = lax.broadcasted_iota(I32, (LANES, LANES), 1)
    bucket = _rel_bucket(dlt * LANES + i - j)
    for h in range(HEADS):
        out_ref[h, 0] = _bias_lookup(bucket, rb_ref, h)


def _bias_tab(rel_bias, n_tiles):
    return pl.pallas_call(
        _bias_tab_body,
        grid=(n_tiles,),
        in_specs=[pl.BlockSpec(memory_space=pltpu.SMEM)],
        out_specs=pl.BlockSpec((HEADS, 1, LANES, LANES), lambda t: (0, t, 0, 0)),
        out_shape=jax.ShapeDtypeStruct((HEADS, n_tiles, LANES, LANES), F32),
        name="bias_tab",
    )(rel_bias.astype(F32))


def _bias_sample_body(rb_ref, out_ref, *, past_len):
    blk = pl.program_id(0)
    q = lax.broadcasted_iota(I32, (SUBLANES, LANES), 0)
    s = blk * LANES + lax.broadcasted_iota(I32, (SUBLANES, LANES), 1)
    bucket = _rel_bucket(past_len + q - s)
    for h in range(HEADS):
        out_ref[h * SUBLANES:(h + 1) * SUBLANES, :] = _bias_lookup(bucket, rb_ref, h)


def _bias_sample(rel_bias, past_len, n_blocks):
    return pl.pallas_call(
        functools.partial(_bias_sample_body, past_len=past_len),
        grid=(n_blocks,),
        in_specs=[pl.BlockSpec(memory_space=pltpu.SMEM)],
        out_specs=pl.BlockSpec((HEADS * SUBLANES, LANES), lambda t: (0, t)),
        out_shape=jax.ShapeDtypeStruct((HEADS * SUBLANES, n_blocks * LANES), F32),
        name="bias_sample",
    )(rel_bias.astype(F32))


def _sort_key(score, admissible):
    score = jnp.where(score == 0.0, 0.0, score)
    bits = pltpu.bitcast(score, I32)
    key = jnp.where(bits < 0, bits ^ 0x7FFFFFFF, bits)
    return jnp.where(admissible, key, INT_MIN)


def _kth_largest_key(count_ge, rows, topk, two_bits=False):
    count = lambda cand: jnp.broadcast_to(count_ge(cand), (rows, LANES))
    c0 = count(jnp.zeros((rows, LANES), I32))
    t0 = jnp.where(c0 >= topk, 0, INT_MIN).astype(I32)
    c0 = jnp.where(c0 >= topk, c0, float(2 ** 30))

    def accept(carry, cand, cc):
        t, c = carry
        ok = cc >= topk
        return jnp.where(ok, cand, t), jnp.where(ok, cc, c)

    def bit_step(i, carry):
        cand = carry[0] + lax.shift_left(jnp.int32(1), 30 - i)
        return accept(carry, cand, count(cand))

    def two_bit_step(i, carry):
        unit = lax.shift_left(jnp.int32(1), 29 - 2 * i)
        cands = [carry[0] + m * unit for m in (1, 2, 3)]
        counts = [count(cand) for cand in cands]
        for cand, cc in zip(cands, counts):
            carry = accept(carry, cand, cc)
        return carry

    if not two_bits:
        return lax.fori_loop(0, 31, bit_step, (t0, c0))
    carry = lax.fori_loop(0, 15, two_bit_step, (t0, c0))
    return bit_step(30, carry)


def _tie_mask(key, thr, need, running, tri):
    eq = (key == thr) & (key != INT_MIN)
    pref = _mm(jnp.where(eq, 1.0, 0.0).astype(BF16), tri) + running
    sel = (key > thr) | (eq & (pref <= need))
    return sel, jnp.broadcast_to(pref[:, LANES - 1:LANES], pref.shape)


def _tri_incl():
    r = lax.broadcasted_iota(I32, (LANES, LANES), 0)
    c = lax.broadcasted_iota(I32, (LANES, LANES), 1)
    return jnp.where(r <= c, 1.0, 0.0).astype(BF16)


def _dsa_prompt_tiles(first, n_tiles, qi, w_raw, kt_ref, kidxt_ref, bias_ref, out_ref,
                      qh_scr, vh_scr, keys_scr, mbias_scr, o_scr, *, lo, topk):
    nblk = first + n_tiles
    tq = n_tiles * LANES
    wid = nblk * LANES
    w = w_raw * (HEADS ** -0.5)
    tiles = [(slice(r * LANES, (r + 1) * LANES), (first + r + 1) * LANES) for r in range(n_tiles)]
    for r, (rows, wr) in enumerate(tiles):
        t_pos = (first + r) * LANES + lax.broadcasted_iota(I32, (LANES, wr), 0)
        s_pos = lax.broadcasted_iota(I32, (LANES, wr), 1)
        kxt = kidxt_ref[S_KIDX:S_KIDX + HD, 0:wr]
        acc = jnp.zeros((LANES, wr), F32)
        for h in range(HEADS):
            acc = acc + jnp.maximum(_mm(qi[rows, h * HD:(h + 1) * HD], kxt), 0.0) * w[rows, h:h + 1]
        keys_scr[rows, 0:wr] = _sort_key(acc, (s_pos <= t_pos) & (s_pos >= lo))
        if wr < wid:
            keys_scr[rows, wr:wid] = jnp.full((LANES, wid - wr), INT_MIN, I32)

    def count_ge(cand):
        return jnp.sum(jnp.where(keys_scr[:, 0:wid] >= cand[:, 0:1], 1.0, 0.0), axis=1, keepdims=True)

    thr, cnt = _kth_largest_key(count_ge, tq, topk)
    keys = keys_scr[:, 0:wid]
    mbias_scr[:, 0:wid] = jnp.where(keys >= jnp.maximum(thr[:, 0:1], INT_MIN + 1), 0.0, NEG)

    @pl.when(jnp.max(jnp.where(cnt < float(2 ** 30), cnt, 0.0)) > topk)
    def _():
        need = topk - jnp.sum(jnp.where(keys_scr[:, 0:wid] > thr[:, 0:1], 1.0, 0.0), axis=1, keepdims=True)
        tri = _tri_incl()
        running = jnp.zeros((tq, LANES), F32)
        for jb in range(nblk):
            cols = slice(jb * LANES, (jb + 1) * LANES)
            sel, running = _tie_mask(keys_scr[:, cols], thr, need, running, tri)
            mbias_scr[:, cols] = jnp.where(sel, 0.0, NEG)

    def head_attn(h, carry):
        for r, (rows, wr) in enumerate(tiles):
            bias = jnp.concatenate([bias_ref[h, first + r - jb] for jb in range(wr // LANES)], axis=1)
            kt = kt_ref[pl.ds(pl.multiple_of(h * HD, HD), HD), 0:wr]
            lg = _mm(qh_scr[h, rows], kt) + bias + mbias_scr[rows, 0:wr]
            p = jnp.exp(lg - jnp.max(lg, axis=1, keepdims=True))
            inv = 1.0 / jnp.sum(p, axis=1, keepdims=True)
            o_scr[h, rows] = _mm(p.astype(BF16), vh_scr[h, 0:wr, :]) * inv
        return carry

    lax.fori_loop(0, HEADS, head_attn, 0)
    for h in range(HEADS):
        out_ref[0, :, h * HD:(h + 1) * HD] = o_scr[h].astype(out_ref.dtype)


def _dsa_prompt_body(*refs, lo, topk, n_tiles, n_steps):
    qb_refs, qi_refs, small_refs = refs[:n_tiles], refs[n_tiles:2 * n_tiles], refs[2 * n_tiles:3 * n_tiles]
    (kt_ref, vbf_ref, kidxt_ref, bias_ref, out_ref, qh_scr, vh_scr, keys_scr, mbias_scr, o_scr) = refs[3 * n_tiles:]
    js = pl.program_id(1)

    @pl.when(js == 0)
    def _():
        for h in range(HEADS):
            vh_scr[h] = vbf_ref[0, :, h * HD:(h + 1) * HD]

    for r in range(n_tiles):
        for h in range(HEADS):
            qh_scr[h, r * LANES:(r + 1) * LANES, :] = qb_refs[r][0, :, h * HD:(h + 1) * HD]
    qi = jnp.concatenate([ref[0] for ref in qi_refs], axis=0)
    w_raw = jnp.concatenate([ref[0][:, S_W:S_W + HEADS] for ref in small_refs], axis=0)
    for step in range(n_steps):
        @pl.when(js == step)
        def _(step=step):
            _dsa_prompt_tiles(1 + step * n_tiles, n_tiles, qi, w_raw, kt_ref, kidxt_ref, bias_ref, out_ref,
                              qh_scr, vh_scr, keys_scr, mbias_scr, o_scr, lo=lo, topk=topk)


DSA_TILES_PER_STEP = 4


def _dsa_prompt(qb, qi, small, kt, vbf, smalltbf, bias_tab, bsz, t_pad, lo, topk):
    n_real = t_pad // LANES - 1
    n_tiles = max(d for d in (1, 2, DSA_TILES_PER_STEP) if n_real % d == 0)
    n_steps = n_real // n_tiles
    rows = n_tiles * LANES
    r3 = lambda a: a.reshape(bsz, t_pad, a.shape[-1])
    tiles = lambda w: [pl.BlockSpec((1, LANES, w), functools.partial(lambda b, js, r: (b, 1 + js * n_tiles + r, 0), r=r))
                       for r in range(n_tiles)]
    seq_cols = lambda r: pl.BlockSpec((r, t_pad), lambda b, js: (0, b))
    return pl.pallas_call(
        functools.partial(_dsa_prompt_body, lo=lo, topk=topk, n_tiles=n_tiles, n_steps=n_steps),
        grid=(bsz, n_steps),
        in_specs=tiles(B_DIM) + tiles(HEADS * HD) + tiles(LANES) + [
            seq_cols(B_DIM), pl.BlockSpec((1, t_pad, B_DIM), lambda b, js: (b, 0, 0)), seq_cols(LANES),
            _const_spec(bias_tab.shape)],
        out_specs=pl.BlockSpec((1, rows, B_DIM), lambda b, js: (b, js, 0)),
        out_shape=jax.ShapeDtypeStruct((bsz, t_pad - FRONT, B_DIM), BF16),
        scratch_shapes=[pltpu.VMEM((HEADS, rows, HD), BF16), pltpu.VMEM((HEADS, t_pad, HD), BF16),
                        pltpu.VMEM((rows, t_pad), I32), pltpu.VMEM((rows, t_pad), F32),
                        pltpu.VMEM((HEADS, rows, HD), F32)],
        compiler_params=pltpu.CompilerParams(dimension_semantics=("parallel", "arbitrary"),
                                             vmem_limit_bytes=VMEM_LIMIT),
        name="dsa_prompt",
    )(*([r3(qb)] * n_tiles), *([r3(qi)] * n_tiles), *([r3(small)] * n_tiles), kt, r3(vbf), smalltbf, bias_tab)


PAGES_PER_STEP = 16


def _dsa_sample_body(pt_ref, *refs, n_new, past_len, topk):
    npp = PAGES_PER_STEP
    kidx_refs, k_refs, v_refs = refs[:npp], refs[npp:2 * npp], refs[2 * npp:3 * npp]
    (qi_ref, qbd_ref, w_ref, knew_ref, vnew_ref, kidxnew_ref, bias_ref, out_ref,
     keys_scr, mb_scr, logit_scr, vt_scr) = refs[3 * npp:]
    del pt_ref
    s = pl.program_id(1)
    n_page_steps = past_len // (PAGE * npp)
    n_blocks = past_len // PAGE + 1
    qi = qi_ref[0]
    qbd = qbd_ref[0]
    w = w_ref[0] * (HEADS ** -0.5)
    q_row = lax.broadcasted_iota(I32, (n_new, LANES), 0)
    lane = lax.broadcasted_iota(I32, (n_new, LANES), 1)

    def do_block(off, kxt, kt, vt, admissible):
        rel = jnp.maximum(_mm(qi, kxt), 0.0) * w
        sc = rel[0:n_new]
        for h in range(1, HEADS):
            sc = sc + rel[h * n_new:(h + 1) * n_new]
        keys_scr[:, pl.ds(off, LANES)] = _sort_key(sc, admissible)
        logit_scr[:, pl.ds(off, LANES)] = _mm(qbd, kt) + bias_ref[:, pl.ds(off, LANES)]
        vt_scr[:, pl.ds(off, LANES)] = vt

    @pl.when(s < n_page_steps)
    def _():
        for i in range(npp):
            off = pl.multiple_of((s * npp + i) * PAGE, PAGE)
            do_block(off, kidx_refs[i][0].astype(BF16), k_refs[i][0].reshape(B_DIM, PAGE).astype(BF16),
                     v_refs[i][0].reshape(B_DIM, PAGE).astype(BF16), True)

    @pl.when(s == n_page_steps)
    def _():
        tr = lambda a: jnp.concatenate(
            [a.astype(F32), jnp.zeros((PAGE - n_new, a.shape[1]), F32)], axis=0).T.astype(BF16)
        do_block(past_len, tr(kidxnew_ref[0])[S_KIDX:S_KIDX + HD], tr(knew_ref[0]), tr(vnew_ref[0]), lane <= q_row)

        keys = keys_scr[...]
        thr, cnt = _kth_largest_key(
            lambda cand: jnp.sum(jnp.where(keys >= cand[:, 0:1], 1.0, 0.0), axis=1, keepdims=True), n_new, topk,
            two_bits=True)
        mb_scr[...] = jnp.where(keys >= jnp.maximum(thr[:, 0:1], INT_MIN + 1), 0.0, NEG)

        @pl.when(jnp.max(jnp.where(cnt < float(2 ** 30), cnt, 0.0)) > topk)
        def _():
            need = topk - jnp.sum(jnp.where(keys > thr[:, 0:1], 1.0, 0.0), axis=1, keepdims=True)
            tri = _tri_incl()

            def mask_blk(jb, running):
                cols = pl.ds(pl.multiple_of(jb * LANES, LANES), LANES)
                sel, running = _tie_mask(keys_scr[:, cols], thr, need, running, tri)
                mb_scr[:, cols] = jnp.where(sel, 0.0, NEG)
                return running

            lax.fori_loop(0, n_blocks, mask_blk, jnp.zeros((n_new, LANES), F32))

        lg = logit_scr[...] + jnp.concatenate([mb_scr[...]] * HEADS, axis=0)
        m = jnp.max(lg, axis=1, keepdims=True)
        p = jnp.exp(lg - m)
        pb = (p * (1.0 / jnp.sum(p, axis=1, keepdims=True))).astype(BF16)
        out_ref[0] = _nt(vt_scr[...], pb)


def _dsa_sample(qb, qi, small, kbf, vbf, smallbf, cache_k, cache_v, cache_kidx, page_table, bias_s, n_new, topk):
    dbs, n_pages = page_table.shape
    past_len = n_pages * PAGE
    npp = PAGES_PER_STEP
    assert n_pages % npp == 0 and n_new == SUBLANES
    n_page_steps = n_pages // npp
    n_blocks = n_pages + 1
    n_pool = cache_k.shape[0]
    qi_rows = qi.reshape(dbs, n_new, HEADS, HD).transpose(0, 2, 1, 3).reshape(dbs, HEADS * n_new, HD)
    q4 = qb.reshape(dbs, n_new, HEADS, HD).transpose(0, 2, 1, 3)
    eye = jnp.eye(HEADS, dtype=qb.dtype)
    qbd = (q4[:, :, :, None, :] * eye[None, :, None, :, None]).reshape(dbs, HEADS * n_new, B_DIM)
    w_rows = small.reshape(dbs, n_new, LANES)[:, :, S_W:S_W + HEADS].transpose(0, 2, 1).reshape(dbs, HEADS * n_new, 1)
    w_rows = jnp.broadcast_to(w_rows, (dbs, HEADS * n_new, LANES))

    def page_spec(shape, i):
        def imap(b, s, pt):
            return (pt[b, jnp.minimum(s, n_page_steps - 1) * npp + i],) + (0,) * len(shape)
        return pl.BlockSpec((1,) + shape, imap)

    per_seq = lambda r, w: pl.BlockSpec((1, r, w), lambda b, s, pt: (b, 0, 0))
    in_specs = ([page_spec((HD, PAGE), i) for i in range(npp)] + [page_spec((HEADS, HD, PAGE), i) for i in range(npp)]
                + [page_spec((HEADS, HD, PAGE), i) for i in range(npp)]
                + [per_seq(HEADS * n_new, HD), per_seq(HEADS * n_new, B_DIM), per_seq(HEADS * n_new, LANES),
                   per_seq(n_new, B_DIM), per_seq(n_new, B_DIM), per_seq(n_new, LANES),
                   pl.BlockSpec(bias_s.shape, lambda b, s, pt: (0, 0), pipeline_mode=pl.Buffered(1))])
    grid_spec = pltpu.PrefetchScalarGridSpec(
        num_scalar_prefetch=1,
        grid=(dbs, n_page_steps + 1),
        in_specs=in_specs,
        out_specs=pl.BlockSpec((1, B_DIM, HEADS * n_new), lambda b, s, pt: (b, 0, 0)),
        scratch_shapes=[pltpu.VMEM((n_new, n_blocks * LANES), I32), pltpu.VMEM((n_new, n_blocks * LANES), F32),
                        pltpu.VMEM((HEADS * n_new, n_blocks * LANES), F32),
                        pltpu.VMEM((B_DIM, n_blocks * LANES), BF16)],
    )
    ck = cache_k.transpose(0, 2, 3, 1)
    cv = cache_v.transpose(0, 2, 3, 1)
    cki = cache_kidx.transpose(0, 2, 1)
    r3 = lambda a: a.reshape(dbs, n_new, a.shape[-1])
    out_t = pl.pallas_call(
        functools.partial(_dsa_sample_body, n_new=n_new, past_len=past_len, topk=topk),
        grid_spec=grid_spec,
        out_shape=jax.ShapeDtypeStruct((dbs, B_DIM, HEADS * n_new), F32),
        compiler_params=pltpu.CompilerParams(dimension_semantics=("parallel", "arbitrary"),
                                             vmem_limit_bytes=VMEM_LIMIT),
        name="dsa_sample",
    )(page_table, *([cki] * npp), *([ck] * npp), *([cv] * npp),
      qi_rows, qbd, w_rows, r3(kbf), r3(vbf), r3(smallbf), bias_s)
    o5 = out_t.reshape(dbs, HEADS, HD, HEADS, n_new)
    o = jnp.stack([o5[:, h, :, h, :] for h in range(HEADS)], axis=1)
    return o.transpose(0, 3, 1, 2).reshape(dbs, n_new, B_DIM).astype(BF16)


def _rms(x, g):
    return x * lax.rsqrt(jnp.mean(x * x, axis=-1, keepdims=True) + EPS) * g


def _mlp_body(x_ref, ma_ref, ob_ref, woa_ref, wob_ref, gffn_ref, wg_ref, wu_ref, wd_ref, gfin_ref, y_ref):
    h = x_ref[...] + (_mm(ma_ref[...], woa_ref[...]) + _mm(ob_ref[...], wob_ref[...]))
    hn = _rms(h, gffn_ref[...]).astype(BF16)
    act = _silu(_mm(hn, wg_ref[...])) * _mm(hn, wu_ref[...])
    y_ref[...] = _rms(h + _mm(act.astype(BF16), wd_ref[...]), gfin_ref[...])


def _mlp(x2d, mix_a, o_b, w_out, g_ffn, w_gate, w_up, w_down, g_final, tm):
    n = x2d.shape[0]
    assert n % tm == 0
    row = lambda w: pl.BlockSpec((tm, w), lambda i: (i, 0))
    woa, wob = w_out[:A_DIM].astype(BF16), w_out[A_DIM:].astype(BF16)
    return pl.pallas_call(
        _mlp_body,
        grid=(n // tm,),
        in_specs=[row(D_MODEL), row(A_DIM), row(B_DIM), _const_spec(woa.shape), _const_spec(wob.shape),
                  _const_spec((1, D_MODEL)), _const_spec(w_gate.shape), _const_spec(w_up.shape),
                  _const_spec(w_down.shape), _const_spec((1, D_MODEL))],
        out_specs=row(D_MODEL),
        out_shape=jax.ShapeDtypeStruct((n, D_MODEL), F32),
        compiler_params=pltpu.CompilerParams(dimension_semantics=("parallel",), vmem_limit_bytes=VMEM_LIMIT),
        name="mlp",
    )(x2d, mix_a, o_b, woa, wob, g_ffn.reshape(1, D_MODEL), w_gate.astype(BF16), w_up.astype(BF16),
      w_down.astype(BF16), g_final.reshape(1, D_MODEL))


def kernel(x_prompt, x_sample, cache_k, cache_v, cache_kidx, state_conv, state_ssm, page_table, meta_tokens, norm_mix, w_in, conv_w, a_log, dt_bias, o_norm, w_out, rel_bias, norm_ffn, w_gate, w_up, w_down, norm_final):
    assert w_in.shape[0] == 1, "single-layer stack"
    bsz, seq, _ = x_prompt.shape
    dbs, n_new, _ = x_sample.shape
    t_pad = FRONT + seq
    lo = FRONT - N_META
    past_len = page_table.shape[1] * PAGE
    topk_p = min(TOPK_MAX, seq // 4)
    topk_s = min(TOPK_MAX, (past_len + n_new) // 4)
    w_parts = _split_w_in(w_in[0])
    mlp_w = (w_out[0], norm_ffn[0], w_gate[0], w_up[0], w_down[0], norm_final)
    gdn_w = (conv_w[0], a_log[0], dt_bias[0], o_norm[0])

    meta = jnp.broadcast_to(meta_tokens.astype(F32)[None], (bsz, N_META, D_MODEL))
    xp = jnp.concatenate([jnp.zeros((bsz, lo, D_MODEL), F32), meta, x_prompt], axis=1).reshape(bsz * t_pad, D_MODEL)
    qkva, z, qb, k, v, _, vbf, qi, small, _, small_t, small_tbf, kt = _proj(xp, norm_mix[0], w_parts, 512)
    mix_a, ssm_p = _gdn_prompt(qkva, z, small, small_t, *gdn_w, bsz, t_pad, lo)
    bias_tab = _bias_tab(rel_bias, t_pad // LANES)
    o_b = _dsa_prompt(qb, qi, small, kt, vbf, small_tbf, bias_tab, bsz, t_pad, lo, topk_p)
    y_p = _mlp(x_prompt.reshape(bsz * seq, D_MODEL), mix_a.reshape(bsz * seq, A_DIM), o_b.reshape(bsz * seq, B_DIM),
               *mlp_w, 512).reshape(bsz, seq, D_MODEL)
    real = lambda a: a.reshape(bsz, t_pad, a.shape[-1])[:, lo:]
    k_p = real(k).reshape(1, bsz, N_META + seq, HEADS, HD)
    v_p = real(v).reshape(1, bsz, N_META + seq, HEADS, HD)
    kidx_p = real(small)[:, :, S_KIDX:S_KIDX + HD][None]
    conv_p = real(qkva)[:, -(CONV_WIDTH - 1):][None]

    xs = x_sample.reshape(dbs * n_new, D_MODEL)
    (qkva_s, z_s, qb_s, k_s, v_s, kbf_s, vbf_s, qi_s, small_s, smallbf_s, small_t_s, _, _) = _proj(
        xs, norm_mix[0], w_parts, 512)
    mix_a_s, ssm_s = _gdn_sample(qkva_s, z_s, small_s, small_t_s, state_conv[0], state_ssm[0], *gdn_w, n_new)
    bias_s = _bias_sample(rel_bias, past_len, past_len // PAGE + 1)
    o_b_s = _dsa_sample(qb_s, qi_s, small_s, kbf_s, vbf_s, smallbf_s, cache_k[0], cache_v[0], cache_kidx[0], page_table,
                        bias_s, n_new, topk_s)
    y_s = _mlp(xs, mix_a_s, o_b_s.reshape(dbs * n_new, B_DIM), *mlp_w, 512).reshape(dbs, n_new, D_MODEL)
    k_sn = k_s.reshape(1, dbs, n_new, HEADS, HD)
    v_sn = v_s.reshape(1, dbs, n_new, HEADS, HD)
    kidx_s = small_s.reshape(dbs, n_new, LANES)[:, :, S_KIDX:S_KIDX + HD][None]
    conv_s = jnp.concatenate([state_conv[0].astype(F32), qkva_s.reshape(dbs, n_new, 3 * A_DIM)],
                             axis=1)[:, -(CONV_WIDTH - 1):][None]
    return (y_p, y_s, k_p, v_p, kidx_p, conv_p, ssm_p[None], k_sn, v_sn, kidx_s, conv_s, ssm_s[None])
```

```python
import functools
import math

import jax
import jax.numpy as jnp
from jax import lax
from jax.experimental import pallas as pl
from jax.experimental.pallas import tpu as pltpu

F32 = jnp.float32
BF16 = jnp.bfloat16
I32 = jnp.int32
HI = lax.Precision.HIGHEST

D_MODEL = 1024
N_META = 16
HEADS = 8
HD = 64
A_DIM = HEADS * HD
B_DIM = HEADS * HD
CONV_WIDTH = 4
GDN_CHUNK = 64
TOPK_MAX = 256
PAGE = 128
REL_BUCKETS = 32
REL_MAX_DIST = 1024
EPS = 1e-6
LANES = 128
SUBLANES = 8
FRONT = 128
NEG = -1e30
INT_MIN = -2 ** 31
V7X_VMEM_BYTES = 64 * 1024 * 1024
VMEM_LIMIT = V7X_VMEM_BYTES * 7 // 8

S_KIDX, S_BETA, S_A, S_W = 0, 64, 72, 80


def _mm(a, b):
    return jnp.dot(a, b, preferred_element_type=F32)


def _mm_hi(a, b):
    return jnp.dot(a, b, preferred_element_type=F32, precision=HI)


def _split_bf16(x):
    hi = x.astype(BF16)
    return hi, (x - hi.astype(F32)).astype(BF16)


def _nt(a, b, precision=None):
    return lax.dot_general(a, b, (((1,), (1,)), ((), ())), preferred_element_type=F32, precision=precision)


def _const_spec(shape):
    nd = len(shape)
    return pl.BlockSpec(shape, lambda *_: (0,) * nd, pipeline_mode=pl.Buffered(1))


def _silu(x):
    return x * jax.nn.sigmoid(x)


def _proj_body(x_ref, g_ref, wa_ref, wz_ref, wb_ref, wqi_ref, ws_ref, wst_ref,
               qkva_ref, z_ref, qb_ref, k_ref, v_ref, kbf_ref, vbf_ref, qi_ref, small_ref, smallbf_ref, smallt_ref,
               smalltbf_ref, kt_ref):
    x = x_ref[...]
    ms = jnp.mean(x * x, axis=-1, keepdims=True)
    h = (x * lax.rsqrt(ms + EPS) * g_ref[...]).astype(BF16)
    qkva_ref[...] = _mm(h, wa_ref[...])
    z_ref[...] = _mm(h, wz_ref[...])
    qkvb = _mm(h, wb_ref[...])
    qb_ref[...] = (qkvb[:, :B_DIM] * 0.125).astype(BF16)
    k = qkvb[:, B_DIM:2 * B_DIM]
    v = qkvb[:, 2 * B_DIM:]
    k_ref[...] = k
    v_ref[...] = v
    kbf_ref[...] = k.astype(BF16)
    vbf_ref[...] = v.astype(BF16)
    qi_ref[...] = (_mm(h, wqi_ref[...]) * 0.125).astype(BF16)
    s = _mm(h, ws_ref[...])
    small_ref[...] = s
    smallbf_ref[...] = s.astype(BF16)
    t = _nt(wst_ref[...], h)
    smallt_ref[...] = t[:LANES]
    smalltbf_ref[...] = t[:LANES].astype(BF16)
    kt_ref[...] = t[LANES:].astype(BF16)


def _split_w_in(w_in):
    c = 0
    parts = []
    for n in (3 * A_DIM, A_DIM, HEADS, HEADS, 3 * B_DIM, HEADS * HD, HD, HEADS):
        parts.append(w_in[:, c:c + n])
        c += n
    w_qkva, w_z, w_beta, w_a, w_qkvb, w_qi, w_kidx, w_w = parts
    pad = jnp.zeros((w_in.shape[0], LANES - (HD + 3 * HEADS)), w_in.dtype)
    w_small = jnp.concatenate([w_kidx, w_beta, w_a, w_w, pad], axis=1)
    bf = lambda t: t.astype(BF16)
    w_t = jnp.concatenate([w_small, w_qkvb[:, B_DIM:2 * B_DIM]], axis=1).T
    return bf(w_qkva), bf(w_z), bf(w_qkvb), bf(w_qi), bf(w_small), bf(w_t)


def _proj(x2d, g, w_parts, tm):
    n = x2d.shape[0]
    assert n % tm == 0 and tm % LANES == 0
    wa, wz, wb, wqi, ws, wst = w_parts
    row = lambda w: pl.BlockSpec((tm, w), lambda i: (i, 0))
    outs = [
        (jax.ShapeDtypeStruct((n, 3 * A_DIM), F32), row(3 * A_DIM)),
        (jax.ShapeDtypeStruct((n, A_DIM), F32), row(A_DIM)),
        (jax.ShapeDtypeStruct((n, B_DIM), BF16), row(B_DIM)),
        (jax.ShapeDtypeStruct((n, B_DIM), F32), row(B_DIM)),
        (jax.ShapeDtypeStruct((n, B_DIM), F32), row(B_DIM)),
        (jax.ShapeDtypeStruct((n, B_DIM), BF16), row(B_DIM)),
        (jax.ShapeDtypeStruct((n, B_DIM), BF16), row(B_DIM)),
        (jax.ShapeDtypeStruct((n, HEADS * HD), BF16), row(HEADS * HD)),
        (jax.ShapeDtypeStruct((n, LANES), F32), row(LANES)),
        (jax.ShapeDtypeStruct((n, LANES), BF16), row(LANES)),
        (jax.ShapeDtypeStruct((LANES, n), F32), pl.BlockSpec((LANES, tm), lambda i: (0, i))),
        (jax.ShapeDtypeStruct((LANES, n), BF16), pl.BlockSpec((LANES, tm), lambda i: (0, i))),
        (jax.ShapeDtypeStruct((B_DIM, n), BF16), pl.BlockSpec((B_DIM, tm), lambda i: (0, i))),
    ]
    return pl.pallas_call(
        _proj_body,
        grid=(n // tm,),
        in_specs=[row(D_MODEL), _const_spec((1, D_MODEL)), _const_spec(wa.shape), _const_spec(wz.shape),
                  _const_spec(wb.shape), _const_spec(wqi.shape), _const_spec(ws.shape), _const_spec(wst.shape)],
        out_specs=[o[1] for o in outs],
        out_shape=[o[0] for o in outs],
        compiler_params=pltpu.CompilerParams(dimension_semantics=("parallel",), vmem_limit_bytes=VMEM_LIMIT),
        name="proj",
    )(x2d, g.reshape(1, D_MODEL), wa, wz, wb, wqi, ws, wst)


def _gdn_prep(qkv, tail, conv_w, small, small_t, a_log_row, dt_row, a_log_col, dt_col, valid_col, valid_row, chunk):
    rows = qkv.shape[0]
    n_chunks = rows // chunk
    ys = []
    for c in range(n_chunks):
        ext = jnp.concatenate([tail[c], qkv[c * chunk:(c + 1) * chunk]], axis=0)
        y = sum(ext[SUBLANES - (CONV_WIDTH - 1) + j: SUBLANES - (CONV_WIDTH - 1) + j + chunk] * conv_w[j:j + 1]
                for j in range(CONV_WIDTH))
        ys.append(_silu(y))
    y = ys[0] if n_chunks == 1 else jnp.concatenate(ys, axis=0)
    beta_c = jnp.where(valid_col, jax.nn.sigmoid(small), 0.0)
    beta_r = jnp.where(valid_row, jax.nn.sigmoid(small_t), 0.0)
    g_c = jnp.where(valid_col, -jnp.exp(a_log_row) * jax.nn.softplus(small + dt_row), 0.0)
    g_r = jnp.where(valid_row, -jnp.exp(a_log_col) * jax.nn.softplus(small_t + dt_col), 0.0)
    ri = lax.broadcasted_iota(I32, (rows, rows), 0)
    ci = lax.broadcasted_iota(I32, (rows, rows), 1)
    same = (ri // chunk) == (ci // chunk)
    low = jnp.where(same & (ci <= ri), 1.0, 0.0).astype(F32)
    b_c = _mm_hi(low, g_c)
    b_r = _nt(g_r, low, precision=HI)
    return y, beta_c, b_c, b_r


def _bdot(a, b, ca, cb):
    return lax.dot_general(a, b, (((ca,), (cb,)), ((0,), (0,))), preferred_element_type=F32)


def _bmm_solve(a, b):
    a_hi, a_lo = _split_bf16(a)
    b_hi, b_lo = _split_bf16(b)
    return _bdot(a_hi, b_hi, 2, 1) + (_bdot(a_hi, b_lo, 2, 1) + _bdot(a_lo, b_hi, 2, 1))


def _gdn_solve(q, k, v, beta, b_c, b_r, chunk):
    c = chunk
    qn = q * lax.rsqrt(jnp.sum(q * q, axis=-1, keepdims=True) + EPS) * (HD ** -0.5)
    kn = k * lax.rsqrt(jnp.sum(k * k, axis=-1, keepdims=True) + EPS)
    ri = lax.broadcasted_iota(I32, (1, c, c), 1)
    ci = lax.broadcasted_iota(I32, (1, c, c), 2)
    incl = ci <= ri
    decay = jnp.where(incl, jnp.exp(jnp.where(incl, b_c - b_r, 0.0)), 0.0)
    kb, qb16 = kn.astype(BF16), qn.astype(BF16)
    kk = _bdot(kb, kb, 2, 2)
    qk = _bdot(qb16, kb, 2, 2)
    bk = -jnp.where(ci < ri, beta * kk * decay, 0.0)
    eb = jnp.exp(b_c)
    y = jnp.concatenate([beta * v, (beta * eb) * kn], axis=2)
    n_levels = int(math.log2(c))
    for lvl in range(n_levels):
        if lvl + 1 < n_levels:
            prod = _bmm_solve(bk, jnp.concatenate([y, bk], axis=2))
            y, bk = y + prod[:, :, :2 * HD], prod[:, :, 2 * HD:]
        else:
            y = y + _bmm_solve(bk, y)
    b_last = b_c[:, c - 1:c, :]
    return (y[:, :, :HD], y[:, :, HD:].astype(BF16), (qk * decay).astype(BF16), qb16,
            (kn * jnp.exp(b_last - b_c)).astype(BF16), eb, jnp.exp(b_last))


def _gdn_apply(u, w, attn, qb16, kd, eb, eb_last, s0, o_norm, z):
    s0b = s0.astype(BF16)
    delta = u - _bdot(w, s0b, 2, 1)
    deltab = delta.astype(BF16)
    o = eb * _bdot(qb16, s0b, 2, 1) + _bdot(attn, deltab, 2, 1)
    s_new = eb_last * s0 + _bdot(kd, deltab, 1, 1)
    on = o * lax.rsqrt(jnp.mean(o * o, axis=-1, keepdims=True) + EPS) * o_norm
    return s_new, on * _silu(z)


def _gdn_prompt_body(*refs, rows, lo, nb):
    qkv_ref, z_ref, small_ref = refs[:3]
    smallt_refs = refs[3:3 + nb]
    convw_ref, alr_ref, dtr_ref, alc_ref, dtc_ref, onorm_ref, mix_ref, s_ref, tail_scr = refs[3 + nb:]
    t = pl.program_id(1)

    @pl.when(t == 0)
    def _():
        s_ref[...] = jnp.zeros_like(s_ref)
        tail_scr[...] = jnp.zeros_like(tail_scr)

    chunk = GDN_CHUNK
    n_chunks = rows // chunk
    pos_c = t * rows + lax.broadcasted_iota(I32, (rows, LANES), 0)
    pos_r = t * rows + lax.broadcasted_iota(I32, (LANES, rows), 1)
    preps = []
    for s in range(nb):
        qkv = qkv_ref[s]
        tails = [tail_scr[s]] + [qkv[c * chunk - SUBLANES:c * chunk] for c in range(1, n_chunks)]
        preps.append(_gdn_prep(qkv, tails, convw_ref[...], small_ref[s], smallt_refs[s][...],
                               alr_ref[...], dtr_ref[...], alc_ref[...], dtc_ref[...],
                               pos_c >= lo, pos_r >= lo, chunk) + (z_ref[s],))
        tail_scr[s] = qkv[rows - SUBLANES:rows]
    units = [(c, s, h) for c in range(n_chunks) for s in range(nb) for h in range(HEADS)]
    rs = lambda c: slice(c * chunk, (c + 1) * chunk)
    stack = lambda k, off: jnp.stack([preps[s][k][rs(c), off + h * HD:off + (h + 1) * HD] for c, s, h in units])
    col = lambda k, off: jnp.stack([preps[s][k][rs(c), off + h:off + h + 1] for c, s, h in units])
    sol = _gdn_solve(stack(0, 0), stack(0, A_DIM), stack(0, 2 * A_DIM), col(1, S_BETA), col(2, S_A),
                     jnp.stack([preps[s][3][S_A + h:S_A + h + 1, rs(c)] for c, s, h in units]), chunk)
    z3 = stack(4, 0)
    o_norm = onorm_ref[...]
    per_chunk = nb * HEADS
    for c in range(n_chunks):
        g = slice(c * per_chunk, (c + 1) * per_chunk)
        s_new, out = _gdn_apply(*[a[g] for a in sol], s_ref[...].reshape(per_chunk, HD, HD), o_norm, z3[g])
        s_ref[...] = s_new.reshape(nb, HEADS, HD, HD)
        for s in range(nb):
            for h in range(HEADS):
                mix_ref[s, rs(c), h * HD:(h + 1) * HD] = out[s * HEADS + h].astype(mix_ref.dtype)


GDN_SEQS_PER_STEP = 4


def _gate_params(a_log, dt_bias):
    row = lambda v: jnp.zeros((1, LANES), F32).at[0, S_A:S_A + HEADS].set(v.astype(F32))
    return row(a_log), row(dt_bias), row(a_log).T, row(dt_bias).T


def _gdn_prompt(qkva, z, small, small_t, conv_w, a_log, dt_bias, o_norm, bsz, t_pad, lo):
    rows = LANES
    nt = t_pad // rows
    nb = GDN_SEQS_PER_STEP if bsz % GDN_SEQS_PER_STEP == 0 else 1
    alr, dtr, alc, dtc = _gate_params(a_log, dt_bias)
    body = functools.partial(_gdn_prompt_body, rows=rows, lo=lo, nb=nb)
    seqs = lambda w: pl.BlockSpec((nb, rows, w), lambda b, t: (b, t, 0))
    mix, s = pl.pallas_call(
        body,
        grid=(bsz // nb, nt),
        in_specs=[seqs(3 * A_DIM), seqs(A_DIM), seqs(LANES)] + [
            pl.BlockSpec((LANES, rows), functools.partial(lambda b, t, s: (0, (b * nb + s) * nt + t), s=s))
            for s in range(nb)] + [
            _const_spec((CONV_WIDTH, 3 * A_DIM)), _const_spec((1, LANES)), _const_spec((1, LANES)),
            _const_spec((LANES, 1)), _const_spec((LANES, 1)), _const_spec((1, HD)),
        ],
        out_specs=[
            pl.BlockSpec((nb, rows, A_DIM), lambda b, t: (b, jnp.maximum(t - 1, 0), 0)),
            pl.BlockSpec((nb, HEADS, HD, HD), lambda b, t: (b, 0, 0, 0)),
        ],
        out_shape=[jax.ShapeDtypeStruct((bsz, t_pad - FRONT, A_DIM), BF16),
                   jax.ShapeDtypeStruct((bsz, HEADS, HD, HD), F32)],
        scratch_shapes=[pltpu.VMEM((nb, SUBLANES, 3 * A_DIM), F32)],
        compiler_params=pltpu.CompilerParams(dimension_semantics=("parallel", "arbitrary"),
                                             vmem_limit_bytes=VMEM_LIMIT),
        name="gdn_prompt",
    )(qkva.reshape(bsz, t_pad, 3 * A_DIM), z.reshape(bsz, t_pad, A_DIM), small.reshape(bsz, t_pad, LANES),
      *([small_t] * nb), conv_w, alr, dtr, alc, dtc, o_norm.reshape(1, HD))
    return mix, s


def _gdn_sample_body(qkv_ref, z_ref, small_ref, smallt_ref, conv0_ref, s0_ref, convw_ref, alr_ref, dtr_ref, alc_ref,
                     dtc_ref, onorm_ref, mix_ref, s_ref, *, chunk):
    rows = LANES
    n_seq = rows // chunk
    qkv = qkv_ref[...]
    tails = [conv0_ref[i] for i in range(n_seq)]
    y, beta_c, b_c, b_r = _gdn_prep(qkv, tails, convw_ref[...], small_ref[...], smallt_ref[...],
                                    alr_ref[...], dtr_ref[...], alc_ref[...], dtc_ref[...], True, True, chunk)
    z = z_ref[...]
    per_seq = lambda a: a.reshape(n_seq, chunk, a.shape[-1])
    stack = lambda a, off: jnp.concatenate([per_seq(a[:, off + h * HD:off + (h + 1) * HD]) for h in range(HEADS)])
    col = lambda a, off: jnp.concatenate([per_seq(a[:, off + h:off + h + 1]) for h in range(HEADS)])
    b_r3 = jnp.stack([b_r[S_A + h:S_A + h + 1, i * chunk:(i + 1) * chunk] for h in range(HEADS) for i in range(n_seq)])
    sol = _gdn_solve(stack(y, 0), stack(y, A_DIM), stack(y, 2 * A_DIM), col(beta_c, S_BETA), col(b_c, S_A), b_r3, chunk)
    s0 = jnp.concatenate([s0_ref[:, h] for h in range(HEADS)])
    s_new, out = _gdn_apply(*sol, s0, onorm_ref[...], stack(z, 0))
    for h in range(HEADS):
        g = slice(h * n_seq, (h + 1) * n_seq)
        s_ref[:, h] = s_new[g]
        mix_ref[:, h * HD:(h + 1) * HD] = out[g].reshape(rows, HD).astype(mix_ref.dtype)


def _gdn_sample(qkva, z, small, small_t, state_conv, state_ssm, conv_w, a_log, dt_bias, o_norm, n_new):
    n = qkva.shape[0]
    bsz = n // n_new
    rows = LANES
    n_seq = rows // n_new
    assert n_new == SUBLANES and n % rows == 0
    alr, dtr, alc, dtc = _gate_params(a_log, dt_bias)
    conv0 = jnp.pad(state_conv.astype(F32), ((0, 0), (SUBLANES - (CONV_WIDTH - 1), 0), (0, 0)))
    body = functools.partial(_gdn_sample_body, chunk=n_new)
    mix, s = pl.pallas_call(
        body,
        grid=(n // rows,),
        in_specs=[
            pl.BlockSpec((rows, 3 * A_DIM), lambda g: (g, 0)),
            pl.BlockSpec((rows, A_DIM), lambda g: (g, 0)),
            pl.BlockSpec((rows, LANES), lambda g: (g, 0)),
            pl.BlockSpec((LANES, rows), lambda g: (0, g)),
            pl.BlockSpec((n_seq, SUBLANES, 3 * A_DIM), lambda g: (g, 0, 0)),
            pl.BlockSpec((n_seq, HEADS, HD, HD), lambda g: (g, 0, 0, 0)),
            _const_spec((CONV_WIDTH, 3 * A_DIM)), _const_spec((1, LANES)), _const_spec((1, LANES)),
            _const_spec((LANES, 1)), _const_spec((LANES, 1)), _const_spec((1, HD)),
        ],
        out_specs=[
            pl.BlockSpec((rows, A_DIM), lambda g: (g, 0)),
            pl.BlockSpec((n_seq, HEADS, HD, HD), lambda g: (g, 0, 0, 0)),
        ],
        out_shape=[jax.ShapeDtypeStruct((n, A_DIM), BF16), jax.ShapeDtypeStruct((bsz, HEADS, HD, HD), F32)],
        compiler_params=pltpu.CompilerParams(dimension_semantics=("parallel",), vmem_limit_bytes=VMEM_LIMIT),
        name="gdn_sample",
    )(qkva, z, small, small_t, conv0, state_ssm.astype(F32), conv_w, alr, dtr, alc, dtc, o_norm.reshape(1, HD))
    return mix, s


def _rel_bucket(d):
    d = jnp.maximum(d, 0)
    max_exact = REL_BUCKETS // 2
    df = jnp.maximum(d, 1).astype(F32)
    large = max_exact + (jnp.log(df / max_exact) / math.log(REL_MAX_DIST / max_exact)
                         * (REL_BUCKETS - max_exact)).astype(I32)
    large = jnp.minimum(large, REL_BUCKETS - 1)
    return jnp.where(d < max_exact, d, large)


def _bias_lookup(bucket, rb_ref, h):
    acc = jnp.zeros(bucket.shape, F32)
    for b in range(REL_BUCKETS):
        acc = jnp.where(bucket == b, rb_ref[b, h], acc)
    return acc


def _bias_tab_body(rb_ref, out_ref):
    dlt = pl.program_id(0)
    i = lax.broadcasted_iota(I32, (LANES, LANES), 0)
    j = lax.broadcasted_iota(I32, (LANES, LANES), 1)
    bucket = _rel_bucket(dlt * LANES + i - j)
    for h in range(HEADS):
        out_ref[h, 0] = _bias_lookup(bucket, rb_ref, h)


def _bias_tab(rel_bias, n_tiles):
    return pl.pallas_call(
        _bias_tab_body,
        grid=(n_tiles,),
        in_specs=[pl.BlockSpec(memory_space=pltpu.SMEM)],
        out_specs=pl.BlockSpec((HEADS, 1, LANES, LANES), lambda t: (0, t, 0, 0)),
        out_shape=jax.ShapeDtypeStruct((HEADS, n_tiles, LANES, LANES), F32),
        name="bias_tab",
    )(rel_bias.astype(F32))


def _bias_sample_body(rb_ref, out_ref, *, past_len):
    blk = pl.program_id(0)
    q = lax.broadcasted_iota(I32, (SUBLANES, LANES), 0)
    s = blk * LANES + lax.broadcasted_iota(I32, (SUBLANES, LANES), 1)
    bucket = _rel_bucket(past_len + q - s)
    for h in range(HEADS):
        out_ref[h * SUBLANES:(h + 1) * SUBLANES, :] = _bias_lookup(bucket, rb_ref, h)


def _bias_sample(rel_bias, past_len, n_blocks):
    return pl.pallas_call(
        functools.partial(_bias_sample_body, past_len=past_len),
        grid=(n_blocks,),
        in_specs=[pl.BlockSpec(memory_space=pltpu.SMEM)],
        out_specs=pl.BlockSpec((HEADS * SUBLANES, LANES), lambda t: (0, t)),
        out_shape=jax.ShapeDtypeStruct((HEADS * SUBLANES, n_blocks * LANES), F32),
        name="bias_sample",
    )(rel_bias.astype(F32))


def _sort_key(score, admissible):
    score = jnp.where(score == 0.0, 0.0, score)
    bits = pltpu.bitcast(score, I32)
    key = jnp.where(bits < 0, bits ^ 0x7FFFFFFF, bits)
    return jnp.where(admissible, key, INT_MIN)


def _kth_largest_key(count_ge, rows, topk, two_bits=False):
    count = lambda cand: jnp.broadcast_to(count_ge(cand), (rows, LANES))
    c0 = count(jnp.zeros((rows, LANES), I32))
    t0 = jnp.where(c0 >= topk, 0, INT_MIN).astype(I32)
    c0 = jnp.where(c0 >= topk, c0, float(2 ** 30))

    def accept(carry, cand, cc):
        t, c = carry
        ok = cc >= topk
        return jnp.where(ok, cand, t), jnp.where(ok, cc, c)

    def bit_step(i, carry):
        cand = carry[0] + lax.shift_left(jnp.int32(1), 30 - i)
        return accept(carry, cand, count(cand))

    def two_bit_step(i, carry):
        unit = lax.shift_left(jnp.int32(1), 29 - 2 * i)
        cands = [carry[0] + m * unit for m in (1, 2, 3)]
        counts = [count(cand) for cand in cands]
        for cand, cc in zip(cands, counts):
            carry = accept(carry, cand, cc)
        return carry

    if not two_bits:
        return lax.fori_loop(0, 31, bit_step, (t0, c0))
    carry = lax.fori_loop(0, 15, two_bit_step, (t0, c0))
    return bit_step(30, carry)


def _tie_mask(key, thr, need, running, tri):
    eq = (key == thr) & (key != INT_MIN)
    pref = _mm(jnp.where(eq, 1.0, 0.0).astype(BF16), tri) + running
    sel = (key > thr) | (eq & (pref <= need))
    return sel, jnp.broadcast_to(pref[:, LANES - 1:LANES], pref.shape)


def _tri_incl():
    r = lax.broadcasted_iota(I32, (LANES, LANES), 0)
    c = lax.broadcasted_iota(I32, (LANES, LANES), 1)
    return jnp.where(r <= c, 1.0, 0.0).astype(BF16)


def _dsa_prompt_tiles(first, n_tiles, qi, w_raw, kt_ref, kidxt_ref, bias_ref, out_ref,
                      qh_scr, vh_scr, keys_scr, mbias_scr, o_scr, *, lo, topk):
    nblk = first + n_tiles
    tq = n_tiles * LANES
    wid = nblk * LANES
    w = w_raw * (HEADS ** -0.5)
    tiles = [(slice(r * LANES, (r + 1) * LANES), (first + r + 1) * LANES) for r in range(n_tiles)]
    for r, (rows, wr) in enumerate(tiles):
        t_pos = (first + r) * LANES + lax.broadcasted_iota(I32, (LANES, wr), 0)
        s_pos = lax.broadcasted_iota(I32, (LANES, wr), 1)
        kxt = kidxt_ref[S_KIDX:S_KIDX + HD, 0:wr]
        acc = jnp.zeros((LANES, wr), F32)
        for h in range(HEADS):
            acc = acc + jnp.maximum(_mm(qi[rows, h * HD:(h + 1) * HD], kxt), 0.0) * w[rows, h:h + 1]
        keys_scr[rows, 0:wr] = _sort_key(acc, (s_pos <= t_pos) & (s_pos >= lo))
        if wr < wid:
            keys_scr[rows, wr:wid] = jnp.full((LANES, wid - wr), INT_MIN, I32)

    def count_ge(cand):
        return jnp.sum(jnp.where(keys_scr[:, 0:wid] >= cand[:, 0:1], 1.0, 0.0), axis=1, keepdims=True)

    thr, cnt = _kth_largest_key(count_ge, tq, topk)
    keys = keys_scr[:, 0:wid]
    mbias_scr[:, 0:wid] = jnp.where(keys >= jnp.maximum(thr[:, 0:1], INT_MIN + 1), 0.0, NEG)

    @pl.when(jnp.max(jnp.where(cnt < float(2 ** 30), cnt, 0.0)) > topk)
    def _():
        need = topk - jnp.sum(jnp.where(keys_scr[:, 0:wid] > thr[:, 0:1], 1.0, 0.0), axis=1, keepdims=True)
        tri = _tri_incl()
        running = jnp.zeros((tq, LANES), F32)
        for jb in range(nblk):
            cols = slice(jb * LANES, (jb + 1) * LANES)
            sel, running = _tie_mask(keys_scr[:, cols], thr, need, running, tri)
            mbias_scr[:, cols] = jnp.where(sel, 0.0, NEG)

    def head_attn(h, carry):
        for r, (rows, wr) in enumerate(tiles):
            bias = jnp.concatenate([bias_ref[h, first + r - jb] for jb in range(wr // LANES)], axis=1)
            kt = kt_ref[pl.ds(pl.multiple_of(h * HD, HD), HD), 0:wr]
            lg = _mm(qh_scr[h, rows], kt) + bias + mbias_scr[rows, 0:wr]
            p = jnp.exp(lg - jnp.max(lg, axis=1, keepdims=True))
            inv = 1.0 / jnp.sum(p, axis=1, keepdims=True)
            o_scr[h, rows] = _mm(p.astype(BF16), vh_scr[h, 0:wr, :]) * inv
        return carry

    lax.fori_loop(0, HEADS, head_attn, 0)
    for h in range(HEADS):
        out_ref[0, :, h * HD:(h + 1) * HD] = o_scr[h].astype(out_ref.dtype)


def _dsa_prompt_body(*refs, lo, topk, n_tiles, n_steps):
    qb_refs, qi_refs, small_refs = refs[:n_tiles], refs[n_tiles:2 * n_tiles], refs[2 * n_tiles:3 * n_tiles]
    (kt_ref, vbf_ref, kidxt_ref, bias_ref, out_ref, qh_scr, vh_scr, keys_scr, mbias_scr, o_scr) = refs[3 * n_tiles:]
    js = pl.program_id(1)

    @pl.when(js == 0)
    def _():
        for h in range(HEADS):
            vh_scr[h] = vbf_ref[0, :, h * HD:(h + 1) * HD]

    for r in range(n_tiles):
        for h in range(HEADS):
            qh_scr[h, r * LANES:(r + 1) * LANES, :] = qb_refs[r][0, :, h * HD:(h + 1) * HD]
    qi = jnp.concatenate([ref[0] for ref in qi_refs], axis=0)
    w_raw = jnp.concatenate([ref[0][:, S_W:S_W + HEADS] for ref in small_refs], axis=0)
    for step in range(n_steps):
        @pl.when(js == step)
        def _(step=step):
            _dsa_prompt_tiles(1 + step * n_tiles, n_tiles, qi, w_raw, kt_ref, kidxt_ref, bias_ref, out_ref,
                              qh_scr, vh_scr, keys_scr, mbias_scr, o_scr, lo=lo, topk=topk)


DSA_TILES_PER_STEP = 4


def _dsa_prompt(qb, qi, small, kt, vbf, smalltbf, bias_tab, bsz, t_pad, lo, topk):
    n_real = t_pad // LANES - 1
    n_tiles = max(d for d in (1, 2, DSA_TILES_PER_STEP) if n_real % d == 0)
    n_steps = n_real // n_tiles
    rows = n_tiles * LANES
    r3 = lambda a: a.reshape(bsz, t_pad, a.shape[-1])
    tiles = lambda w: [pl.BlockSpec((1, LANES, w), functools.partial(lambda b, js, r: (b, 1 + js * n_tiles + r, 0), r=r))
                       for r in range(n_tiles)]
    seq_cols = lambda r: pl.BlockSpec((r, t_pad), lambda b, js: (0, b))
    return pl.pallas_call(
        functools.partial(_dsa_prompt_body, lo=lo, topk=topk, n_tiles=n_tiles, n_steps=n_steps),
        grid=(bsz, n_steps),
        in_specs=tiles(B_DIM) + tiles(HEADS * HD) + tiles(LANES) + [
            seq_cols(B_DIM), pl.BlockSpec((1, t_pad, B_DIM), lambda b, js: (b, 0, 0)), seq_cols(LANES),
            _const_spec(bias_tab.shape)],
        out_specs=pl.BlockSpec((1, rows, B_DIM), lambda b, js: (b, js, 0)),
        out_shape=jax.ShapeDtypeStruct((bsz, t_pad - FRONT, B_DIM), BF16),
        scratch_shapes=[pltpu.VMEM((HEADS, rows, HD), BF16), pltpu.VMEM((HEADS, t_pad, HD), BF16),
                        pltpu.VMEM((rows, t_pad), I32), pltpu.VMEM((rows, t_pad), F32),
                        pltpu.VMEM((HEADS, rows, HD), F32)],
        compiler_params=pltpu.CompilerParams(dimension_semantics=("parallel", "arbitrary"),
                                             vmem_limit_bytes=VMEM_LIMIT),
        name="dsa_prompt",
    )(*([r3(qb)] * n_tiles), *([r3(qi)] * n_tiles), *([r3(small)] * n_tiles), kt, r3(vbf), smalltbf, bias_tab)


PAGES_PER_STEP = 16


def _dsa_sample_body(pt_ref, *refs, n_new, past_len, topk):
    npp = PAGES_PER_STEP
    kidx_refs, k_refs, v_refs = refs[:npp], refs[npp:2 * npp], refs[2 * npp:3 * npp]
    (qi_ref, qbd_ref, w_ref, knew_ref, vnew_ref, kidxnew_ref, bias_ref, out_ref,
     keys_scr, mb_scr, logit_scr, vt_scr) = refs[3 * npp:]
    del pt_ref
    s = pl.program_id(1)
    n_page_steps = past_len // (PAGE * npp)
    n_blocks = past_len // PAGE + 1
    qi = qi_ref[0]
    qbd = qbd_ref[0]
    w = w_ref[0] * (HEADS ** -0.5)
    q_row = lax.broadcasted_iota(I32, (n_new, LANES), 0)
    lane = lax.broadcasted_iota(I32, (n_new, LANES), 1)

    def do_block(off, kxt, kt, vt, admissible):
        rel = jnp.maximum(_mm(qi, kxt), 0.0) * w
        sc = rel[0:n_new]
        for h in range(1, HEADS):
            sc = sc + rel[h * n_new:(h + 1) * n_new]
        keys_scr[:, pl.ds(off, LANES)] = _sort_key(sc, admissible)
        logit_scr[:, pl.ds(off, LANES)] = _mm(qbd, kt) + bias_ref[:, pl.ds(off, LANES)]
        vt_scr[:, pl.ds(off, LANES)] = vt

    @pl.when(s < n_page_steps)
    def _():
        for i in range(npp):
            off = pl.multiple_of((s * npp + i) * PAGE, PAGE)
            do_block(off, kidx_refs[i][0].astype(BF16), k_refs[i][0].reshape(B_DIM, PAGE).astype(BF16),
                     v_refs[i][0].reshape(B_DIM, PAGE).astype(BF16), True)

    @pl.when(s == n_page_steps)
    def _():
        tr = lambda a: jnp.concatenate(
            [a.astype(F32), jnp.zeros((PAGE - n_new, a.shape[1]), F32)], axis=0).T.astype(BF16)
        do_block(past_len, tr(kidxnew_ref[0])[S_KIDX:S_KIDX + HD], tr(knew_ref[0]), tr(vnew_ref[0]), lane <= q_row)

        keys = keys_scr[...]
        thr, cnt = _kth_largest_key(
            lambda cand: jnp.sum(jnp.where(keys >= cand[:, 0:1], 1.0, 0.0), axis=1, keepdims=True), n_new, topk,
            two_bits=True)
        mb_scr[...] = jnp.where(keys >= jnp.maximum(thr[:, 0:1], INT_MIN + 1), 0.0, NEG)

        @pl.when(jnp.max(jnp.where(cnt < float(2 ** 30), cnt, 0.0)) > topk)
        def _():
            need = topk - jnp.sum(jnp.where(keys > thr[:, 0:1], 1.0, 0.0), axis=1, keepdims=True)
            tri = _tri_incl()

            def mask_blk(jb, running):
                cols = pl.ds(pl.multiple_of(jb * LANES, LANES), LANES)
                sel, running = _tie_mask(keys_scr[:, cols], thr, need, running, tri)
                mb_scr[:, cols] = jnp.where(sel, 0.0, NEG)
                return running

            lax.fori_loop(0, n_blocks, mask_blk, jnp.zeros((n_new, LANES), F32))

        lg = logit_scr[...] + jnp.concatenate([mb_scr[...]] * HEADS, axis=0)
        m = jnp.max(lg, axis=1, keepdims=True)
        p = jnp.exp(lg - m)
        pb = (p * (1.0 / jnp.sum(p, axis=1, keepdims=True))).astype(BF16)
        out_ref[0] = _nt(vt_scr[...], pb)


def _dsa_sample(qb, qi, small, kbf, vbf, smallbf, cache_k, cache_v, cache_kidx, page_table, bias_s, n_new, topk):
    dbs, n_pages = page_table.shape
    past_len = n_pages * PAGE
    npp = PAGES_PER_STEP
    assert n_pages % npp == 0 and n_new == SUBLANES
    n_page_steps = n_pages // npp
    n_blocks = n_pages + 1
    n_pool = cache_k.shape[0]
    qi_rows = qi.reshape(dbs, n_new, HEADS, HD).transpose(0, 2, 1, 3).reshape(dbs, HEADS * n_new, HD)
    q4 = qb.reshape(dbs, n_new, HEADS, HD).transpose(0, 2, 1, 3)
    eye = jnp.eye(HEADS, dtype=qb.dtype)
    qbd = (q4[:, :, :, None, :] * eye[None, :, None, :, None]).reshape(dbs, HEADS * n_new, B_DIM)
    w_rows = small.reshape(dbs, n_new, LANES)[:, :, S_W:S_W + HEADS].transpose(0, 2, 1).reshape(dbs, HEADS * n_new, 1)
    w_rows = jnp.broadcast_to(w_rows, (dbs, HEADS * n_new, LANES))

    def page_spec(shape, i):
        def imap(b, s, pt):
            return (pt[b, jnp.minimum(s, n_page_steps - 1) * npp + i],) + (0,) * len(shape)
        return pl.BlockSpec((1,) + shape, imap)

    per_seq = lambda r, w: pl.BlockSpec((1, r, w), lambda b, s, pt: (b, 0, 0))
    in_specs = ([page_spec((HD, PAGE), i) for i in range(npp)] + [page_spec((HEADS, HD, PAGE), i) for i in range(npp)]
                + [page_spec((HEADS, HD, PAGE), i) for i in range(npp)]
                + [per_seq(HEADS * n_new, HD), per_seq(HEADS * n_new, B_DIM), per_seq(HEADS * n_new, LANES),
                   per_seq(n_new, B_DIM), per_seq(n_new, B_DIM), per_seq(n_new, LANES),
                   pl.BlockSpec(bias_s.shape, lambda b, s, pt: (0, 0), pipeline_mode=pl.Buffered(1))])
    grid_spec = pltpu.PrefetchScalarGridSpec(
        num_scalar_prefetch=1,
        grid=(dbs, n_page_steps + 1),
        in_specs=in_specs,
        out_specs=pl.BlockSpec((1, B_DIM, HEADS * n_new), lambda b, s, pt: (b, 0, 0)),
        scratch_shapes=[pltpu.VMEM((n_new, n_blocks * LANES), I32), pltpu.VMEM((n_new, n_blocks * LANES), F32),
                        pltpu.VMEM((HEADS * n_new, n_blocks * LANES), F32),
                        pltpu.VMEM((B_DIM, n_blocks * LANES), BF16)],
    )
    ck = cache_k.transpose(0, 2, 3, 1)
    cv = cache_v.transpose(0, 2, 3, 1)
    cki = cache_kidx.transpose(0, 2, 1)
    r3 = lambda a: a.reshape(dbs, n_new, a.shape[-1])
    out_t = pl.pallas_call(
        functools.partial(_dsa_sample_body, n_new=n_new, past_len=past_len, topk=topk),
        grid_spec=grid_spec,
        out_shape=jax.ShapeDtypeStruct((dbs, B_DIM, HEADS * n_new), F32),
        compiler_params=pltpu.CompilerParams(dimension_semantics=("parallel", "arbitrary"),
                                             vmem_limit_bytes=VMEM_LIMIT),
        name="dsa_sample",
    )(page_table, *([cki] * npp), *([ck] * npp), *([cv] * npp),
      qi_rows, qbd, w_rows, r3(kbf), r3(vbf), r3(smallbf), bias_s)
    o5 = out_t.reshape(dbs, HEADS, HD, HEADS, n_new)
    o = jnp.stack([o5[:, h, :, h, :] for h in range(HEADS)], axis=1)
    return o.transpose(0, 3, 1, 2).reshape(dbs, n_new, B_DIM).astype(BF16)


def _rms(x, g):
    return x * lax.rsqrt(jnp.mean(x * x, axis=-1, keepdims=True) + EPS) * g


def _mlp_body(x_ref, ma_ref, ob_ref, woa_ref, wob_ref, gffn_ref, wg_ref, wu_ref, wd_ref, gfin_ref, y_ref):
    h = x_ref[...] + (_mm(ma_ref[...], woa_ref[...]) + _mm(ob_ref[...], wob_ref[...]))
    hn = _rms(h, gffn_ref[...]).astype(BF16)
    act = _silu(_mm(hn, wg_ref[...])) * _mm(hn, wu_ref[...])
    y_ref[...] = _rms(h + _mm(act.astype(BF16), wd_ref[...]), gfin_ref[...])


def _mlp(x2d, mix_a, o_b, w_out, g_ffn, w_gate, w_up, w_down, g_final, tm):
    n = x2d.shape[0]
    assert n % tm == 0
    row = lambda w: pl.BlockSpec((tm, w), lambda i: (i, 0))
    woa, wob = w_out[:A_DIM].astype(BF16), w_out[A_DIM:].astype(BF16)
    return pl.pallas_call(
        _mlp_body,
        grid=(n // tm,),
        in_specs=[row(D_MODEL), row(A_DIM), row(B_DIM), _const_spec(woa.shape), _const_spec(wob.shape),
                  _const_spec((1, D_MODEL)), _const_spec(w_gate.shape), _const_spec(w_up.shape),
                  _const_spec(w_down.shape), _const_spec((1, D_MODEL))],
        out_specs=row(D_MODEL),
        out_shape=jax.ShapeDtypeStruct((n, D_MODEL), F32),
        compiler_params=pltpu.CompilerParams(dimension_semantics=("parallel",), vmem_limit_bytes=VMEM_LIMIT),
        name="mlp",
    )(x2d, mix_a, o_b, woa, wob, g_ffn.reshape(1, D_MODEL), w_gate.astype(BF16), w_up.astype(BF16),
      w_down.astype(BF16), g_final.reshape(1, D_MODEL))


def kernel(x_prompt, x_sample, cache_k, cache_v, cache_kidx, state_conv, state_ssm, page_table, meta_tokens, norm_mix, w_in, conv_w, a_log, dt_bias, o_norm, w_out, rel_bias, norm_ffn, w_gate, w_up, w_down, norm_final):
    assert w_in.shape[0] == 1, "single-layer stack"
    bsz, seq, _ = x_prompt.shape
    dbs, n_new, _ = x_sample.shape
    t_pad = FRONT + seq
    lo = FRONT - N_META
    past_len = page_table.shape[1] * PAGE
    topk_p = min(TOPK_MAX, seq // 4)
    topk_s = min(TOPK_MAX, (past_len + n_new) // 4)
    w_parts = _split_w_in(w_in[0])
    mlp_w = (w_out[0], norm_ffn[0], w_gate[0], w_up[0], w_down[0], norm_final)
    gdn_w = (conv_w[0], a_log[0], dt_bias[0], o_norm[0])

    meta = jnp.broadcast_to(meta_tokens.astype(F32)[None], (bsz, N_META, D_MODEL))
    xp = jnp.concatenate([jnp.zeros((bsz, lo, D_MODEL), F32), meta, x_prompt], axis=1).reshape(bsz * t_pad, D_MODEL)
    qkva, z, qb, k, v, _, vbf, qi, small, _, small_t, small_tbf, kt = _proj(xp, norm_mix[0], w_parts, 512)
    mix_a, ssm_p = _gdn_prompt(qkva, z, small, small_t, *gdn_w, bsz, t_pad, lo)
    bias_tab = _bias_tab(rel_bias, t_pad // LANES)
    o_b = _dsa_prompt(qb, qi, small, kt, vbf, small_tbf, bias_tab, bsz, t_pad, lo, topk_p)
    y_p = _mlp(x_prompt.reshape(bsz * seq, D_MODEL), mix_a.reshape(bsz * seq, A_DIM), o_b.reshape(bsz * seq, B_DIM),
               *mlp_w, 512).reshape(bsz, seq, D_MODEL)
    real = lambda a: a.reshape(bsz, t_pad, a.shape[-1])[:, lo:]
    k_p = real(k).reshape(1, bsz, N_META + seq, HEADS, HD)
    v_p = real(v).reshape(1, bsz, N_META + seq, HEADS, HD)
    kidx_p = real(small)[:, :, S_KIDX:S_KIDX + HD][None]
    conv_p = real(qkva)[:, -(CONV_WIDTH - 1):][None]

    xs = x_sample.reshape(dbs * n_new, D_MODEL)
    (qkva_s, z_s, qb_s, k_s, v_s, kbf_s, vbf_s, qi_s, small_s, smallbf_s, small_t_s, _, _) = _proj(
        xs, norm_mix[0], w_parts, 512)
    mix_a_s, ssm_s = _gdn_sample(qkva_s, z_s, small_s, small_t_s, state_conv[0], state_ssm[0], *gdn_w, n_new)
    bias_s = _bias_sample(rel_bias, past_len, past_len // PAGE + 1)
    o_b_s = _dsa_sample(qb_s, qi_s, small_s, kbf_s, vbf_s, smallbf_s, cache_k[0], cache_v[0], cache_kidx[0], page_table,
                        bias_s, n_new, topk_s)
    y_s = _mlp(xs, mix_a_s, o_b_s.reshape(dbs * n_new, B_DIM), *mlp_w, 512).reshape(dbs, n_new, D_MODEL)
    k_sn = k_s.reshape(1, dbs, n_new, HEADS, HD)
    v_sn = v_s.reshape(1, dbs, n_new, HEADS, HD)
    kidx_s = small_s.reshape(dbs, n_new, LANES)[:, :, S_KIDX:S_KIDX + HD][None]
    conv_s = jnp.concatenate([state_conv[0].astype(F32), qkva_s.reshape(dbs, n_new, 3 * A_DIM)],
                             axis=1)[:, -(CONV_WIDTH - 1):][None]
    return (y_p, y_s, k_p, v_p, kidx_p, conv_p, ssm_p[None], k_sn, v_sn, kidx_s, conv_s, ssm_s[None])
```

```python
import functools
import math

import jax
import jax.numpy as jnp
from jax import lax
from jax.experimental import pallas as pl
from jax.experimental.pallas import tpu as pltpu

F32 = jnp.float32
BF16 = jnp.bfloat16
I32 = jnp.int32
HI = lax.Precision.HIGHEST

D_MODEL = 1024
N_META = 16
HEADS = 8
HD = 64
A_DIM = HEADS * HD
B_DIM = HEADS * HD
CONV_WIDTH = 4
GDN_CHUNK = 64
TOPK_MAX = 256
PAGE = 128
REL_BUCKETS = 32
REL_MAX_DIST = 1024
EPS = 1e-6
LANES = 128
SUBLANES = 8
FRONT = 128
NEG = -1e30
INT_MIN = -2 ** 31
V7X_VMEM_BYTES = 64 * 1024 * 1024
VMEM_LIMIT = V7X_VMEM_BYTES * 7 // 8

S_KIDX, S_BETA, S_A, S_W = 0, 64, 72, 80


def _mm(a, b):
    return jnp.dot(a, b, preferred_element_type=F32)


def _mm_hi(a, b):
    return jnp.dot(a, b, preferred_element_type=F32, precision=HI)


def _split_bf16(x):
    hi = x.astype(BF16)
    return hi, (x - hi.astype(F32)).astype(BF16)


def _nt(a, b, precision=None):
    return lax.dot_general(a, b, (((1,), (1,)), ((), ())), preferred_element_type=F32, precision=precision)


def _const_spec(shape):
    nd = len(shape)
    return pl.BlockSpec(shape, lambda *_: (0,) * nd, pipeline_mode=pl.Buffered(1))


def _silu(x):
    return x * jax.nn.sigmoid(x)


def _proj_body(x_ref, g_ref, wa_ref, wz_ref, wb_ref, wqi_ref, ws_ref, wst_ref,
               qkva_ref, z_ref, qb_ref, k_ref, v_ref, kbf_ref, vbf_ref, qi_ref, small_ref, smallbf_ref, smallt_ref,
               smalltbf_ref, kt_ref):
    x = x_ref[...]
    ms = jnp.mean(x * x, axis=-1, keepdims=True)
    h = (x * lax.rsqrt(ms + EPS) * g_ref[...]).astype(BF16)
    qkva_ref[...] = _mm(h, wa_ref[...])
    z_ref[...] = _mm(h, wz_ref[...])
    qkvb = _mm(h, wb_ref[...])
    qb_ref[...] = (qkvb[:, :B_DIM] * 0.125).astype(BF16)
    k = qkvb[:, B_DIM:2 * B_DIM]
    v = qkvb[:, 2 * B_DIM:]
    k_ref[...] = k
    v_ref[...] = v
    kbf_ref[...] = k.astype(BF16)
    vbf_ref[...] = v.astype(BF16)
    qi_ref[...] = (_mm(h, wqi_ref[...]) * 0.125).astype(BF16)
    s = _mm(h, ws_ref[...])
    small_ref[...] = s
    smallbf_ref[...] = s.astype(BF16)
    t = _nt(wst_ref[...], h)
    smallt_ref[...] = t[:LANES]
    smalltbf_ref[...] = t[:LANES].astype(BF16)
    kt_ref[...] = t[LANES:].astype(BF16)


def _split_w_in(w_in):
    c = 0
    parts = []
    for n in (3 * A_DIM, A_DIM, HEADS, HEADS, 3 * B_DIM, HEADS * HD, HD, HEADS):
        parts.append(w_in[:, c:c + n])
        c += n
    w_qkva, w_z, w_beta, w_a, w_qkvb, w_qi, w_kidx, w_w = parts
    pad = jnp.zeros((w_in.shape[0], LANES - (HD + 3 * HEADS)), w_in.dtype)
    w_small = jnp.concatenate([w_kidx, w_beta, w_a, w_w, pad], axis=1)
    bf = lambda t: t.astype(BF16)
    w_t = jnp.concatenate([w_small, w_qkvb[:, B_DIM:2 * B_DIM]], axis=1).T
    return bf(w_qkva), bf(w_z), bf(w_qkvb), bf(w_qi), bf(w_small), bf(w_t)


def _proj(x2d, g, w_parts, tm):
    n = x2d.shape[0]
    assert n % tm == 0 and tm % LANES == 0
    wa, wz, wb, wqi, ws, wst = w_parts
    row = lambda w: pl.BlockSpec((tm, w), lambda i: (i, 0))
    outs = [
        (jax.ShapeDtypeStruct((n, 3 * A_DIM), F32), row(3 * A_DIM)),
        (jax.ShapeDtypeStruct((n, A_DIM), F32), row(A_DIM)),
        (jax.ShapeDtypeStruct((n, B_DIM), BF16), row(B_DIM)),
        (jax.ShapeDtypeStruct((n, B_DIM), F32), row(B_DIM)),
        (jax.ShapeDtypeStruct((n, B_DIM), F32), row(B_DIM)),
        (jax.ShapeDtypeStruct((n, B_DIM), BF16), row(B_DIM)),
        (jax.ShapeDtypeStruct((n, B_DIM), BF16), row(B_DIM)),
        (jax.ShapeDtypeStruct((n, HEADS * HD), BF16), row(HEADS * HD)),
        (jax.ShapeDtypeStruct((n, LANES), F32), row(LANES)),
        (jax.ShapeDtypeStruct((n, LANES), BF16), row(LANES)),
        (jax.ShapeDtypeStruct((LANES, n), F32), pl.BlockSpec((LANES, tm), lambda i: (0, i))),
        (jax.ShapeDtypeStruct((LANES, n), BF16), pl.BlockSpec((LANES, tm), lambda i: (0, i))),
        (jax.ShapeDtypeStruct((B_DIM, n), BF16), pl.BlockSpec((B_DIM, tm), lambda i: (0, i))),
    ]
    return pl.pallas_call(
        _proj_body,
        grid=(n // tm,),
        in_specs=[row(D_MODEL), _const_spec((1, D_MODEL)), _const_spec(wa.shape), _const_spec(wz.shape),
                  _const_spec(wb.shape), _const_spec(wqi.shape), _const_spec(ws.shape), _const_spec(wst.shape)],
        out_specs=[o[1] for o in outs],
        out_shape=[o[0] for o in outs],
        compiler_params=pltpu.CompilerParams(dimension_semantics=("parallel",), vmem_limit_bytes=VMEM_LIMIT),
        name="proj",
    )(x2d, g.reshape(1, D_MODEL), wa, wz, wb, wqi, ws, wst)


def _gdn_prep(qkv, tail, conv_w, small, small_t, a_log_row, dt_row, a_log_col, dt_col, valid_col, valid_row, chunk):
    rows = qkv.shape[0]
    n_chunks = rows // chunk
    ys = []
    for c in range(n_chunks):
        ext = jnp.concatenate([tail[c], qkv[c * chunk:(c + 1) * chunk]], axis=0)
        y = sum(ext[SUBLANES - (CONV_WIDTH - 1) + j: SUBLANES - (CONV_WIDTH - 1) + j + chunk] * conv_w[j:j + 1]
                for j in range(CONV_WIDTH))
        ys.append(_silu(y))
    y = ys[0] if n_chunks == 1 else jnp.concatenate(ys, axis=0)
    beta_c = jnp.where(valid_col, jax.nn.sigmoid(small), 0.0)
    beta_r = jnp.where(valid_row, jax.nn.sigmoid(small_t), 0.0)
    g_c = jnp.where(valid_col, -jnp.exp(a_log_row) * jax.nn.softplus(small + dt_row), 0.0)
    g_r = jnp.where(valid_row, -jnp.exp(a_log_col) * jax.nn.softplus(small_t + dt_col), 0.0)
    ri = lax.broadcasted_iota(I32, (rows, rows), 0)
    ci = lax.broadcasted_iota(I32, (rows, rows), 1)
    same = (ri // chunk) == (ci // chunk)
    low = jnp.where(same & (ci <= ri), 1.0, 0.0).astype(F32)
    b_c = _mm_hi(low, g_c)
    b_r = _nt(g_r, low, precision=HI)
    return y, beta_c, b_c, b_r


def _bdot(a, b, ca, cb):
    return lax.dot_general(a, b, (((ca,), (cb,)), ((0,), (0,))), preferred_element_type=F32)


def _bmm_solve(a, b):
    a_hi, a_lo = _split_bf16(a)
    b_hi, b_lo = _split_bf16(b)
    return _bdot(a_hi, b_hi, 2, 1) + (_bdot(a_hi, b_lo, 2, 1) + _bdot(a_lo, b_hi, 2, 1))


def _gdn_solve(q, k, v, beta, b_c, b_r, chunk):
    c = chunk
    qn = q * lax.rsqrt(jnp.sum(q * q, axis=-1, keepdims=True) + EPS) * (HD ** -0.5)
    kn = k * lax.rsqrt(jnp.sum(k * k, axis=-1, keepdims=True) + EPS)
    ri = lax.broadcasted_iota(I32, (1, c, c), 1)
    ci = lax.broadcasted_iota(I32, (1, c, c), 2)
    incl = ci <= ri
    decay = jnp.where(incl, jnp.exp(jnp.where(incl, b_c - b_r, 0.0)), 0.0)
    kb, qb16 = kn.astype(BF16), qn.astype(BF16)
    kk = _bdot(kb, kb, 2, 2)
    qk = _bdot(qb16, kb, 2, 2)
    bk = -jnp.where(ci < ri, beta * kk * decay, 0.0)
    eb = jnp.exp(b_c)
    y = jnp.concatenate([beta * v, (beta * eb) * kn], axis=2)
    n_levels = int(math.log2(c))
    for lvl in range(n_levels):
        if lvl + 1 < n_levels:
            prod = _bmm_solve(bk, jnp.concatenate([y, bk], axis=2))
            y, bk = y + prod[:, :, :2 * HD], prod[:, :, 2 * HD:]
        else:
            y = y + _bmm_solve(bk, y)
    b_last = b_c[:, c - 1:c, :]
    return (y[:, :, :HD], y[:, :, HD:].astype(BF16), (qk * decay).astype(BF16), qb16,
            (kn * jnp.exp(b_last - b_c)).astype(BF16), eb, jnp.exp(b_last))


def _gdn_apply(u, w, attn, qb16, kd, eb, eb_last, s0, o_norm, z):
    s0b = s0.astype(BF16)
    delta = u - _bdot(w, s0b, 2, 1)
    deltab = delta.astype(BF16)
    o = eb * _bdot(qb16, s0b, 2, 1) + _bdot(attn, deltab, 2, 1)
    s_new = eb_last * s0 + _bdot(kd, deltab, 1, 1)
    on = o * lax.rsqrt(jnp.mean(o * o, axis=-1, keepdims=True) + EPS) * o_norm
    return s_new, on * _silu(z)


def _gdn_prompt_body(*refs, rows, lo, nb):
    qkv_ref, z_ref, small_ref = refs[:3]
    smallt_refs = refs[3:3 + nb]
    convw_ref, alr_ref, dtr_ref, alc_ref, dtc_ref, onorm_ref, mix_ref, s_ref, tail_scr = refs[3 + nb:]
    t = pl.program_id(1)

    @pl.when(t == 0)
    def _():
        s_ref[...] = jnp.zeros_like(s_ref)
        tail_scr[...] = jnp.zeros_like(tail_scr)

    chunk = GDN_CHUNK
    n_chunks = rows // chunk
    pos_c = t * rows + lax.broadcasted_iota(I32, (rows, LANES), 0)
    pos_r = t * rows + lax.broadcasted_iota(I32, (LANES, rows), 1)
    preps = []
    for s in range(nb):
        qkv = qkv_ref[s]
        tails = [tail_scr[s]] + [qkv[c * chunk - SUBLANES:c * chunk] for c in range(1, n_chunks)]
        preps.append(_gdn_prep(qkv, tails, convw_ref[...], small_ref[s], smallt_refs[s][...],
                               alr_ref[...], dtr_ref[...], alc_ref[...], dtc_ref[...],
                               pos_c >= lo, pos_r >= lo, chunk) + (z_ref[s],))
        tail_scr[s] = qkv[rows - SUBLANES:rows]
    units = [(c, s, h) for c in range(n_chunks) for s in range(nb) for h in range(HEADS)]
    rs = lambda c: slice(c * chunk, (c + 1) * chunk)
    stack = lambda k, off: jnp.stack([preps[s][k][rs(c), off + h * HD:off + (h + 1) * HD] for c, s, h in units])
    col = lambda k, off: jnp.stack([preps[s][k][rs(c), off + h:off + h + 1] for c, s, h in units])
    sol = _gdn_solve(stack(0, 0), stack(0, A_DIM), stack(0, 2 * A_DIM), col(1, S_BETA), col(2, S_A),
                     jnp.stack([preps[s][3][S_A + h:S_A + h + 1, rs(c)] for c, s, h in units]), chunk)
    z3 = stack(4, 0)
    o_norm = onorm_ref[...]
    per_chunk = nb * HEADS
    for c in range(n_chunks):
        g = slice(c * per_chunk, (c + 1) * per_chunk)
        s_new, out = _gdn_apply(*[a[g] for a in sol], s_ref[...].reshape(per_chunk, HD, HD), o_norm, z3[g])
        s_ref[...] = s_new.reshape(nb, HEADS, HD, HD)
        for s in range(nb):
            for h in range(HEADS):
                mix_ref[s, rs(c), h * HD:(h + 1) * HD] = out[s * HEADS + h].astype(mix_ref.dtype)


GDN_SEQS_PER_STEP = 4


def _gate_params(a_log, dt_bias):
    row = lambda v: jnp.zeros((1, LANES), F32).at[0, S_A:S_A + HEADS].set(v.astype(F32))
    return row(a_log), row(dt_bias), row(a_log).T, row(dt_bias).T


def _gdn_prompt(qkva, z, small, small_t, conv_w, a_log, dt_bias, o_norm, bsz, t_pad, lo):
    rows = LANES
    nt = t_pad // rows
    nb = GDN_SEQS_PER_STEP if bsz % GDN_SEQS_PER_STEP == 0 else 1
    alr, dtr, alc, dtc = _gate_params(a_log, dt_bias)
    body = functools.partial(_gdn_prompt_body, rows=rows, lo=lo, nb=nb)
    seqs = lambda w: pl.BlockSpec((nb, rows, w), lambda b, t: (b, t, 0))
    mix, s = pl.pallas_call(
        body,
        grid=(bsz // nb, nt),
        in_specs=[seqs(3 * A_DIM), seqs(A_DIM), seqs(LANES)] + [
            pl.BlockSpec((LANES, rows), functools.partial(lambda b, t, s: (0, (b * nb + s) * nt + t), s=s))
            for s in range(nb)] + [
            _const_spec((CONV_WIDTH, 3 * A_DIM)), _const_spec((1, LANES)), _const_spec((1, LANES)),
            _const_spec((LANES, 1)), _const_spec((LANES, 1)), _const_spec((1, HD)),
        ],
        out_specs=[
            pl.BlockSpec((nb, rows, A_DIM), lambda b, t: (b, jnp.maximum(t - 1, 0), 0)),
            pl.BlockSpec((nb, HEADS, HD, HD), lambda b, t: (b, 0, 0, 0)),
        ],
        out_shape=[jax.ShapeDtypeStruct((bsz, t_pad - FRONT, A_DIM), BF16),
                   jax.ShapeDtypeStruct((bsz, HEADS, HD, HD), F32)],
        scratch_shapes=[pltpu.VMEM((nb, SUBLANES, 3 * A_DIM), F32)],
        compiler_params=pltpu.CompilerParams(dimension_semantics=("parallel", "arbitrary"),
                                             vmem_limit_bytes=VMEM_LIMIT),
        name="gdn_prompt",
    )(qkva.reshape(bsz, t_pad, 3 * A_DIM), z.reshape(bsz, t_pad, A_DIM), small.reshape(bsz, t_pad, LANES),
      *([small_t] * nb), conv_w, alr, dtr, alc, dtc, o_norm.reshape(1, HD))
    return mix, s


def _gdn_sample_body(qkv_ref, z_ref, small_ref, smallt_ref, conv0_ref, s0_ref, convw_ref, alr_ref, dtr_ref, alc_ref,
                     dtc_ref, onorm_ref, mix_ref, s_ref, *, chunk):
    rows = LANES
    n_seq = rows // chunk
    qkv = qkv_ref[...]
    tails = [conv0_ref[i] for i in range(n_seq)]
    y, beta_c, b_c, b_r = _gdn_prep(qkv, tails, convw_ref[...], small_ref[...], smallt_ref[...],
                                    alr_ref[...], dtr_ref[...], alc_ref[...], dtc_ref[...], True, True, chunk)
    z = z_ref[...]
    per_seq = lambda a: a.reshape(n_seq, chunk, a.shape[-1])
    stack = lambda a, off: jnp.concatenate([per_seq(a[:, off + h * HD:off + (h + 1) * HD]) for h in range(HEADS)])
    col = lambda a, off: jnp.concatenate([per_seq(a[:, off + h:off + h + 1]) for h in range(HEADS)])
    b_r3 = jnp.stack([b_r[S_A + h:S_A + h + 1, i * chunk:(i + 1) * chunk] for h in range(HEADS) for i in range(n_seq)])
    sol = _gdn_solve(stack(y, 0), stack(y, A_DIM), stack(y, 2 * A_DIM), col(beta_c, S_BETA), col(b_c, S_A), b_r3, chunk)
    s0 = jnp.concatenate([s0_ref[:, h] for h in range(HEADS)])
    s_new, out = _gdn_apply(*sol, s0, onorm_ref[...], stack(z, 0))
    for h in range(HEADS):
        g = slice(h * n_seq, (h + 1) * n_seq)
        s_ref[:, h] = s_new[g]
        mix_ref[:, h * HD:(h + 1) * HD] = out[g].reshape(rows, HD).astype(mix_ref.dtype)


def _gdn_sample(qkva, z, small, small_t, state_conv, state_ssm, conv_w, a_log, dt_bias, o_norm, n_new):
    n = qkva.shape[0]
    bsz = n // n_new
    rows = LANES
    n_seq = rows // n_new
    assert n_new == SUBLANES and n % rows == 0
    alr, dtr, alc, dtc = _gate_params(a_log, dt_bias)
    conv0 = jnp.pad(state_conv.astype(F32), ((0, 0), (SUBLANES - (CONV_WIDTH - 1), 0), (0, 0)))
    body = functools.partial(_gdn_sample_body, chunk=n_new)
    mix, s = pl.pallas_call(
        body,
        grid=(n // rows,),
        in_specs=[
            pl.BlockSpec((rows, 3 * A_DIM), lambda g: (g, 0)),
            pl.BlockSpec((rows, A_DIM), lambda g: (g, 0)),
            pl.BlockSpec((rows, LANES), lambda g: (g, 0)),
            pl.BlockSpec((LANES, rows), lambda g: (0, g)),
            pl.BlockSpec((n_seq, SUBLANES, 3 * A_DIM), lambda g: (g, 0, 0)),
            pl.BlockSpec((n_seq, HEADS, HD, HD), lambda g: (g, 0, 0, 0)),
            _const_spec((CONV_WIDTH, 3 * A_DIM)), _const_spec((1, LANES)), _const_spec((1, LANES)),
            _const_spec((LANES, 1)), _const_spec((LANES, 1)), _const_spec((1, HD)),
        ],
        out_specs=[
            pl.BlockSpec((rows, A_DIM), lambda g: (g, 0)),
            pl.BlockSpec((n_seq, HEADS, HD, HD), lambda g: (g, 0, 0, 0)),
        ],
        out_shape=[jax.ShapeDtypeStruct((n, A_DIM), BF16), jax.ShapeDtypeStruct((bsz, HEADS, HD, HD), F32)],
        compiler_params=pltpu.CompilerParams(dimension_semantics=("parallel",), vmem_limit_bytes=VMEM_LIMIT),
        name="gdn_sample",
    )(qkva, z, small, small_t, conv0, state_ssm.astype(F32), conv_w, alr, dtr, alc, dtc, o_norm.reshape(1, HD))
    return mix, s


def _rel_bucket(d):
    d = jnp.maximum(d, 0)
    max_exact = REL_BUCKETS // 2
    df = jnp.maximum(d, 1).astype(F32)
    large = max_exact + (jnp.log(df / max_exact) / math.log(REL_MAX_DIST / max_exact)
                         * (REL_BUCKETS - max_exact)).astype(I32)
    large = jnp.minimum(large, REL_BUCKETS - 1)
    return jnp.where(d < max_exact, d, large)


def _bias_lookup(bucket, rb_ref, h):
    acc = jnp.zeros(bucket.shape, F32)
    for b in range(REL_BUCKETS):
        acc = jnp.where(bucket == b, rb_ref[b, h], acc)
    return acc


def _bias_tab_body(rb_ref, out_ref):
    dlt = pl.program_id(0)
    i = lax.broadcasted_iota(I32, (LANES, LANES), 0)
    j = lax.broadcasted_iota(I32, (LANES, LANES), 1)
    bucket = _rel_bucket(dlt * LANES + i - j)
    for h in range(HEADS):
        out_ref[h, 0] = _bias_lookup(bucket, rb_ref, h)


def _bias_tab(rel_bias, n_tiles):
    return pl.pallas_call(
        _bias_tab_body,
        grid=(n_tiles,),
        in_specs=[pl.BlockSpec(memory_space=pltpu.SMEM)],
        out_specs=pl.BlockSpec((HEADS, 1, LANES, LANES), lambda t: (0, t, 0, 0)),
        out_shape=jax.ShapeDtypeStruct((HEADS, n_tiles, LANES, LANES), F32),
        name="bias_tab",
    )(rel_bias.astype(F32))


def _bias_sample_body(rb_ref, out_ref, *, past_len):
    blk = pl.program_id(0)
    q = lax.broadcasted_iota(I32, (SUBLANES, LANES), 0)
    s = blk * LANES + lax.broadcasted_iota(I32, (SUBLANES, LANES), 1)
    bucket = _rel_bucket(past_len + q - s)
    for h in range(HEADS):
        out_ref[h * SUBLANES:(h + 1) * SUBLANES, :] = _bias_lookup(bucket, rb_ref, h)


def _bias_sample(rel_bias, past_len, n_blocks):
    return pl.pallas_call(
        functools.partial(_bias_sample_body, past_len=past_len),
        grid=(n_blocks,),
        in_specs=[pl.BlockSpec(memory_space=pltpu.SMEM)],
        out_specs=pl.BlockSpec((HEADS * SUBLANES, LANES), lambda t: (0, t)),
        out_shape=jax.ShapeDtypeStruct((HEADS * SUBLANES, n_blocks * LANES), F32),
        name="bias_sample",
    )(rel_bias.astype(F32))


def _sort_key(score, admissible):
    score = jnp.where(score == 0.0, 0.0, score)
    bits = pltpu.bitcast(score, I32)
    key = jnp.where(bits < 0, bits ^ 0x7FFFFFFF, bits)
    return jnp.where(admissible, key, INT_MIN)


def _kth_largest_key(count_ge, rows, topk, two_bits=False):
    count = lambda cand: jnp.broadcast_to(count_ge(cand), (rows, LANES))
    c0 = count(jnp.zeros((rows, LANES), I32))
    t0 = jnp.where(c0 >= topk, 0, INT_MIN).astype(I32)
    c0 = jnp.where(c0 >= topk, c0, float(2 ** 30))

    def accept(carry, cand, cc):
        t, c = carry
        ok = cc >= topk
        return jnp.where(ok, cand, t), jnp.where(ok, cc, c)

    def bit_step(i, carry):
        cand = carry[0] + lax.shift_left(jnp.int32(1), 30 - i)
        return accept(carry, cand, count(cand))

    def two_bit_step(i, carry):
        unit = lax.shift_left(jnp.int32(1), 29 - 2 * i)
        cands = [carry[0] + m * unit for m in (1, 2, 3)]
        counts = [count(cand) for cand in cands]
        for cand, cc in zip(cands, counts):
            carry = accept(carry, cand, cc)
        return carry

    if not two_bits:
        return lax.fori_loop(0, 31, bit_step, (t0, c0))
    carry = lax.fori_loop(0, 15, two_bit_step, (t0, c0))
    return bit_step(30, carry)


def _tie_mask(key, thr, need, running, tri):
    eq = (key == thr) & (key != INT_MIN)
    pref = _mm(jnp.where(eq, 1.0, 0.0).astype(BF16), tri) + running
    sel = (key > thr) | (eq & (pref <= need))
    return sel, jnp.broadcast_to(pref[:, LANES - 1:LANES], pref.shape)


def _tri_incl():
    r = lax.broadcasted_iota(I32, (LANES, LANES), 0)
    c = lax.broadcasted_iota(I32, (LANES, LANES), 1)
    return jnp.where(r <= c, 1.0, 0.0).astype(BF16)


def _dsa_prompt_tiles(first, n_tiles, qi, w_raw, kt_ref, kidxt_ref, bias_ref, out_ref,
                      qh_scr, vh_scr, keys_scr, mbias_scr, o_scr, *, lo, topk):
    nblk = first + n_tiles
    tq = n_tiles * LANES
    wid = nblk * LANES
    w = w_raw * (HEADS ** -0.5)
    tiles = [(slice(r * LANES, (r + 1) * LANES), (first + r + 1) * LANES) for r in range(n_tiles)]
    for r, (rows, wr) in enumerate(tiles):
        t_pos = (first + r) * LANES + lax.broadcasted_iota(I32, (LANES, wr), 0)
        s_pos = lax.broadcasted_iota(I32, (LANES, wr), 1)
        kxt = kidxt_ref[S_KIDX:S_KIDX + HD, 0:wr]
        acc = jnp.zeros((LANES, wr), F32)
        for h in range(HEADS):
            acc = acc + jnp.maximum(_mm(qi[rows, h * HD:(h + 1) * HD], kxt), 0.0) * w[rows, h:h + 1]
        keys_scr[rows, 0:wr] = _sort_key(acc, (s_pos <= t_pos) & (s_pos >= lo))
        if wr < wid:
            keys_scr[rows, wr:wid] = jnp.full((LANES, wid - wr), INT_MIN, I32)

    def count_ge(cand):
        return jnp.sum(jnp.where(keys_scr[:, 0:wid] >= cand[:, 0:1], 1.0, 0.0), axis=1, keepdims=True)

    thr, cnt = _kth_largest_key(count_ge, tq, topk)
    keys = keys_scr[:, 0:wid]
    mbias_scr[:, 0:wid] = jnp.where(keys >= jnp.maximum(thr[:, 0:1], INT_MIN + 1), 0.0, NEG)

    @pl.when(jnp.max(jnp.where(cnt < float(2 ** 30), cnt, 0.0)) > topk)
    def _():
        need = topk - jnp.sum(jnp.where(keys_scr[:, 0:wid] > thr[:, 0:1], 1.0, 0.0), axis=1, keepdims=True)
        tri = _tri_incl()
        running = jnp.zeros((tq, LANES), F32)
        for jb in range(nblk):
            cols = slice(jb * LANES, (jb + 1) * LANES)
            sel, running = _tie_mask(keys_scr[:, cols], thr, need, running, tri)
            mbias_scr[:, cols] = jnp.where(sel, 0.0, NEG)

    def head_attn(h, carry):
        for r, (rows, wr) in enumerate(tiles):
            bias = jnp.concatenate([bias_ref[h, first + r - jb] for jb in range(wr // LANES)], axis=1)
            kt = kt_ref[pl.ds(pl.multiple_of(h * HD, HD), HD), 0:wr]
            lg = _mm(qh_scr[h, rows], kt) + bias + mbias_scr[rows, 0:wr]
            p = jnp.exp(lg - jnp.max(lg, axis=1, keepdims=True))
            inv = 1.0 / jnp.sum(p, axis=1, keepdims=True)
            o_scr[h, rows] = _mm(p.astype(BF16), vh_scr[h, 0:wr, :]) * inv
        return carry

    lax.fori_loop(0, HEADS, head_attn, 0)
    for h in range(HEADS):
        out_ref[0, :, h * HD:(h + 1) * HD] = o_scr[h].astype(out_ref.dtype)


def _dsa_prompt_body(*refs, lo, topk, n_tiles, n_steps):
    qb_refs, qi_refs, small_refs = refs[:n_tiles], refs[n_tiles:2 * n_tiles], refs[2 * n_tiles:3 * n_tiles]
    (kt_ref, vbf_ref, kidxt_ref, bias_ref, out_ref, qh_scr, vh_scr, keys_scr, mbias_scr, o_scr) = refs[3 * n_tiles:]
    js = pl.program_id(1)

    @pl.when(js == 0)
    def _():
        for h in range(HEADS):
            vh_scr[h] = vbf_ref[0, :, h * HD:(h + 1) * HD]

    for r in range(n_tiles):
        for h in range(HEADS):
            qh_scr[h, r * LANES:(r + 1) * LANES, :] = qb_refs[r][0, :, h * HD:(h + 1) * HD]
    qi = jnp.concatenate([ref[0] for ref in qi_refs], axis=0)
    w_raw = jnp.concatenate([ref[0][:, S_W:S_W + HEADS] for ref in small_refs], axis=0)
    for step in range(n_steps):
        @pl.when(js == step)
        def _(step=step):
            _dsa_prompt_tiles(1 + step * n_tiles, n_tiles, qi, w_raw, kt_ref, kidxt_ref, bias_ref, out_ref,
                              qh_scr, vh_scr, keys_scr, mbias_scr, o_scr, lo=lo, topk=topk)


DSA_TILES_PER_STEP = 4


def _dsa_prompt(qb, qi, small, kt, vbf, smalltbf, bias_tab, bsz, t_pad, lo, topk):
    n_real = t_pad // LANES - 1
    n_tiles = max(d for d in (1, 2, DSA_TILES_PER_STEP) if n_real % d == 0)
    n_steps = n_real // n_tiles
    rows = n_tiles * LANES
    r3 = lambda a: a.reshape(bsz, t_pad, a.shape[-1])
    tiles = lambda w: [pl.BlockSpec((1, LANES, w), functools.partial(lambda b, js, r: (b, 1 + js * n_tiles + r, 0), r=r))
                       for r in range(n_tiles)]
    seq_cols = lambda r: pl.BlockSpec((r, t_pad), lambda b, js: (0, b))
    return pl.pallas_call(
        functools.partial(_dsa_prompt_body, lo=lo, topk=topk, n_tiles=n_tiles, n_steps=n_steps),
        grid=(bsz, n_steps),
        in_specs=tiles(B_DIM) + tiles(HEADS * HD) + tiles(LANES) + [
            seq_cols(B_DIM), pl.BlockSpec((1, t_pad, B_DIM), lambda b, js: (b, 0, 0)), seq_cols(LANES),
            _const_spec(bias_tab.shape)],
        out_specs=pl.BlockSpec((1, rows, B_DIM), lambda b, js: (b, js, 0)),
        out_shape=jax.ShapeDtypeStruct((bsz, t_pad - FRONT, B_DIM), BF16),
        scratch_shapes=[pltpu.VMEM((HEADS, rows, HD), BF16), pltpu.VMEM((HEADS, t_pad, HD), BF16),
                        pltpu.VMEM((rows, t_pad), I32), pltpu.VMEM((rows, t_pad), F32),
                        pltpu.VMEM((HEADS, rows, HD), F32)],
        compiler_params=pltpu.CompilerParams(dimension_semantics=("parallel", "arbitrary"),
                                             vmem_limit_bytes=VMEM_LIMIT),
        name="dsa_prompt",
    )(*([r3(qb)] * n_tiles), *([r3(qi)] * n_tiles), *([r3(small)] * n_tiles), kt, r3(vbf), smalltbf, bias_tab)


PAGES_PER_STEP = 32


def _dsa_sample_body(pt_ref, *refs, n_new, past_len, topk):
    npp = PAGES_PER_STEP
    kidx_refs, k_refs, v_refs = refs[:npp], refs[npp:2 * npp], refs[2 * npp:3 * npp]
    (qi_ref, qbd_ref, w_ref, knew_ref, vnew_ref, kidxnew_ref, bias_ref, out_ref,
     keys_scr, mb_scr, logit_scr, vt_scr) = refs[3 * npp:]
    del pt_ref
    s = pl.program_id(1)
    n_page_steps = past_len // (PAGE * npp)
    n_blocks = past_len // PAGE + 1
    qi = qi_ref[0]
    qbd = qbd_ref[0]
    w = w_ref[0] * (HEADS ** -0.5)
    q_row = lax.broadcasted_iota(I32, (n_new, LANES), 0)
    lane = lax.broadcasted_iota(I32, (n_new, LANES), 1)

    def do_block(off, kxt, kt, vt, admissible):
        rel = jnp.maximum(_mm(qi, kxt), 0.0) * w
        sc = rel[0:n_new]
        for h in range(1, HEADS):
            sc = sc + rel[h * n_new:(h + 1) * n_new]
        keys_scr[:, pl.ds(off, LANES)] = _sort_key(sc, admissible)
        logit_scr[:, pl.ds(off, LANES)] = _mm(qbd, kt) + bias_ref[:, pl.ds(off, LANES)]
        vt_scr[:, pl.ds(off, LANES)] = vt

    @pl.when(s < n_page_steps)
    def _():
        for i in range(npp):
            off = pl.multiple_of((s * npp + i) * PAGE, PAGE)
            do_block(off, kidx_refs[i][0].astype(BF16), k_refs[i][0].reshape(B_DIM, PAGE).astype(BF16),
                     v_refs[i][0].reshape(B_DIM, PAGE).astype(BF16), True)

    @pl.when(s == n_page_steps)
    def _():
        tr = lambda a: jnp.concatenate(
            [a.astype(F32), jnp.zeros((PAGE - n_new, a.shape[1]), F32)], axis=0).T.astype(BF16)
        do_block(past_len, tr(kidxnew_ref[0])[S_KIDX:S_KIDX + HD], tr(knew_ref[0]), tr(vnew_ref[0]), lane <= q_row)

        keys = keys_scr[...]
        thr, cnt = _kth_largest_key(
            lambda cand: jnp.sum(jnp.where(keys >= cand[:, 0:1], 1.0, 0.0), axis=1, keepdims=True), n_new, topk,
            two_bits=True)
        mb_scr[...] = jnp.where(keys >= jnp.maximum(thr[:, 0:1], INT_MIN + 1), 0.0, NEG)

        @pl.when(jnp.max(jnp.where(cnt < float(2 ** 30), cnt, 0.0)) > topk)
        def _():
            need = topk - jnp.sum(jnp.where(keys > thr[:, 0:1], 1.0, 0.0), axis=1, keepdims=True)
            tri = _tri_incl()

            def mask_blk(jb, running):
                cols = pl.ds(pl.multiple_of(jb * LANES, LANES), LANES)
                sel, running = _tie_mask(keys_scr[:, cols], thr, need, running, tri)
                mb_scr[:, cols] = jnp.where(sel, 0.0, NEG)
                return running

            lax.fori_loop(0, n_blocks, mask_blk, jnp.zeros((n_new, LANES), F32))

        lg = logit_scr[...] + jnp.concatenate([mb_scr[...]] * HEADS, axis=0)
        m = jnp.max(lg, axis=1, keepdims=True)
        p = jnp.exp(lg - m)
        pb = (p * (1.0 / jnp.sum(p, axis=1, keepdims=True))).astype(BF16)
        out_ref[0] = _nt(vt_scr[...], pb)


def _dsa_sample(qb, qi, small, kbf, vbf, smallbf, cache_k, cache_v, cache_kidx, page_table, bias_s, n_new, topk):
    dbs, n_pages = page_table.shape
    past_len = n_pages * PAGE
    npp = PAGES_PER_STEP
    assert n_pages % npp == 0 and n_new == SUBLANES
    n_page_steps = n_pages // npp
    n_blocks = n_pages + 1
    n_pool = cache_k.shape[0]
    qi_rows = qi.reshape(dbs, n_new, HEADS, HD).transpose(0, 2, 1, 3).reshape(dbs, HEADS * n_new, HD)
    q4 = qb.reshape(dbs, n_new, HEADS, HD).transpose(0, 2, 1, 3)
    eye = jnp.eye(HEADS, dtype=qb.dtype)
    qbd = (q4[:, :, :, None, :] * eye[None, :, None, :, None]).reshape(dbs, HEADS * n_new, B_DIM)
    w_rows = small.reshape(dbs, n_new, LANES)[:, :, S_W:S_W + HEADS].transpose(0, 2, 1).reshape(dbs, HEADS * n_new, 1)
    w_rows = jnp.broadcast_to(w_rows, (dbs, HEADS * n_new, LANES))

    def page_spec(shape, i):
        def imap(b, s, pt):
            return (pt[b, jnp.minimum(s, n_page_steps - 1) * npp + i],) + (0,) * len(shape)
        return pl.BlockSpec((1,) + shape, imap)

    per_seq = lambda r, w: pl.BlockSpec((1, r, w), lambda b, s, pt: (b, 0, 0))
    in_specs = ([page_spec((HD, PAGE), i) for i in range(npp)] + [page_spec((HEADS, HD, PAGE), i) for i in range(npp)]
                + [page_spec((HEADS, HD, PAGE), i) for i in range(npp)]
                + [per_seq(HEADS * n_new, HD), per_seq(HEADS * n_new, B_DIM), per_seq(HEADS * n_new, LANES),
                   per_seq(n_new, B_DIM), per_seq(n_new, B_DIM), per_seq(n_new, LANES),
                   pl.BlockSpec(bias_s.shape, lambda b, s, pt: (0, 0), pipeline_mode=pl.Buffered(1))])
    grid_spec = pltpu.PrefetchScalarGridSpec(
        num_scalar_prefetch=1,
        grid=(dbs, n_page_steps + 1),
        in_specs=in_specs,
        out_specs=pl.BlockSpec((1, B_DIM, HEADS * n_new), lambda b, s, pt: (b, 0, 0)),
        scratch_shapes=[pltpu.VMEM((n_new, n_blocks * LANES), I32), pltpu.VMEM((n_new, n_blocks * LANES), F32),
                        pltpu.VMEM((HEADS * n_new, n_blocks * LANES), F32),
                        pltpu.VMEM((B_DIM, n_blocks * LANES), BF16)],
    )
    ck = cache_k.transpose(0, 2, 3, 1)
    cv = cache_v.transpose(0, 2, 3, 1)
    cki = cache_kidx.transpose(0, 2, 1)
    r3 = lambda a: a.reshape(dbs, n_new, a.shape[-1])
    out_t = pl.pallas_call(
        functools.partial(_dsa_sample_body, n_new=n_new, past_len=past_len, topk=topk),
        grid_spec=grid_spec,
        out_shape=jax.ShapeDtypeStruct((dbs, B_DIM, HEADS * n_new), F32),
        compiler_params=pltpu.CompilerParams(dimension_semantics=("parallel", "arbitrary"),
                                             vmem_limit_bytes=VMEM_LIMIT),
        name="dsa_sample",
    )(page_table, *([cki] * npp), *([ck] * npp), *([cv] * npp),
      qi_rows, qbd, w_rows, r3(kbf), r3(vbf), r3(smallbf), bias_s)
    o5 = out_t.reshape(dbs, HEADS, HD, HEADS, n_new)
    o = jnp.stack([o5[:, h, :, h, :] for h in range(HEADS)], axis=1)
    return o.transpose(0, 3, 1, 2).reshape(dbs, n_new, B_DIM).astype(BF16)


def _rms(x, g):
    return x * lax.rsqrt(jnp.mean(x * x, axis=-1, keepdims=True) + EPS) * g


def _mlp_body(x_ref, ma_ref, ob_ref, woa_ref, wob_ref, gffn_ref, wg_ref, wu_ref, wd_ref, gfin_ref, y_ref):
    h = x_ref[...] + (_mm(ma_ref[...], woa_ref[...]) + _mm(ob_ref[...], wob_ref[...]))
    hn = _rms(h, gffn_ref[...]).astype(BF16)
    act = _silu(_mm(hn, wg_ref[...])) * _mm(hn, wu_ref[...])
    y_ref[...] = _rms(h + _mm(act.astype(BF16), wd_ref[...]), gfin_ref[...])


def _mlp(x2d, mix_a, o_b, w_out, g_ffn, w_gate, w_up, w_down, g_final, tm):
    n = x2d.shape[0]
    assert n % tm == 0
    row = lambda w: pl.BlockSpec((tm, w), lambda i: (i, 0))
    woa, wob = w_out[:A_DIM].astype(BF16), w_out[A_DIM:].astype(BF16)
    return pl.pallas_call(
        _mlp_body,
        grid=(n // tm,),
        in_specs=[row(D_MODEL), row(A_DIM), row(B_DIM), _const_spec(woa.shape), _const_spec(wob.shape),
                  _const_spec((1, D_MODEL)), _const_spec(w_gate.shape), _const_spec(w_up.shape),
                  _const_spec(w_down.shape), _const_spec((1, D_MODEL))],
        out_specs=row(D_MODEL),
        out_shape=jax.ShapeDtypeStruct((n, D_MODEL), F32),
        compiler_params=pltpu.CompilerParams(dimension_semantics=("parallel",), vmem_limit_bytes=VMEM_LIMIT),
        name="mlp",
    )(x2d, mix_a, o_b, woa, wob, g_ffn.reshape(1, D_MODEL), w_gate.astype(BF16), w_up.astype(BF16),
      w_down.astype(BF16), g_final.reshape(1, D_MODEL))


def kernel(x_prompt, x_sample, cache_k, cache_v, cache_kidx, state_conv, state_ssm, page_table, meta_tokens, norm_mix, w_in, conv_w, a_log, dt_bias, o_norm, w_out, rel_bias, norm_ffn, w_gate, w_up, w_down, norm_final):
    assert w_in.shape[0] == 1, "single-layer stack"
    bsz, seq, _ = x_prompt.shape
    dbs, n_new, _ = x_sample.shape
    t_pad = FRONT + seq
    lo = FRONT - N_META
    past_len = page_table.shape[1] * PAGE
    topk_p = min(TOPK_MAX, seq // 4)
    topk_s = min(TOPK_MAX, (past_len + n_new) // 4)
    w_parts = _split_w_in(w_in[0])
    mlp_w = (w_out[0], norm_ffn[0], w_gate[0], w_up[0], w_down[0], norm_final)
    gdn_w = (conv_w[0], a_log[0], dt_bias[0], o_norm[0])

    meta = jnp.broadcast_to(meta_tokens.astype(F32)[None], (bsz, N_META, D_MODEL))
    xp = jnp.concatenate([jnp.zeros((bsz, lo, D_MODEL), F32), meta, x_prompt], axis=1).reshape(bsz * t_pad, D_MODEL)
    qkva, z, qb, k, v, _, vbf, qi, small, _, small_t, small_tbf, kt = _proj(xp, norm_mix[0], w_parts, 512)
    mix_a, ssm_p = _gdn_prompt(qkva, z, small, small_t, *gdn_w, bsz, t_pad, lo)
    bias_tab = _bias_tab(rel_bias, t_pad // LANES)
    o_b = _dsa_prompt(qb, qi, small, kt, vbf, small_tbf, bias_tab, bsz, t_pad, lo, topk_p)
    y_p = _mlp(x_prompt.reshape(bsz * seq, D_MODEL), mix_a.reshape(bsz * seq, A_DIM), o_b.reshape(bsz * seq, B_DIM),
               *mlp_w, 512).reshape(bsz, seq, D_MODEL)
    real = lambda a: a.reshape(bsz, t_pad, a.shape[-1])[:, lo:]
    k_p = real(k).reshape(1, bsz, N_META + seq, HEADS, HD)
    v_p = real(v).reshape(1, bsz, N_META + seq, HEADS, HD)
    kidx_p = real(small)[:, :, S_KIDX:S_KIDX + HD][None]
    conv_p = real(qkva)[:, -(CONV_WIDTH - 1):][None]

    xs = x_sample.reshape(dbs * n_new, D_MODEL)
    (qkva_s, z_s, qb_s, k_s, v_s, kbf_s, vbf_s, qi_s, small_s, smallbf_s, small_t_s, _, _) = _proj(
        xs, norm_mix[0], w_parts, 512)
    mix_a_s, ssm_s = _gdn_sample(qkva_s, z_s, small_s, small_t_s, state_conv[0], state_ssm[0], *gdn_w, n_new)
    bias_s = _bias_sample(rel_bias, past_len, past_len // PAGE + 1)
    o_b_s = _dsa_sample(qb_s, qi_s, small_s, kbf_s, vbf_s, smallbf_s, cache_k[0], cache_v[0], cache_kidx[0], page_table,
                        bias_s, n_new, topk_s)
    y_s = _mlp(xs, mix_a_s, o_b_s.reshape(dbs * n_new, B_DIM), *mlp_w, 512).reshape(dbs, n_new, D_MODEL)
    k_sn = k_s.reshape(1, dbs, n_new, HEADS, HD)
    v_sn = v_s.reshape(1, dbs, n_new, HEADS, HD)
    kidx_s = small_s.reshape(dbs, n_new, LANES)[:, :, S_KIDX:S_KIDX + HD][None]
    conv_s = jnp.concatenate([state_conv[0].astype(F32), qkva_s.reshape(dbs, n_new, 3 * A_DIM)],
                             axis=1)[:, -(CONV_WIDTH - 1):][None]
    return (y_p, y_s, k_p, v_p, kidx_p, conv_p, ssm_p[None], k_sn, v_sn, kidx_s, conv_s, ssm_s[None])
```
